```python
import math
import jax, jax.numpy as jnp
from jax import lax
import numpy as np

D_MODEL = 1024
BATCH = 1
SEQ = 16384
DEPTH = 4

N_MIXERS = 2
N_LAYERS_A = (DEPTH + 1) // 2
N_LAYERS_B = DEPTH // 2
A_CHUNK = 128
A_D_FFN = 4 * D_MODEL
A_WIDTH = A_D_FFN // 2
A_GROUPS = 8
A_GROUP_DIM = A_WIDTH // A_GROUPS
B_HEAD_DIM = 64
B_HEADS = D_MODEL // (2 * B_HEAD_DIM)
B_V_DIM = 2 * B_HEAD_DIM
Q_BLOCK = 128
REL_BUCKETS = 32
REL_MAX_DISTANCE = 128
FFN_DIM = ((8 * D_MODEL // 3 + 127) // 128) * 128
CONV_WIDTH = 3
DEEPNORM_ALPHA = (2 * DEPTH) ** 0.25
DEEPNORM_BETA = (8 * DEPTH) ** -0.25
LN_EPS = 1e-5

kernel_name = "hybrid_gmlp_diffattn_convffn_deepnorm"


def _layer_norm(x, g, b):
    xf = x.astype(jnp.float32)
    mu = xf.mean(-1, keepdims=True)
    var = jnp.square(xf - mu).mean(-1, keepdims=True)
    y = (xf - mu) * lax.rsqrt(var + LN_EPS)
    return (y * g.astype(jnp.float32) + b.astype(jnp.float32)).astype(x.dtype)


def _t5_bucket(rel):
    half = REL_BUCKETS // 2
    max_exact = half // 2
    bucket = jnp.where(rel > 0, half, 0)
    n = jnp.abs(rel)
    nf = jnp.maximum(n, 1).astype(jnp.float32)
    large = max_exact + (jnp.log(nf / max_exact) / math.log(REL_MAX_DISTANCE / max_exact)
                         * (half - max_exact)).astype(jnp.int32)
    large = jnp.minimum(large, half - 1)
    return bucket + jnp.where(n < max_exact, n, large)


def _lambda_init(layer_idx):
    return 0.8 - 0.6 * math.exp(-0.3 * layer_idx)


def _chunked_spatial_gating(x, w_in, norm_g, norm_b, w_s, b_s, w_out):
    B, S, _ = x.shape
    h = jax.nn.gelu(x @ w_in, approximate=False)
    u, v = jnp.split(h, 2, axis=-1)
    v = _layer_norm(v, norm_g, norm_b)
    v = v.reshape(B, S // A_CHUNK, A_CHUNK, A_GROUPS, A_GROUP_DIM)
    mixed = jnp.einsum('gts,bcsgd->bctgd', w_s, v) + b_s.T[:, :, None]
    y = u * mixed.reshape(B, S, A_WIDTH)
    return y @ w_out


def _diff_attention(x, positions, rel_table, w_qkv, lam_params, subln_g, w_out, lambda_init):
    B, S, _ = x.shape
    qk_dim = 2 * B_HEADS * B_HEAD_DIM
    q, k, v = jnp.split(x @ w_qkv, [qk_dim, 2 * qk_dim], axis=-1)
    q = q.reshape(B, S, B_HEADS, 2, B_HEAD_DIM) * (B_HEAD_DIM ** -0.5)
    k = k.reshape(B, S, B_HEADS, 2, B_HEAD_DIM)
    v = v.reshape(B, S, B_HEADS, B_V_DIM)
    lp = lam_params.astype(jnp.float32)
    lam = jnp.exp(jnp.sum(lp[0] * lp[1])) - jnp.exp(jnp.sum(lp[2] * lp[3])) + lambda_init
    table = rel_table.astype(jnp.float32)
    n_blk = S // Q_BLOCK
    q_blocks = q.reshape(B, n_blk, Q_BLOCK, B_HEADS, 2, B_HEAD_DIM).swapaxes(0, 1)
    p_blocks = positions.reshape(B, n_blk, Q_BLOCK).swapaxes(0, 1)

    def block(args):
        qb, pb = args
        logits = jnp.einsum('bqhjd,bkhjd->bhjqk', qb, k).astype(jnp.float32)
        rel = positions[:, None, :] - pb[:, :, None]
        bias = jnp.transpose(table[_t5_bucket(rel)], (0, 3, 1, 2))
        probs = jax.nn.softmax(logits + bias[:, :, None], axis=-1)
        attn = probs[:, :, 0] - lam * probs[:, :, 1]
        return jnp.einsum('bhqk,bkhe->bqhe', attn.astype(v.dtype), v)

    o = lax.map(block, (q_blocks, p_blocks))
    o = o.swapaxes(0, 1).reshape(B, S, B_HEADS, B_V_DIM)
    of = o.astype(jnp.float32)
    of = of * lax.rsqrt(jnp.mean(jnp.square(of), axis=-1, keepdims=True) + LN_EPS)
    of = of * subln_g.astype(jnp.float32) * (1.0 - lambda_init)
    return of.astype(x.dtype).reshape(B, S, B_HEADS * B_V_DIM) @ w_out


def _conv_glu_ffn(x, w_in, conv_w, conv_b, w_down):
    gate, up = jnp.split(x @ w_in, 2, axis=-1)
    gate = lax.conv_general_dilated(
        gate, conv_w[:, None, :], window_strides=(1,),
        padding=[(CONV_WIDTH // 2, CONV_WIDTH // 2)],
        dimension_numbers=('NWC', 'WIO', 'NWC'),
        feature_group_count=FFN_DIM) + conv_b
    return (jax.nn.gelu(gate, approximate=False) * up) @ w_down


def setup_inputs(seed: int = 0) -> dict:
    key = jax.random.key(seed)
    ks = jax.random.split(key, 20)
    nrm = jax.random.normal
    f32 = jnp.float32
    x = nrm(ks[0], (BATCH, SEQ, D_MODEL), f32)
    positions = jnp.broadcast_to(jnp.arange(SEQ, dtype=jnp.int32)[None], (BATCH, SEQ))
    rel_bias_table = 0.5 * nrm(ks[1], (REL_BUCKETS, B_HEADS), f32)
    a_w_in = nrm(ks[2], (N_LAYERS_A, D_MODEL, A_D_FFN), f32) * D_MODEL ** -0.5
    a_norm_g = 1.0 + 0.02 * nrm(ks[3], (N_LAYERS_A, A_WIDTH), f32)
    a_norm_b = 0.02 * nrm(ks[4], (N_LAYERS_A, A_WIDTH), f32)
    a_w_s = nrm(ks[5], (N_LAYERS_A, A_GROUPS, A_CHUNK, A_CHUNK), f32) * A_CHUNK ** -0.5
    a_b_s = 1.0 + 0.1 * nrm(ks[6], (N_LAYERS_A, A_GROUPS, A_CHUNK), f32)
    a_w_out = nrm(ks[7], (N_LAYERS_A, A_WIDTH, D_MODEL), f32) * (A_WIDTH ** -0.5 * DEEPNORM_BETA)
    b_w_qkv = nrm(ks[8], (N_LAYERS_B, D_MODEL, 3 * D_MODEL), f32) * D_MODEL ** -0.5
    b_lambda = 0.1 * nrm(ks[9], (N_LAYERS_B, 4, B_HEAD_DIM), f32)
    b_subln_g = 1.0 + 0.02 * nrm(ks[10], (N_LAYERS_B, B_V_DIM), f32)
    b_w_out = nrm(ks[11], (N_LAYERS_B, B_HEADS * B_V_DIM, D_MODEL), f32) * (D_MODEL ** -0.5 * DEEPNORM_BETA)
    f_w_in = nrm(ks[12], (DEPTH, D_MODEL, 2 * FFN_DIM), f32) * D_MODEL ** -0.5
    f_conv_w = nrm(ks[13], (DEPTH, CONV_WIDTH, FFN_DIM), f32) * CONV_WIDTH ** -0.5
    f_conv_b = 0.02 * nrm(ks[14], (DEPTH, FFN_DIM), f32)
    f_w_down = nrm(ks[15], (DEPTH, FFN_DIM, D_MODEL), f32) * (FFN_DIM ** -0.5 * DEEPNORM_BETA)
    ln_g = 1.0 + 0.02 * nrm(ks[16], (DEPTH, 2, D_MODEL), f32)
    ln_b = 0.02 * nrm(ks[17], (DEPTH, 2, D_MODEL), f32)
    return {"x": x, "positions": positions, "rel_bias_table": rel_bias_table,
            "a_w_in": a_w_in, "a_norm_g": a_norm_g, "a_norm_b": a_norm_b,
            "a_w_s": a_w_s, "a_b_s": a_b_s, "a_w_out": a_w_out,
            "b_w_qkv": b_w_qkv, "b_lambda": b_lambda, "b_subln_g": b_subln_g,
            "b_w_out": b_w_out, "f_w_in": f_w_in, "f_conv_w": f_conv_w,
            "f_conv_b": f_conv_b, "f_w_down": f_w_down, "ln_g": ln_g, "ln_b": ln_b}


def reference(x, positions, rel_bias_table, a_w_in, a_norm_g, a_norm_b, a_w_s, a_b_s,
              a_w_out, b_w_qkv, b_lambda, b_subln_g, b_w_out, f_w_in, f_conv_w,
              f_conv_b, f_w_down, ln_g, ln_b):
    for i in range(DEPTH):
        j = i // N_MIXERS
        if i % N_MIXERS == 0:
            h = _chunked_spatial_gating(x, a_w_in[j], a_norm_g[j], a_norm_b[j],
                                        a_w_s[j], a_b_s[j], a_w_out[j])
        else:
            h = _diff_attention(x, positions, rel_bias_table, b_w_qkv[j], b_lambda[j],
                                b_subln_g[j], b_w_out[j], _lambda_init(i))
        x = _layer_norm(DEEPNORM_ALPHA * x + h, ln_g[i, 0], ln_b[i, 0])
        h = _conv_glu_ffn(x, f_w_in[i], f_conv_w[i], f_conv_b[i], f_w_down[i])
        x = _layer_norm(DEEPNORM_ALPHA * x + h, ln_g[i, 1], ln_b[i, 1])
    return x
```

```python
import functools
import math

import jax
import jax.numpy as jnp
from jax import lax
from jax.experimental import pallas as pl
from jax.experimental.pallas import tpu as pltpu

F32 = jnp.float32
BF16 = jnp.bfloat16

DEPTH = 4
A_CHUNK = 128
A_GROUPS = 8
HEAD_DIM = 64
V_DIM = 2 * HEAD_DIM
REL_BUCKETS = 32
REL_FAR = 128
LN_EPS = 1e-5
DEEPNORM_ALPHA = (2 * DEPTH) ** 0.25
BUCKET_STEPS = (12, 16, 23, 32, 46, 64, 91)

LANES = 128
SUBLANES = 8
BF16_ROWS = 16
VMEM_LIMIT = 56 * 1024 * 1024

TM_GMLP = 256
TM_QKV = 512
TQ = 256
TK = 512
TM_PROJ = 512
TM_FFN = 512
FFN_CHUNK = 256
ONES_ROWS = BF16_ROWS


def _lambda_init(layer_idx):
    return 0.8 - 0.6 * math.exp(-0.3 * layer_idx)


def _gelu(x):
    return 0.5 * x * (1.0 + lax.erf(x * (1.0 / math.sqrt(2.0))))


def _layer_norm(z, g, b):
    mu = jnp.mean(z, axis=-1, keepdims=True)
    zc = z - mu
    var = jnp.mean(zc * zc, axis=-1, keepdims=True)
    return zc * lax.rsqrt(var + LN_EPS) * g + b


def _dot(a, b):
    return jnp.dot(a, b, preferred_element_type=F32)


def _dot_nt(a, b):
    return lax.dot_general(a, b, (((1,), (1,)), ((), ())), preferred_element_type=F32)


def _const_spec(shape):
    nd = len(shape)
    return pl.BlockSpec(shape, lambda *_: (0,) * nd, pipeline_mode=pl.Buffered(1))


def _params(*sem):
    return pltpu.CompilerParams(dimension_semantics=sem, vmem_limit_bytes=VMEM_LIMIT)


def _gmlp_kernel(x_ref, wu_ref, wv_ref, ng_ref, nb_ref, ws_ref, bs_ref, wo_ref,
                 lg_ref, lb_ref, o_ref):
    x = x_ref[...]
    xb = x.astype(BF16)
    tm = x.shape[0]
    gd = wu_ref.shape[2]
    v = _gelu(_dot(xb, wv_ref[...]))
    v = _layer_norm(v, ng_ref[...], nb_ref[...]).astype(BF16)
    acc = DEEPNORM_ALPHA * x
    for g in range(A_GROUPS):
        u = _gelu(_dot(xb, wu_ref[g]))
        mixed = []
        for c in range(tm // A_CHUNK):
            vc = v[c * A_CHUNK:(c + 1) * A_CHUNK, g * gd:(g + 1) * gd]
            mixed.append(_dot(ws_ref[g], vc) + bs_ref[g])
        y = (u * jnp.concatenate(mixed, axis=0)).astype(BF16)
        acc = acc + _dot(y, wo_ref[g])
    o_ref[...] = _layer_norm(acc, lg_ref[...], lb_ref[...])


def _gmlp_layer(x, w_in, norm_g, norm_b, w_s, b_s, w_out, ln_g, ln_b):
    S, D = x.shape
    W = w_in.shape[1] // 2
    gd = W // A_GROUPS
    tm = TM_GMLP
    wu = w_in[:, :W].astype(BF16).reshape(D, A_GROUPS, gd).transpose(1, 0, 2)
    wv = w_in[:, W:].astype(BF16)
    wo = w_out.astype(BF16).reshape(A_GROUPS, gd, D)
    return pl.pallas_call(
        _gmlp_kernel,
        grid=(S // tm,),
        in_specs=[
            pl.BlockSpec((tm, D), lambda i: (i, 0)),
            _const_spec((A_GROUPS, D, gd)),
            _const_spec((D, W)),
            _const_spec((1, W)),
            _const_spec((1, W)),
            _const_spec((A_GROUPS, A_CHUNK, A_CHUNK)),
            _const_spec((A_GROUPS, A_CHUNK, 1)),
            _const_spec((A_GROUPS, gd, D)),
            _const_spec((1, D)),
            _const_spec((1, D)),
        ],
        out_specs=pl.BlockSpec((tm, D), lambda i: (i, 0)),
        out_shape=jax.ShapeDtypeStruct((S, D), F32),
        compiler_params=_params("parallel"),
        name="gmlp_layer",
    )(x, wu, wv, norm_g.reshape(1, W), norm_b.reshape(1, W), w_s.astype(BF16),
      b_s.reshape(A_GROUPS, A_CHUNK, 1), wo, ln_g.reshape(1, D), ln_b.reshape(1, D))


def _pos_stats_kernel(p_ref, mn_ref, mx_ref):
    p = p_ref[...]
    mn_ref[...] = jnp.min(p, axis=1, keepdims=True)
    mx_ref[...] = jnp.max(p, axis=1, keepdims=True)


def _pos_stats(positions):
    nb = positions.shape[0] // LANES
    mn, mx = pl.pallas_call(
        _pos_stats_kernel,
        out_shape=(jax.ShapeDtypeStruct((nb, 1), jnp.int32),) * 2,
        name="pos_stats",
    )(positions.reshape(nb, LANES))
    return mn.reshape(nb), mx.reshape(nb)


def _qkv_kernel(x_ref, wqt_ref, wk_ref, wvt_ref, qt_ref, k_ref, vt_ref):
    xb = x_ref[...].astype(BF16)
    tm = xb.shape[0]
    qt_ref[...] = _dot_nt(wqt_ref[...], xb).astype(BF16)
    k_ref[...] = _dot(xb, wk_ref[...]).astype(BF16)
    vt = _dot_nt(wvt_ref[...], xb).astype(BF16)
    heads = vt.shape[0] // V_DIM
    vt_ref[:, 0, 0:V_DIM, :] = vt.reshape(heads, V_DIM, tm)
    vt_ref[:, 0, V_DIM:, :] = jnp.ones((heads, ONES_ROWS, tm), BF16)


def _qkv(x, w_qkv):
    S, D = x.shape
    tm = TM_QKV
    heads = D // V_DIM
    wqt = (w_qkv[:, :D] * (HEAD_DIM ** -0.5)).T.astype(BF16)
    wk = w_qkv[:, D:2 * D].astype(BF16)
    wvt = w_qkv[:, 2 * D:].T.astype(BF16)
    return pl.pallas_call(
        _qkv_kernel,
        grid=(S // tm,),
        in_specs=[
            pl.BlockSpec((tm, D), lambda i: (i, 0)),
            _const_spec((D, D)),
            _const_spec((D, D)),
            _const_spec((D, D)),
        ],
        out_specs=[
            pl.BlockSpec((D, tm), lambda i: (0, i)),
            pl.BlockSpec((tm, D), lambda i: (i, 0)),
            pl.BlockSpec((heads, 1, V_DIM + ONES_ROWS, tm), lambda i: (0, i, 0, 0)),
        ],
        out_shape=[
            jax.ShapeDtypeStruct((D, S), BF16),
            jax.ShapeDtypeStruct((S, D), BF16),
            jax.ShapeDtypeStruct((heads, S // tm, V_DIM + ONES_ROWS, tm), BF16),
        ],
        compiler_params=_params("parallel"),
        name="qkv_proj",
    )(x, wqt, wk, wvt)


def _bias_tile(posk_row, posq_row, table_row):
    tk = posk_row.shape[1]
    tq = posq_row.shape[1]
    table_sq = jnp.broadcast_to(table_row, (LANES, LANES))
    rows = []
    for c in range(tk // LANES):
        pk = posk_row[:, c * LANES:(c + 1) * LANES]
        pk_col = jnp.transpose(jnp.broadcast_to(pk, (LANES, LANES)))
        cols = []
        for d in range(tq // LANES):
            rel = pk_col - posq_row[:, d * LANES:(d + 1) * LANES]
            n = jnp.abs(rel)
            large = jnp.full(rel.shape, REL_BUCKETS // 4, jnp.int32)
            for step in BUCKET_STEPS:
                large = large + jnp.where(n >= step, 1, 0)
            bucket = jnp.where(n < REL_BUCKETS // 4, n, large)
            bucket = bucket + jnp.where(rel > 0, REL_BUCKETS // 2, 0)
            cols.append(jnp.take_along_axis(table_sq, bucket, axis=1))
        rows.append(jnp.concatenate(cols, axis=1))
    return jnp.concatenate(rows, axis=0)


def _attn_kernel(bmin_ref, bmax_ref, tbl_ref,
                 qt_ref, k_ref, vt_ref, posq_ref, posk_ref, tblv_ref, lam_ref, sg_ref,
                 o_ref, acc1, acc2, m1, m2, *, nk, lambda_init):
    h = pl.program_id(0)
    qi = pl.program_id(1)
    tq = qt_ref.shape[1]
    tk = vt_ref.shape[3]
    heads = pl.num_programs(0)

    q = qt_ref[...]
    row = lax.broadcasted_iota(jnp.int32, q.shape, 0)
    zero = jnp.zeros_like(q)
    q_maps = (jnp.where(row < HEAD_DIM, q, zero), jnp.where(row >= HEAD_DIM, q, zero))
    accs = (acc1, acc2)
    ms = (m1, m2)

    qmin = bmin_ref[qi * (tq // LANES)]
    qmax = bmax_ref[qi * (tq // LANES)]
    for r in range(1, tq // LANES):
        qmin = jnp.minimum(qmin, bmin_ref[qi * (tq // LANES) + r])
        qmax = jnp.maximum(qmax, bmax_ref[qi * (tq // LANES) + r])
    bias_before = tbl_ref[(REL_BUCKETS // 2 - 1) * heads + h]
    bias_after = tbl_ref[(REL_BUCKETS - 1) * heads + h]

    for acc, m in zip(accs, ms):
        acc[...] = jnp.zeros_like(acc)
        m[...] = jnp.full(m.shape, -1e30, F32)

    def tile(kt, const_bias, bias):
        kk = k_ref[pl.ds(pl.multiple_of(kt * tk, tk), tk), :]
        vv = vt_ref[0, kt]
        for qm, acc, m in zip(q_maps, accs, ms):
            s = _dot(kk, qm)
            if bias is not None:
                s = s + bias
            m_old = m[...]
            m_new = jnp.maximum(m_old, jnp.max(s, axis=0, keepdims=True) + const_bias)
            p = jnp.exp(s - (m_new - const_bias)).astype(BF16)
            acc[...] = jnp.exp(m_old - m_new) * acc[...] + _dot(vv, p)
            m[...] = m_new

    def body(kt, carry):
        kmin = bmin_ref[kt * (tk // LANES)]
        kmax = bmax_ref[kt * (tk // LANES)]
        for r in range(1, tk // LANES):
            kmin = jnp.minimum(kmin, bmin_ref[kt * (tk // LANES) + r])
            kmax = jnp.maximum(kmax, bmax_ref[kt * (tk // LANES) + r])
        all_after = kmin - qmax >= REL_FAR
        all_before = kmax - qmin <= -REL_FAR

        def far():
            tile(kt, jnp.where(all_after, bias_after, bias_before), None)

        def near():
            table_row = tblv_ref[pl.ds(h, 1), :]
            tile(kt, 0.0, _bias_tile(posk_ref[kt], posq_ref[...], table_row))

        lax.cond(jnp.logical_or(all_after, all_before), far, near)
        return carry

    lax.fori_loop(0, nk, body, 0)

    lp = lam_ref[...]
    lam = (jnp.exp(jnp.sum(lp[0:1] * lp[1:2], axis=1, keepdims=True))
           - jnp.exp(jnp.sum(lp[2:3] * lp[3:4], axis=1, keepdims=True)) + lambda_init)
    o1 = acc1[0:V_DIM, :] / acc1[V_DIM:V_DIM + 1, :]
    o2 = acc2[0:V_DIM, :] / acc2[V_DIM:V_DIM + 1, :]
    o = o1 - lam * o2
    o = o * lax.rsqrt(jnp.mean(o * o, axis=0, keepdims=True) + LN_EPS)
    o = o * sg_ref[...] * (1.0 - lambda_init)
    o_ref[...] = jnp.transpose(o).astype(BF16)


def _attention(qt, k, vt, positions, bmin, bmax, rel_table, lam_params, subln_g, lambda_init):
    D, S = qt.shape
    heads = D // V_DIM
    tq, tk = TQ, TK
    nk = S // tk
    table_rows = jnp.zeros((heads, LANES), F32).at[:, :REL_BUCKETS].set(rel_table.T)
    kernel = functools.partial(_attn_kernel, nk=nk, lambda_init=lambda_init)
    grid_spec = pltpu.PrefetchScalarGridSpec(
        num_scalar_prefetch=3,
        grid=(heads, S // tq),
        in_specs=[
            pl.BlockSpec((V_DIM, tq), lambda h, i, *_: (h, i)),
            pl.BlockSpec((S, V_DIM), lambda h, i, *_: (0, h)),
            pl.BlockSpec((1, nk, V_DIM + ONES_ROWS, tk), lambda h, i, *_: (h, 0, 0, 0)),
            pl.BlockSpec((1, tq), lambda h, i, *_: (0, i)),
            pl.BlockSpec((nk, 1, tk), lambda h, i, *_: (0, 0, 0)),
            pl.BlockSpec((heads, LANES), lambda h, i, *_: (0, 0)),
            pl.BlockSpec((4, HEAD_DIM), lambda h, i, *_: (0, 0)),
            pl.BlockSpec((V_DIM, 1), lambda h, i, *_: (0, 0)),
        ],
        out_specs=pl.BlockSpec((tq, V_DIM), lambda h, i, *_: (i, h)),
        scratch_shapes=[
            pltpu.VMEM((V_DIM + ONES_ROWS, tq), F32),
            pltpu.VMEM((V_DIM + ONES_ROWS, tq), F32),
            pltpu.VMEM((1, tq), F32),
            pltpu.VMEM((1, tq), F32),
        ],
    )
    return pl.pallas_call(
        kernel,
        grid_spec=grid_spec,
        out_shape=jax.ShapeDtypeStruct((S, D), BF16),
        compiler_params=_params("parallel", "parallel"),
        name="diff_attention",
    )(bmin, bmax, rel_table.reshape(-1),
      qt, k, vt, positions.reshape(1, S), positions.reshape(nk, 1, tk), table_rows,
      lam_params, subln_g.reshape(V_DIM, 1))


def _proj_ln_kernel(a_ref, x_ref, w_ref, lg_ref, lb_ref, o_ref):
    z = DEEPNORM_ALPHA * x_ref[...] + _dot(a_ref[...], w_ref[...])
    o_ref[...] = _layer_norm(z, lg_ref[...], lb_ref[...])


def _proj_ln(a, x, w, ln_g, ln_b):
    S, D = x.shape
    K = a.shape[1]
    tm = TM_PROJ
    return pl.pallas_call(
        _proj_ln_kernel,
        grid=(S // tm,),
        in_specs=[
            pl.BlockSpec((tm, K), lambda i: (i, 0)),
            pl.BlockSpec((tm, D), lambda i: (i, 0)),
            _const_spec((K, D)),
            _const_spec((1, D)),
            _const_spec((1, D)),
        ],
        out_specs=pl.BlockSpec((tm, D), lambda i: (i, 0)),
        out_shape=jax.ShapeDtypeStruct((S, D), F32),
        compiler_params=_params("parallel"),
        name="attn_out_proj",
    )(a, x, w.astype(BF16), ln_g.reshape(1, D), ln_b.reshape(1, D))


def _attn_layer(x, positions, bmin, bmax, rel_table, w_qkv, lam_params, subln_g, w_out,
                lambda_init, ln_g, ln_b):
    qt, k, vt = _qkv(x, w_qkv)
    o = _attention(qt, k, vt, positions, bmin, bmax, rel_table, lam_params, subln_g, lambda_init)
    return _proj_ln(o, x, w_out, ln_g, ln_b)


def _ffn_kernel(x_ref, xp_ref, xn_ref, wg_ref, wu_ref, cw_ref, cb_ref, wd_ref,
                lg_ref, lb_ref, o_ref, g_scr):
    i = pl.program_id(0)
    x = x_ref[...]
    tm = x.shape[0]
    halo = xp_ref.shape[0]
    xp = jnp.where(i > 0, xp_ref[...], 0.0)
    xn = jnp.where(i < pl.num_programs(0) - 1, xn_ref[...], 0.0)
    xe = jnp.concatenate([xp, x, xn], axis=0).astype(BF16)
    xb = x.astype(BF16)
    acc = DEEPNORM_ALPHA * x
    for c in range(wg_ref.shape[0]):
        g_scr[...] = _dot(xe, wg_ref[c])
        cw = cw_ref[c]
        gate = (g_scr[halo - 1:halo - 1 + tm, :] * cw[0:1]
                + g_scr[halo:halo + tm, :] * cw[1:2]
                + g_scr[halo + 1:halo + 1 + tm, :] * cw[2:3]
                + cb_ref[c])
        up = _dot(xb, wu_ref[c])
        hidden = (_gelu(gate) * up).astype(BF16)
        acc = acc + _dot(hidden, wd_ref[c])
    o_ref[...] = _layer_norm(acc, lg_ref[...], lb_ref[...])


def _ffn_layer(x, w_in, conv_w, conv_b, w_down, ln_g, ln_b):
    S, D = x.shape
    F = w_down.shape[0]
    tm, fc, halo = TM_FFN, FFN_CHUNK, SUBLANES
    nc = F // fc
    wg = w_in[:, :F].astype(BF16).reshape(D, nc, fc).transpose(1, 0, 2)
    wu = w_in[:, F:].astype(BF16).reshape(D, nc, fc).transpose(1, 0, 2)
    wd = w_down.astype(BF16).reshape(nc, fc, D)
    cw = conv_w.reshape(conv_w.shape[0], nc, fc).transpose(1, 0, 2)
    cb = conv_b.reshape(nc, 1, fc)
    blocks_per_tile = tm // halo
    last_halo_block = S // halo - 1
    return pl.pallas_call(
        _ffn_kernel,
        grid=(S // tm,),
        in_specs=[
            pl.BlockSpec((tm, D), lambda i: (i, 0)),
            pl.BlockSpec((halo, D), lambda i: (jnp.maximum(i * blocks_per_tile - 1, 0), 0)),
            pl.BlockSpec((halo, D),
                         lambda i: (jnp.minimum((i + 1) * blocks_per_tile, last_halo_block), 0)),
            _const_spec((nc, D, fc)),
            _const_spec((nc, D, fc)),
            _const_spec((nc, conv_w.shape[0], fc)),
            _const_spec((nc, 1, fc)),
            _const_spec((nc, fc, D)),
            _const_spec((1, D)),
            _const_spec((1, D)),
        ],
        out_specs=pl.BlockSpec((tm, D), lambda i: (i, 0)),
        out_shape=jax.ShapeDtypeStruct((S, D), F32),
        scratch_shapes=[pltpu.VMEM((tm + 2 * halo, fc), F32)],
        compiler_params=_params("parallel"),
        name="conv_glu_ffn",
    )(x, x, x, wg, wu, cw, cb, wd, ln_g.reshape(1, D), ln_b.reshape(1, D))


def kernel(x, positions, rel_bias_table, a_w_in, a_norm_g, a_norm_b, a_w_s, a_b_s, a_w_out,
           b_w_qkv, b_lambda, b_subln_g, b_w_out, f_w_in, f_conv_w, f_conv_b, f_w_down,
           ln_g, ln_b):
    B, S, D = x.shape
    outs = []
    for b in range(B):
        xs = x[b]
        pos = positions[b]
        bmin, bmax = _pos_stats(pos)
        for i in range(DEPTH):
            j = i // 2
            if i % 2 == 0:
                xs = _gmlp_layer(xs, a_w_in[j], a_norm_g[j], a_norm_b[j], a_w_s[j], a_b_s[j],
                                 a_w_out[j], ln_g[i, 0], ln_b[i, 0])
            else:
                xs = _attn_layer(xs, pos, bmin, bmax, rel_bias_table, b_w_qkv[j], b_lambda[j],
                                 b_subln_g[j], b_w_out[j], _lambda_init(i), ln_g[i, 0], ln_b[i, 0])
            xs = _ffn_layer(xs, f_w_in[i], f_conv_w[i], f_conv_b[i], f_w_down[i],
                            ln_g[i, 1], ln_b[i, 1])
        outs.append(xs)
    return jnp.stack(outs)
```

```python
import functools
import math

import jax
import jax.numpy as jnp
from jax import lax
from jax.experimental import pallas as pl
from jax.experimental.pallas import tpu as pltpu

F32 = jnp.float32
BF16 = jnp.bfloat16

DEPTH = 4
A_CHUNK = 128
A_GROUPS = 8
HEAD_DIM = 64
V_DIM = 2 * HEAD_DIM
REL_BUCKETS = 32
REL_FAR = 128
LN_EPS = 1e-5
LOG2_E = math.log2(math.e)
DEEPNORM_ALPHA = (2 * DEPTH) ** 0.25
BUCKET_STEPS = (12, 16, 23, 32, 46, 64, 91)

LANES = 128
SUBLANES = 8
BF16_ROWS = 16
VMEM_LIMIT = 56 * 1024 * 1024

TM_GMLP = 256
TM_QKV = 512
TQ = 256
TK = 512
TM_PROJ = 512
TM_FFN = 512
FFN_CHUNK = 256
ONES_ROWS = BF16_ROWS


def _lambda_init(layer_idx):
    return 0.8 - 0.6 * math.exp(-0.3 * layer_idx)


def _gelu(x):
    return 0.5 * x * (1.0 + lax.erf(x * (1.0 / math.sqrt(2.0))))


def _layer_norm(z, g, b):
    mu = jnp.mean(z, axis=-1, keepdims=True)
    zc = z - mu
    var = jnp.mean(zc * zc, axis=-1, keepdims=True)
    return zc * lax.rsqrt(var + LN_EPS) * g + b


def _dot(a, b):
    return jnp.dot(a, b, preferred_element_type=F32)


def _dot_nt(a, b):
    return lax.dot_general(a, b, (((1,), (1,)), ((), ())), preferred_element_type=F32)


def _const_spec(shape):
    nd = len(shape)
    return pl.BlockSpec(shape, lambda *_: (0,) * nd, pipeline_mode=pl.Buffered(1))


def _params(*sem):
    return pltpu.CompilerParams(dimension_semantics=sem, vmem_limit_bytes=VMEM_LIMIT)


def _gmlp_kernel(x_ref, wu_ref, wv_ref, ng_ref, nb_ref, ws_ref, bs_ref, wo_ref,
                 lg_ref, lb_ref, o_ref):
    x = x_ref[...]
    xb = x.astype(BF16)
    tm = x.shape[0]
    gd = wu_ref.shape[2]
    v = _gelu(_dot(xb, wv_ref[...]))
    v = _layer_norm(v, ng_ref[...], nb_ref[...]).astype(BF16)
    acc = DEEPNORM_ALPHA * x
    for g in range(A_GROUPS):
        u = _gelu(_dot(xb, wu_ref[g]))
        mixed = []
        for c in range(tm // A_CHUNK):
            vc = v[c * A_CHUNK:(c + 1) * A_CHUNK, g * gd:(g + 1) * gd]
            mixed.append(_dot(ws_ref[g], vc) + bs_ref[g])
        y = (u * jnp.concatenate(mixed, axis=0)).astype(BF16)
        acc = acc + _dot(y, wo_ref[g])
    o_ref[...] = _layer_norm(acc, lg_ref[...], lb_ref[...])


def _gmlp_layer(x, w_in, norm_g, norm_b, w_s, b_s, w_out, ln_g, ln_b):
    S, D = x.shape
    W = w_in.shape[1] // 2
    gd = W // A_GROUPS
    tm = TM_GMLP
    wu = w_in[:, :W].astype(BF16).reshape(D, A_GROUPS, gd).transpose(1, 0, 2)
    wv = w_in[:, W:].astype(BF16)
    wo = w_out.astype(BF16).reshape(A_GROUPS, gd, D)
    return pl.pallas_call(
        _gmlp_kernel,
        grid=(S // tm,),
        in_specs=[
            pl.BlockSpec((tm, D), lambda i: (i, 0)),
            _const_spec((A_GROUPS, D, gd)),
            _const_spec((D, W)),
            _const_spec((1, W)),
            _const_spec((1, W)),
            _const_spec((A_GROUPS, A_CHUNK, A_CHUNK)),
            _const_spec((A_GROUPS, A_CHUNK, 1)),
            _const_spec((A_GROUPS, gd, D)),
            _const_spec((1, D)),
            _const_spec((1, D)),
        ],
        out_specs=pl.BlockSpec((tm, D), lambda i: (i, 0)),
        out_shape=jax.ShapeDtypeStruct((S, D), F32),
        compiler_params=_params("parallel"),
        name="gmlp_layer",
    )(x, wu, wv, norm_g.reshape(1, W), norm_b.reshape(1, W), w_s.astype(BF16),
      b_s.reshape(A_GROUPS, A_CHUNK, 1), wo, ln_g.reshape(1, D), ln_b.reshape(1, D))


def _pos_stats_kernel(p_ref, mn_ref, mx_ref):
    p = p_ref[...]
    mn_ref[...] = jnp.min(p, axis=1, keepdims=True)
    mx_ref[...] = jnp.max(p, axis=1, keepdims=True)


def _pos_stats(positions):
    nb = positions.shape[0] // LANES
    mn, mx = pl.pallas_call(
        _pos_stats_kernel,
        out_shape=(jax.ShapeDtypeStruct((nb, 1), jnp.int32),) * 2,
        name="pos_stats",
    )(positions.reshape(nb, LANES))
    return mn.reshape(nb), mx.reshape(nb)


def _qkv_kernel(x_ref, wqt_ref, wk_ref, wvt_ref, qt_ref, k_ref, vt_ref):
    xb = x_ref[...].astype(BF16)
    tm = xb.shape[0]
    qt_ref[...] = _dot_nt(wqt_ref[...], xb).astype(BF16)
    k_ref[...] = _dot(xb, wk_ref[...]).astype(BF16)
    vt = _dot_nt(wvt_ref[...], xb).astype(BF16)
    heads = vt.shape[0] // V_DIM
    vt_ref[:, 0, 0:V_DIM, :] = vt.reshape(heads, V_DIM, tm)
    vt_ref[:, 0, V_DIM:, :] = jnp.ones((heads, ONES_ROWS, tm), BF16)


def _qkv(x, w_qkv):
    S, D = x.shape
    tm = TM_QKV
    heads = D // V_DIM
    wqt = (w_qkv[:, :D] * (HEAD_DIM ** -0.5 * LOG2_E)).T.astype(BF16)
    wk = w_qkv[:, D:2 * D].astype(BF16)
    wvt = w_qkv[:, 2 * D:].T.astype(BF16)
    return pl.pallas_call(
        _qkv_kernel,
        grid=(S // tm,),
        in_specs=[
            pl.BlockSpec((tm, D), lambda i: (i, 0)),
            _const_spec((D, D)),
            _const_spec((D, D)),
            _const_spec((D, D)),
        ],
        out_specs=[
            pl.BlockSpec((D, tm), lambda i: (0, i)),
            pl.BlockSpec((tm, D), lambda i: (i, 0)),
            pl.BlockSpec((heads, 1, V_DIM + ONES_ROWS, tm), lambda i: (0, i, 0, 0)),
        ],
        out_shape=[
            jax.ShapeDtypeStruct((D, S), BF16),
            jax.ShapeDtypeStruct((S, D), BF16),
            jax.ShapeDtypeStruct((heads, S // tm, V_DIM + ONES_ROWS, tm), BF16),
        ],
        compiler_params=_params("parallel"),
        name="qkv_proj",
    )(x, wqt, wk, wvt)


def _bias_tile(posk_row, posq_row, table_row):
    tk = posk_row.shape[1]
    tq = posq_row.shape[1]
    table_sq = jnp.broadcast_to(table_row, (LANES, LANES))
    rows = []
    for c in range(tk // LANES):
        pk = posk_row[:, c * LANES:(c + 1) * LANES]
        pk_col = jnp.transpose(jnp.broadcast_to(pk, (LANES, LANES)))
        cols = []
        for d in range(tq // LANES):
            rel = pk_col - posq_row[:, d * LANES:(d + 1) * LANES]
            n = jnp.abs(rel)
            large = jnp.full(rel.shape, REL_BUCKETS // 4, jnp.int32)
            for step in BUCKET_STEPS:
                large = large + jnp.where(n >= step, 1, 0)
            bucket = jnp.where(n < REL_BUCKETS // 4, n, large)
            bucket = bucket + jnp.where(rel > 0, REL_BUCKETS // 2, 0)
            cols.append(jnp.take_along_axis(table_sq, bucket, axis=1))
        rows.append(jnp.concatenate(cols, axis=1))
    return jnp.concatenate(rows, axis=0)


def _attn_kernel(bmin_ref, bmax_ref, tbl_ref,
                 qt_ref, k_ref, vt_ref, posq_ref, posk_ref, tblv_ref, lam_ref, sg_ref,
                 o_ref, qcat, s_a, s_b, smax_a, smax_b, m_scr, acc_scr, *, nk, lambda_init):
    h = pl.program_id(0)
    qi = pl.program_id(1)
    tq = qt_ref.shape[1]
    tk = vt_ref.shape[3]
    heads = pl.num_programs(0)

    q = qt_ref[...]
    row = lax.broadcasted_iota(jnp.int32, q.shape, 0)
    zero = jnp.zeros_like(q)
    qcat[:, 0:tq] = jnp.where(row < HEAD_DIM, q, zero)
    qcat[:, tq:2 * tq] = jnp.where(row >= HEAD_DIM, q, zero)

    qmin = bmin_ref[qi * (tq // LANES)]
    qmax = bmax_ref[qi * (tq // LANES)]
    for r in range(1, tq // LANES):
        qmin = jnp.minimum(qmin, bmin_ref[qi * (tq // LANES) + r])
        qmax = jnp.maximum(qmax, bmax_ref[qi * (tq // LANES) + r])
    bias_before = tbl_ref[(REL_BUCKETS // 2 - 1) * heads + h]
    bias_after = tbl_ref[(REL_BUCKETS - 1) * heads + h]

    acc_scr[...] = jnp.zeros_like(acc_scr)
    m_scr[...] = jnp.full(m_scr.shape, -1e30, F32)

    def classify(kt):
        kmin = bmin_ref[kt * (tk // LANES)]
        kmax = bmax_ref[kt * (tk // LANES)]
        for r in range(1, tk // LANES):
            kmin = jnp.minimum(kmin, bmin_ref[kt * (tk // LANES) + r])
            kmax = jnp.maximum(kmax, bmax_ref[kt * (tk // LANES) + r])
        all_after = kmin - qmax >= REL_FAR
        all_before = kmax - qmin <= -REL_FAR
        near = jnp.logical_not(jnp.logical_or(all_after, all_before))
        const_bias = jnp.where(all_after, bias_after, jnp.where(all_before, bias_before, 0.0))
        return near, const_bias

    def logits(kt, s_ref, smax_ref):
        kk = k_ref[pl.ds(pl.multiple_of(kt * tk, tk), tk), :]
        s = _dot(kk, qcat[...])
        s_ref[...] = s
        smax_ref[...] = jnp.max(s, axis=0, keepdims=True)

    def add_near_bias(kt, s_ref, smax_ref):
        bias = _bias_tile(posk_ref[kt], posq_ref[...], tblv_ref[pl.ds(h, 1), :])
        s = s_ref[...] + jnp.concatenate([bias, bias], axis=1)
        s_ref[...] = s
        smax_ref[...] = jnp.max(s, axis=0, keepdims=True)

    def softmax_update(kt, s_ref, smax_ref, const_bias):
        m_old = m_scr[...]
        m_new = jnp.maximum(m_old, smax_ref[...] + const_bias)
        p = jnp.exp2(s_ref[...] - (m_new - const_bias)).astype(BF16)
        acc_scr[...] = jnp.exp2(m_old - m_new) * acc_scr[...] + _dot(vt_ref[0, kt], p)
        m_scr[...] = m_new

    def stage(kt, cur, nxt, const_bias):
        kn = jnp.minimum(kt + 1, nk - 1)
        logits(kn, *nxt)
        softmax_update(kt, *cur, const_bias)
        near_next, bias_next = classify(kn)

        @pl.when(near_next)
        def _():
            add_near_bias(kn, *nxt)

        return bias_next

    buf_a = (s_a, smax_a)
    buf_b = (s_b, smax_b)
    near0, bias0 = classify(0)
    logits(0, *buf_a)

    @pl.when(near0)
    def _():
        add_near_bias(0, *buf_a)

    def body(j, const_bias):
        const_bias = stage(2 * j, buf_a, buf_b, const_bias)
        return stage(2 * j + 1, buf_b, buf_a, const_bias)

    lax.fori_loop(0, nk // 2, body, bias0)

    lp = lam_ref[...]
    lam = (jnp.exp(jnp.sum(lp[0:1] * lp[1:2], axis=1, keepdims=True))
           - jnp.exp(jnp.sum(lp[2:3] * lp[3:4], axis=1, keepdims=True)) + lambda_init)
    o1 = acc_scr[0:V_DIM, 0:tq] / acc_scr[V_DIM:V_DIM + 1, 0:tq]
    o2 = acc_scr[0:V_DIM, tq:2 * tq] / acc_scr[V_DIM:V_DIM + 1, tq:2 * tq]
    o = o1 - lam * o2
    o = o * lax.rsqrt(jnp.mean(o * o, axis=0, keepdims=True) + LN_EPS)
    o = o * sg_ref[...] * (1.0 - lambda_init)
    o_ref[...] = jnp.transpose(o).astype(BF16)


def _attention(qt, k, vt, positions, bmin, bmax, rel_table, lam_params, subln_g, lambda_init):
    D, S = qt.shape
    heads = D // V_DIM
    tq, tk = TQ, TK
    nk = S // tk
    table2 = rel_table.astype(F32) * LOG2_E
    table_rows = jnp.zeros((heads, LANES), F32).at[:, :REL_BUCKETS].set(table2.T)
    kernel = functools.partial(_attn_kernel, nk=nk, lambda_init=lambda_init)
    grid_spec = pltpu.PrefetchScalarGridSpec(
        num_scalar_prefetch=3,
        grid=(heads, S // tq),
        in_specs=[
            pl.BlockSpec((V_DIM, tq), lambda h, i, *_: (h, i)),
            pl.BlockSpec((S, V_DIM), lambda h, i, *_: (0, h)),
            pl.BlockSpec((1, nk, V_DIM + ONES_ROWS, tk), lambda h, i, *_: (h, 0, 0, 0)),
            pl.BlockSpec((1, tq), lambda h, i, *_: (0, i)),
            pl.BlockSpec((nk, 1, tk), lambda h, i, *_: (0, 0, 0)),
            pl.BlockSpec((heads, LANES), lambda h, i, *_: (0, 0)),
            pl.BlockSpec((4, HEAD_DIM), lambda h, i, *_: (0, 0)),
            pl.BlockSpec((V_DIM, 1), lambda h, i, *_: (0, 0)),
        ],
        out_specs=pl.BlockSpec((tq, V_DIM), lambda h, i, *_: (i, h)),
        scratch_shapes=[
            pltpu.VMEM((V_DIM, 2 * tq), BF16),
            pltpu.VMEM((tk, 2 * tq), F32),
            pltpu.VMEM((tk, 2 * tq), F32),
            pltpu.VMEM((1, 2 * tq), F32),
            pltpu.VMEM((1, 2 * tq), F32),
            pltpu.VMEM((1, 2 * tq), F32),
            pltpu.VMEM((V_DIM + ONES_ROWS, 2 * tq), F32),
        ],
    )
    return pl.pallas_call(
        kernel,
        grid_spec=grid_spec,
        out_shape=jax.ShapeDtypeStruct((S, D), BF16),
        compiler_params=_params("parallel", "parallel"),
        name="diff_attention",
    )(bmin, bmax, table2.reshape(-1),
      qt, k, vt, positions.reshape(1, S), positions.reshape(nk, 1, tk), table_rows,
      lam_params, subln_g.reshape(V_DIM, 1))


def _proj_ln_kernel(a_ref, x_ref, w_ref, lg_ref, lb_ref, o_ref):
    z = DEEPNORM_ALPHA * x_ref[...] + _dot(a_ref[...], w_ref[...])
    o_ref[...] = _layer_norm(z, lg_ref[...], lb_ref[...])


def _proj_ln(a, x, w, ln_g, ln_b):
    S, D = x.shape
    K = a.shape[1]
    tm = TM_PROJ
    return pl.pallas_call(
        _proj_ln_kernel,
        grid=(S // tm,),
        in_specs=[
            pl.BlockSpec((tm, K), lambda i: (i, 0)),
            pl.BlockSpec((tm, D), lambda i: (i, 0)),
            _const_spec((K, D)),
            _const_spec((1, D)),
            _const_spec((1, D)),
        ],
        out_specs=pl.BlockSpec((tm, D), lambda i: (i, 0)),
        out_shape=jax.ShapeDtypeStruct((S, D), F32),
        compiler_params=_params("parallel"),
        name="attn_out_proj",
    )(a, x, w.astype(BF16), ln_g.reshape(1, D), ln_b.reshape(1, D))


def _attn_layer(x, positions, bmin, bmax, rel_table, w_qkv, lam_params, subln_g, w_out,
                lambda_init, ln_g, ln_b):
    qt, k, vt = _qkv(x, w_qkv)
    o = _attention(qt, k, vt, positions, bmin, bmax, rel_table, lam_params, subln_g, lambda_init)
    return _proj_ln(o, x, w_out, ln_g, ln_b)


def _ffn_kernel(x_ref, xp_ref, xn_ref, wg_ref, wu_ref, cw_ref, cb_ref, wd_ref,
                lg_ref, lb_ref, o_ref, g_scr):
    i = pl.program_id(0)
    x = x_ref[...]
    tm = x.shape[0]
    halo = xp_ref.shape[0]
    xp = jnp.where(i > 0, xp_ref[...], 0.0)
    xn = jnp.where(i < pl.num_programs(0) - 1, xn_ref[...], 0.0)
    xe = jnp.concatenate([xp, x, xn], axis=0).astype(BF16)
    xb = x.astype(BF16)
    acc = DEEPNORM_ALPHA * x
    for c in range(wg_ref.shape[0]):
        g_scr[...] = _dot(xe, wg_ref[c])
        cw = cw_ref[c]
        gate = (g_scr[halo - 1:halo - 1 + tm, :] * cw[0:1]
                + g_scr[halo:halo + tm, :] * cw[1:2]
                + g_scr[halo + 1:halo + 1 + tm, :] * cw[2:3]
                + cb_ref[c])
        up = _dot(xb, wu_ref[c])
        hidden = (_gelu(gate) * up).astype(BF16)
        acc = acc + _dot(hidden, wd_ref[c])
    o_ref[...] = _layer_norm(acc, lg_ref[...], lb_ref[...])


def _ffn_layer(x, w_in, conv_w, conv_b, w_down, ln_g, ln_b):
    S, D = x.shape
    F = w_down.shape[0]
    tm, fc, halo = TM_FFN, FFN_CHUNK, SUBLANES
    nc = F // fc
    wg = w_in[:, :F].astype(BF16).reshape(D, nc, fc).transpose(1, 0, 2)
    wu = w_in[:, F:].astype(BF16).reshape(D, nc, fc).transpose(1, 0, 2)
    wd = w_down.astype(BF16).reshape(nc, fc, D)
    cw = conv_w.reshape(conv_w.shape[0], nc, fc).transpose(1, 0, 2)
    cb = conv_b.reshape(nc, 1, fc)
    blocks_per_tile = tm // halo
    last_halo_block = S // halo - 1
    return pl.pallas_call(
        _ffn_kernel,
        grid=(S // tm,),
        in_specs=[
            pl.BlockSpec((tm, D), lambda i: (i, 0)),
            pl.BlockSpec((halo, D), lambda i: (jnp.maximum(i * blocks_per_tile - 1, 0), 0)),
            pl.BlockSpec((halo, D),
                         lambda i: (jnp.minimum((i + 1) * blocks_per_tile, last_halo_block), 0)),
            _const_spec((nc, D, fc)),
            _const_spec((nc, D, fc)),
            _const_spec((nc, conv_w.shape[0], fc)),
            _const_spec((nc, 1, fc)),
            _const_spec((nc, fc, D)),
            _const_spec((1, D)),
            _const_spec((1, D)),
        ],
        out_specs=pl.BlockSpec((tm, D), lambda i: (i, 0)),
        out_shape=jax.ShapeDtypeStruct((S, D), F32),
        scratch_shapes=[pltpu.VMEM((tm + 2 * halo, fc), F32)],
        compiler_params=_params("parallel"),
        name="conv_glu_ffn",
    )(x, x, x, wg, wu, cw, cb, wd, ln_g.reshape(1, D), ln_b.reshape(1, D))


def kernel(x, positions, rel_bias_table, a_w_in, a_norm_g, a_norm_b, a_w_s, a_b_s, a_w_out,
           b_w_qkv, b_lambda, b_subln_g, b_w_out, f_w_in, f_conv_w, f_conv_b, f_w_down,
           ln_g, ln_b):
    B, S, D = x.shape
    outs = []
    for b in range(B):
        xs = x[b]
        pos = positions[b]
        bmin, bmax = _pos_stats(pos)
        for i in range(DEPTH):
            j = i // 2
            if i % 2 == 0:
                xs = _gmlp_layer(xs, a_w_in[j], a_norm_g[j], a_norm_b[j], a_w_s[j], a_b_s[j],
                                 a_w_out[j], ln_g[i, 0], ln_b[i, 0])
            else:
                xs = _attn_layer(xs, pos, bmin, bmax, rel_bias_table, b_w_qkv[j], b_lambda[j],
                                 b_subln_g[j], b_w_out[j], _lambda_init(i), ln_g[i, 0], ln_b[i, 0])
            xs = _ffn_layer(xs, f_w_in[i], f_conv_w[i], f_conv_b[i], f_w_down[i],
                            ln_g[i, 1], ln_b[i, 1])
        outs.append(xs)
    return jnp.stack(outs)
```

```python
import functools
import math

import jax
import jax.numpy as jnp
from jax import lax
from jax.experimental import pallas as pl
from jax.experimental.pallas import tpu as pltpu

F32 = jnp.float32
BF16 = jnp.bfloat16

DEPTH = 4
A_CHUNK = 128
A_GROUPS = 8
HEAD_DIM = 64
V_DIM = 2 * HEAD_DIM
REL_BUCKETS = 32
REL_FAR = 128
LN_EPS = 1e-5
LOG2_E = math.log2(math.e)
DEEPNORM_ALPHA = (2 * DEPTH) ** 0.25
BUCKET_STEPS = (12, 16, 23, 32, 46, 64, 91)

LANES = 128
SUBLANES = 8
BF16_ROWS = 16
VMEM_LIMIT = 56 * 1024 * 1024

TM_GMLP = 256
TM_QKV = 512
TQ = 256
TK = 512
TM_PROJ = 512
TM_FFN = 512
FFN_CHUNK = 256
STAGE_GROUP = 4
ONES_ROWS = BF16_ROWS


def _lambda_init(layer_idx):
    return 0.8 - 0.6 * math.exp(-0.3 * layer_idx)


def _gelu(x):
    return 0.5 * x * (1.0 + lax.erf(x * (1.0 / math.sqrt(2.0))))


def _layer_norm(z, g, b):
    mu = jnp.mean(z, axis=-1, keepdims=True)
    zc = z - mu
    var = jnp.mean(zc * zc, axis=-1, keepdims=True)
    return zc * lax.rsqrt(var + LN_EPS) * g + b


def _dot(a, b):
    return jnp.dot(a, b, preferred_element_type=F32)


def _dot_nt(a, b):
    return lax.dot_general(a, b, (((1,), (1,)), ((), ())), preferred_element_type=F32)


def _const_spec(shape):
    nd = len(shape)
    return pl.BlockSpec(shape, lambda *_: (0,) * nd, pipeline_mode=pl.Buffered(1))


def _params(*sem):
    return pltpu.CompilerParams(dimension_semantics=sem, vmem_limit_bytes=VMEM_LIMIT)


def _gmlp_kernel(x_ref, wu_ref, wv_ref, ng_ref, nb_ref, ws_ref, bs_ref, wo_ref,
                 lg_ref, lb_ref, o_ref):
    x = x_ref[...]
    xb = x.astype(BF16)
    tm = x.shape[0]
    gd = wu_ref.shape[2]
    v = _gelu(_dot(xb, wv_ref[...]))
    v = _layer_norm(v, ng_ref[...], nb_ref[...]).astype(BF16)
    acc = DEEPNORM_ALPHA * x
    for g in range(A_GROUPS):
        u = _gelu(_dot(xb, wu_ref[g]))
        mixed = []
        for c in range(tm // A_CHUNK):
            vc = v[c * A_CHUNK:(c + 1) * A_CHUNK, g * gd:(g + 1) * gd]
            mixed.append(_dot(ws_ref[g], vc) + bs_ref[g])
        y = (u * jnp.concatenate(mixed, axis=0)).astype(BF16)
        acc = acc + _dot(y, wo_ref[g])
    o_ref[...] = _layer_norm(acc, lg_ref[...], lb_ref[...])


def _gmlp_layer(x, w_in, norm_g, norm_b, w_s, b_s, w_out, ln_g, ln_b):
    S, D = x.shape
    W = w_in.shape[1] // 2
    gd = W // A_GROUPS
    tm = TM_GMLP
    wu = w_in[:, :W].astype(BF16).reshape(D, A_GROUPS, gd).transpose(1, 0, 2)
    wv = w_in[:, W:].astype(BF16)
    wo = w_out.astype(BF16).reshape(A_GROUPS, gd, D)
    return pl.pallas_call(
        _gmlp_kernel,
        grid=(S // tm,),
        in_specs=[
            pl.BlockSpec((tm, D), lambda i: (i, 0)),
            _const_spec((A_GROUPS, D, gd)),
            _const_spec((D, W)),
            _const_spec((1, W)),
            _const_spec((1, W)),
            _const_spec((A_GROUPS, A_CHUNK, A_CHUNK)),
            _const_spec((A_GROUPS, A_CHUNK, 1)),
            _const_spec((A_GROUPS, gd, D)),
            _const_spec((1, D)),
            _const_spec((1, D)),
        ],
        out_specs=pl.BlockSpec((tm, D), lambda i: (i, 0)),
        out_shape=jax.ShapeDtypeStruct((S, D), F32),
        compiler_params=_params("parallel"),
        name="gmlp_layer",
    )(x, wu, wv, norm_g.reshape(1, W), norm_b.reshape(1, W), w_s.astype(BF16),
      b_s.reshape(A_GROUPS, A_CHUNK, 1), wo, ln_g.reshape(1, D), ln_b.reshape(1, D))


def _pos_stats_kernel(p_ref, mn_ref, mx_ref):
    p = p_ref[...]
    mn_ref[...] = jnp.min(p, axis=1, keepdims=True)
    mx_ref[...] = jnp.max(p, axis=1, keepdims=True)


def _pos_stats(positions):
    nb = positions.shape[0] // LANES
    mn, mx = pl.pallas_call(
        _pos_stats_kernel,
        out_shape=(jax.ShapeDtypeStruct((nb, 1), jnp.int32),) * 2,
        name="pos_stats",
    )(positions.reshape(nb, LANES))
    return mn.reshape(nb), mx.reshape(nb)


def _qkv_kernel(x_ref, wqt_ref, wk_ref, wvt_ref, qt_ref, k_ref, vt_ref):
    xb = x_ref[...].astype(BF16)
    tm = xb.shape[0]
    qt_ref[...] = _dot_nt(wqt_ref[...], xb).astype(BF16)
    k_ref[...] = _dot(xb, wk_ref[...]).astype(BF16)
    vt = _dot_nt(wvt_ref[...], xb).astype(BF16)
    heads = vt.shape[0] // V_DIM
    vt_ref[:, 0, 0:V_DIM, :] = vt.reshape(heads, V_DIM, tm)
    vt_ref[:, 0, V_DIM:, :] = jnp.ones((heads, ONES_ROWS, tm), BF16)


def _qkv(x, w_qkv):
    S, D = x.shape
    tm = TM_QKV
    heads = D // V_DIM
    wqt = (w_qkv[:, :D] * (HEAD_DIM ** -0.5 * LOG2_E)).T.astype(BF16)
    wk = w_qkv[:, D:2 * D].astype(BF16)
    wvt = w_qkv[:, 2 * D:].T.astype(BF16)
    return pl.pallas_call(
        _qkv_kernel,
        grid=(S // tm,),
        in_specs=[
            pl.BlockSpec((tm, D), lambda i: (i, 0)),
            _const_spec((D, D)),
            _const_spec((D, D)),
            _const_spec((D, D)),
        ],
        out_specs=[
            pl.BlockSpec((D, tm), lambda i: (0, i)),
            pl.BlockSpec((tm, D), lambda i: (i, 0)),
            pl.BlockSpec((heads, 1, V_DIM + ONES_ROWS, tm), lambda i: (0, i, 0, 0)),
        ],
        out_shape=[
            jax.ShapeDtypeStruct((D, S), BF16),
            jax.ShapeDtypeStruct((S, D), BF16),
            jax.ShapeDtypeStruct((heads, S // tm, V_DIM + ONES_ROWS, tm), BF16),
        ],
        compiler_params=_params("parallel"),
        name="qkv_proj",
    )(x, wqt, wk, wvt)


def _bias_tile(posk_row, posq_row, table_row):
    tk = posk_row.shape[1]
    tq = posq_row.shape[1]
    table_sq = jnp.broadcast_to(table_row, (LANES, LANES))
    rows = []
    for c in range(tk // LANES):
        pk = posk_row[:, c * LANES:(c + 1) * LANES]
        pk_col = jnp.transpose(jnp.broadcast_to(pk, (LANES, LANES)))
        cols = []
        for d in range(tq // LANES):
            rel = pk_col - posq_row[:, d * LANES:(d + 1) * LANES]
            n = jnp.abs(rel)
            large = jnp.full(rel.shape, REL_BUCKETS // 4, jnp.int32)
            for step in BUCKET_STEPS:
                large = large + jnp.where(n >= step, 1, 0)
            bucket = jnp.where(n < REL_BUCKETS // 4, n, large)
            bucket = bucket + jnp.where(rel > 0, REL_BUCKETS // 2, 0)
            cols.append(jnp.take_along_axis(table_sq, bucket, axis=1))
        rows.append(jnp.concatenate(cols, axis=1))
    return jnp.concatenate(rows, axis=0)


def _attn_kernel(bmin_ref, bmax_ref, tbl_ref,
                 qt_ref, k_ref, vt_ref, posq_ref, posk_ref, tblv_ref, lam_ref, sg_ref,
                 o_ref, qcat, s_a, s_b, smax_a, smax_b, m_scr, acc_scr, *, nk, lambda_init):
    h = pl.program_id(0)
    qi = pl.program_id(1)
    tq = qt_ref.shape[1]
    tk = vt_ref.shape[3]
    heads = pl.num_programs(0)

    q = qt_ref[...]
    row = lax.broadcasted_iota(jnp.int32, q.shape, 0)
    zero = jnp.zeros_like(q)
    qcat[:, 0:tq] = jnp.where(row < HEAD_DIM, q, zero)
    qcat[:, tq:2 * tq] = jnp.where(row >= HEAD_DIM, q, zero)

    qmin = bmin_ref[qi * (tq // LANES)]
    qmax = bmax_ref[qi * (tq // LANES)]
    for r in range(1, tq // LANES):
        qmin = jnp.minimum(qmin, bmin_ref[qi * (tq // LANES) + r])
        qmax = jnp.maximum(qmax, bmax_ref[qi * (tq // LANES) + r])
    bias_before = tbl_ref[(REL_BUCKETS // 2 - 1) * heads + h]
    bias_after = tbl_ref[(REL_BUCKETS - 1) * heads + h]

    acc_scr[...] = jnp.zeros_like(acc_scr)
    m_scr[...] = jnp.full(m_scr.shape, -1e30, F32)

    def classify(kt):
        kmin = bmin_ref[kt * (tk // LANES)]
        kmax = bmax_ref[kt * (tk // LANES)]
        for r in range(1, tk // LANES):
            kmin = jnp.minimum(kmin, bmin_ref[kt * (tk // LANES) + r])
            kmax = jnp.maximum(kmax, bmax_ref[kt * (tk // LANES) + r])
        all_after = kmin - qmax >= REL_FAR
        all_before = kmax - qmin <= -REL_FAR
        near = jnp.logical_not(jnp.logical_or(all_after, all_before))
        const_bias = jnp.where(all_after, bias_after, jnp.where(all_before, bias_before, 0.0))
        return near, const_bias

    def logits(kt, s_ref, smax_ref):
        kk = k_ref[pl.ds(pl.multiple_of(kt * tk, tk), tk), :]
        s = _dot(kk, qcat[...])
        s_ref[...] = s
        smax_ref[...] = jnp.max(s, axis=0, keepdims=True)

    def add_near_bias(kt, s_ref, smax_ref):
        bias = _bias_tile(posk_ref[kt], posq_ref[...], tblv_ref[pl.ds(h, 1), :])
        s = s_ref[...] + jnp.concatenate([bias, bias], axis=1)
        s_ref[...] = s
        smax_ref[...] = jnp.max(s, axis=0, keepdims=True)

    def softmax_update(kt, s_ref, smax_ref, const_bias):
        m_old = m_scr[...]
        m_new = jnp.maximum(m_old, smax_ref[...] + const_bias)
        p = jnp.exp2(s_ref[...] - (m_new - const_bias)).astype(BF16)
        acc_scr[...] = jnp.exp2(m_old - m_new) * acc_scr[...] + _dot(vt_ref[0, kt], p)
        m_scr[...] = m_new

    def stage(kt, cur, nxt, const_bias, fix_next):
        kn = jnp.minimum(kt + 1, nk - 1)
        logits(kn, *nxt)
        softmax_update(kt, *cur, const_bias)
        near_next, bias_next = classify(kn)
        if fix_next:
            @pl.when(near_next)
            def _():
                add_near_bias(kn, *nxt)
        return near_next, bias_next

    buf_a = (s_a, smax_a)
    buf_b = (s_b, smax_b)
    near0, bias0 = classify(0)
    logits(0, *buf_a)

    @pl.when(near0)
    def _():
        add_near_bias(0, *buf_a)

    def pair(kt, const_bias):
        _, const_bias = stage(kt, buf_a, buf_b, const_bias, True)
        _, const_bias = stage(kt + 1, buf_b, buf_a, const_bias, True)
        return const_bias

    def group(j, const_bias):
        base = j * STAGE_GROUP
        inner_far = jnp.bool_(True)
        for i in range(1, STAGE_GROUP):
            inner_far = jnp.logical_and(inner_far, jnp.logical_not(classify(base + i)[0]))

        def branch_free():
            c = const_bias
            for i in range(STAGE_GROUP):
                cur, nxt = (buf_a, buf_b) if i % 2 == 0 else (buf_b, buf_a)
                _, c = stage(base + i, cur, nxt, c, i == STAGE_GROUP - 1)
            return c

        def checked():
            return lax.fori_loop(0, STAGE_GROUP // 2, lambda t, c: pair(base + 2 * t, c), const_bias)

        return lax.cond(inner_far, branch_free, checked)

    lax.fori_loop(0, nk // STAGE_GROUP, group, bias0)

    lp = lam_ref[...]
    lam = (jnp.exp(jnp.sum(lp[0:1] * lp[1:2], axis=1, keepdims=True))
           - jnp.exp(jnp.sum(lp[2:3] * lp[3:4], axis=1, keepdims=True)) + lambda_init)
    o1 = acc_scr[0:V_DIM, 0:tq] / acc_scr[V_DIM:V_DIM + 1, 0:tq]
    o2 = acc_scr[0:V_DIM, tq:2 * tq] / acc_scr[V_DIM:V_DIM + 1, tq:2 * tq]
    o = o1 - lam * o2
    o = o * lax.rsqrt(jnp.mean(o * o, axis=0, keepdims=True) + LN_EPS)
    o = o * sg_ref[...] * (1.0 - lambda_init)
    o_ref[...] = jnp.transpose(o).astype(BF16)


def _attention(qt, k, vt, positions, bmin, bmax, rel_table, lam_params, subln_g, lambda_init):
    D, S = qt.shape
    heads = D // V_DIM
    tq, tk = TQ, TK
    nk = S // tk
    table2 = rel_table.astype(F32) * LOG2_E
    table_rows = jnp.zeros((heads, LANES), F32).at[:, :REL_BUCKETS].set(table2.T)
    kernel = functools.partial(_attn_kernel, nk=nk, lambda_init=lambda_init)
    grid_spec = pltpu.PrefetchScalarGridSpec(
        num_scalar_prefetch=3,
        grid=(heads, S // tq),
        in_specs=[
            pl.BlockSpec((V_DIM, tq), lambda h, i, *_: (h, i)),
            pl.BlockSpec((S, V_DIM), lambda h, i, *_: (0, h)),
            pl.BlockSpec((1, nk, V_DIM + ONES_ROWS, tk), lambda h, i, *_: (h, 0, 0, 0)),
            pl.BlockSpec((1, tq), lambda h, i, *_: (0, i)),
            pl.BlockSpec((nk, 1, tk), lambda h, i, *_: (0, 0, 0)),
            pl.BlockSpec((heads, LANES), lambda h, i, *_: (0, 0)),
            pl.BlockSpec((4, HEAD_DIM), lambda h, i, *_: (0, 0)),
            pl.BlockSpec((V_DIM, 1), lambda h, i, *_: (0, 0)),
        ],
        out_specs=pl.BlockSpec((tq, V_DIM), lambda h, i, *_: (i, h)),
        scratch_shapes=[
            pltpu.VMEM((V_DIM, 2 * tq), BF16),
            pltpu.VMEM((tk, 2 * tq), F32),
            pltpu.VMEM((tk, 2 * tq), F32),
            pltpu.VMEM((1, 2 * tq), F32),
            pltpu.VMEM((1, 2 * tq), F32),
            pltpu.VMEM((1, 2 * tq), F32),
            pltpu.VMEM((V_DIM + ONES_ROWS, 2 * tq), F32),
        ],
    )
    return pl.pallas_call(
        kernel,
        grid_spec=grid_spec,
        out_shape=jax.ShapeDtypeStruct((S, D), BF16),
        compiler_params=_params("parallel", "parallel"),
        name="diff_attention",
    )(bmin, bmax, table2.reshape(-1),
      qt, k, vt, positions.reshape(1, S), positions.reshape(nk, 1, tk), table_rows,
      lam_params, subln_g.reshape(V_DIM, 1))


def _proj_ln_kernel(a_ref, x_ref, w_ref, lg_ref, lb_ref, o_ref):
    z = DEEPNORM_ALPHA * x_ref[...] + _dot(a_ref[...], w_ref[...])
    o_ref[...] = _layer_norm(z, lg_ref[...], lb_ref[...])


def _proj_ln(a, x, w, ln_g, ln_b):
    S, D = x.shape
    K = a.shape[1]
    tm = TM_PROJ
    return pl.pallas_call(
        _proj_ln_kernel,
        grid=(S // tm,),
        in_specs=[
            pl.BlockSpec((tm, K), lambda i: (i, 0)),
            pl.BlockSpec((tm, D), lambda i: (i, 0)),
            _const_spec((K, D)),
            _const_spec((1, D)),
            _const_spec((1, D)),
        ],
        out_specs=pl.BlockSpec((tm, D), lambda i: (i, 0)),
        out_shape=jax.ShapeDtypeStruct((S, D), F32),
        compiler_params=_params("parallel"),
        name="attn_out_proj",
    )(a, x, w.astype(BF16), ln_g.reshape(1, D), ln_b.reshape(1, D))


def _attn_layer(x, positions, bmin, bmax, rel_table, w_qkv, lam_params, subln_g, w_out,
                lambda_init, ln_g, ln_b):
    qt, k, vt = _qkv(x, w_qkv)
    o = _attention(qt, k, vt, positions, bmin, bmax, rel_table, lam_params, subln_g, lambda_init)
    return _proj_ln(o, x, w_out, ln_g, ln_b)


def _ffn_kernel(x_ref, xp_ref, xn_ref, wg_ref, wu_ref, cw_ref, cb_ref, wd_ref,
                lg_ref, lb_ref, o_ref, g_scr):
    i = pl.program_id(0)
    x = x_ref[...]
    tm = x.shape[0]
    halo = xp_ref.shape[0]
    xp = jnp.where(i > 0, xp_ref[...], 0.0)
    xn = jnp.where(i < pl.num_programs(0) - 1, xn_ref[...], 0.0)
    xe = jnp.concatenate([xp, x, xn], axis=0).astype(BF16)
    xb = x.astype(BF16)
    acc = DEEPNORM_ALPHA * x
    for c in range(wg_ref.shape[0]):
        g_scr[...] = _dot(xe, wg_ref[c])
        cw = cw_ref[c]
        gate = (g_scr[halo - 1:halo - 1 + tm, :] * cw[0:1]
                + g_scr[halo:halo + tm, :] * cw[1:2]
                + g_scr[halo + 1:halo + 1 + tm, :] * cw[2:3]
                + cb_ref[c])
        up = _dot(xb, wu_ref[c])
        hidden = (_gelu(gate) * up).astype(BF16)
        acc = acc + _dot(hidden, wd_ref[c])
    o_ref[...] = _layer_norm(acc, lg_ref[...], lb_ref[...])


def _ffn_layer(x, w_in, conv_w, conv_b, w_down, ln_g, ln_b):
    S, D = x.shape
    F = w_down.shape[0]
    tm, fc, halo = TM_FFN, FFN_CHUNK, SUBLANES
    nc = F // fc
    wg = w_in[:, :F].astype(BF16).reshape(D, nc, fc).transpose(1, 0, 2)
    wu = w_in[:, F:].astype(BF16).reshape(D, nc, fc).transpose(1, 0, 2)
    wd = w_down.astype(BF16).reshape(nc, fc, D)
    cw = conv_w.reshape(conv_w.shape[0], nc, fc).transpose(1, 0, 2)
    cb = conv_b.reshape(nc, 1, fc)
    blocks_per_tile = tm // halo
    last_halo_block = S // halo - 1
    return pl.pallas_call(
        _ffn_kernel,
        grid=(S // tm,),
        in_specs=[
            pl.BlockSpec((tm, D), lambda i: (i, 0)),
            pl.BlockSpec((halo, D), lambda i: (jnp.maximum(i * blocks_per_tile - 1, 0), 0)),
            pl.BlockSpec((halo, D),
                         lambda i: (jnp.minimum((i + 1) * blocks_per_tile, last_halo_block), 0)),
            _const_spec((nc, D, fc)),
            _const_spec((nc, D, fc)),
            _const_spec((nc, conv_w.shape[0], fc)),
            _const_spec((nc, 1, fc)),
            _const_spec((nc, fc, D)),
            _const_spec((1, D)),
            _const_spec((1, D)),
        ],
        out_specs=pl.BlockSpec((tm, D), lambda i: (i, 0)),
        out_shape=jax.ShapeDtypeStruct((S, D), F32),
        scratch_shapes=[pltpu.VMEM((tm + 2 * halo, fc), F32)],
        compiler_params=_params("parallel"),
        name="conv_glu_ffn",
    )(x, x, x, wg, wu, cw, cb, wd, ln_g.reshape(1, D), ln_b.reshape(1, D))


def kernel(x, positions, rel_bias_table, a_w_in, a_norm_g, a_norm_b, a_w_s, a_b_s, a_w_out,
           b_w_qkv, b_lambda, b_subln_g, b_w_out, f_w_in, f_conv_w, f_conv_b, f_w_down,
           ln_g, ln_b):
    B, S, D = x.shape
    outs = []
    for b in range(B):
        xs = x[b]
        pos = positions[b]
        bmin, bmax = _pos_stats(pos)
        for i in range(DEPTH):
            j = i // 2
            if i % 2 == 0:
                xs = _gmlp_layer(xs, a_w_in[j], a_norm_g[j], a_norm_b[j], a_w_s[j], a_b_s[j],
                                 a_w_out[j], ln_g[i, 0], ln_b[i, 0])
            else:
                xs = _attn_layer(xs, pos, bmin, bmax, rel_bias_table, b_w_qkv[j], b_lambda[j],
                                 b_subln_g[j], b_w_out[j], _lambda_init(i), ln_g[i, 0], ln_b[i, 0])
            xs = _ffn_layer(xs, f_w_in[i], f_conv_w[i], f_conv_b[i], f_w_down[i],
                            ln_g[i, 1], ln_b[i, 1])
        outs.append(xs)
    return jnp.stack(outs)
```

```python
import functools
import math

import jax
import jax.numpy as jnp
from jax import lax
from jax.experimental import pallas as pl
from jax.experimental.pallas import tpu as pltpu

F32 = jnp.float32
BF16 = jnp.bfloat16

DEPTH = 4
A_CHUNK = 128
A_GROUPS = 8
HEAD_DIM = 64
V_DIM = 2 * HEAD_DIM
REL_BUCKETS = 32
REL_FAR = 128
LN_EPS = 1e-5
LOG2_E = math.log2(math.e)
DEEPNORM_ALPHA = (2 * DEPTH) ** 0.25
BUCKET_STEPS = (12, 16, 23, 32, 46, 64, 91)

LANES = 128
SUBLANES = 8
BF16_ROWS = 16
VMEM_LIMIT = 56 * 1024 * 1024

TM_GMLP = 256
TM_QKV = 512
TQ = 256
TK = 512
TM_PROJ = 512
TM_FFN = 512
FFN_CHUNK = 1024
STAGE_GROUP = 8
MIN_STAGE_GROUP = 4
ONES_ROWS = BF16_ROWS


def _lambda_init(layer_idx):
    return 0.8 - 0.6 * math.exp(-0.3 * layer_idx)


def _gelu(x):
    return 0.5 * x * (1.0 + lax.erf(x * (1.0 / math.sqrt(2.0))))


def _layer_norm(z, g, b):
    mu = jnp.mean(z, axis=-1, keepdims=True)
    zc = z - mu
    var = jnp.mean(zc * zc, axis=-1, keepdims=True)
    return zc * lax.rsqrt(var + LN_EPS) * g + b


def _dot(a, b):
    return jnp.dot(a, b, preferred_element_type=F32)


def _dot_nt(a, b):
    return lax.dot_general(a, b, (((1,), (1,)), ((), ())), preferred_element_type=F32)


def _const_spec(shape):
    nd = len(shape)
    return pl.BlockSpec(shape, lambda *_: (0,) * nd, pipeline_mode=pl.Buffered(1))


def _params(*sem):
    return pltpu.CompilerParams(dimension_semantics=sem, vmem_limit_bytes=VMEM_LIMIT)


def _gmlp_kernel(x_ref, wi_ref, ng_ref, nb_ref, ws_ref, bs_ref, wo_ref, lg_ref, lb_ref, o_ref):
    x = x_ref[...]
    tm = x.shape[0]
    W = wo_ref.shape[0]
    gd = W // A_GROUPS
    hidden = _gelu(_dot(x.astype(BF16), wi_ref[...]))
    v = _layer_norm(hidden[:, W:], ng_ref[...], nb_ref[...]).astype(BF16)
    mixed = []
    for c in range(tm // A_CHUNK):
        row = []
        for g in range(A_GROUPS):
            vc = v[c * A_CHUNK:(c + 1) * A_CHUNK, g * gd:(g + 1) * gd]
            row.append(_dot(ws_ref[g], vc) + bs_ref[g])
        mixed.append(jnp.concatenate(row, axis=1))
    y = (hidden[:, :W] * jnp.concatenate(mixed, axis=0)).astype(BF16)
    z = DEEPNORM_ALPHA * x + _dot(y, wo_ref[...])
    o_ref[...] = _layer_norm(z, lg_ref[...], lb_ref[...])


def _gmlp_layer(x, w_in, norm_g, norm_b, w_s, b_s, w_out, ln_g, ln_b):
    S, D = x.shape
    W = w_out.shape[0]
    tm = TM_GMLP
    return pl.pallas_call(
        _gmlp_kernel,
        grid=(S // tm,),
        in_specs=[
            pl.BlockSpec((tm, D), lambda i: (i, 0)),
            _const_spec((D, 2 * W)),
            _const_spec((1, W)),
            _const_spec((1, W)),
            _const_spec((A_GROUPS, A_CHUNK, A_CHUNK)),
            _const_spec((A_GROUPS, A_CHUNK, 1)),
            _const_spec((W, D)),
            _const_spec((1, D)),
            _const_spec((1, D)),
        ],
        out_specs=pl.BlockSpec((tm, D), lambda i: (i, 0)),
        out_shape=jax.ShapeDtypeStruct((S, D), F32),
        compiler_params=_params("parallel"),
        name="gmlp_layer",
    )(x, w_in.astype(BF16), norm_g.reshape(1, W), norm_b.reshape(1, W), w_s.astype(BF16),
      b_s.reshape(A_GROUPS, A_CHUNK, 1), w_out.astype(BF16), ln_g.reshape(1, D), ln_b.reshape(1, D))


def _pos_stats_kernel(p_ref, mn_ref, mx_ref):
    p = p_ref[...]
    mn_ref[...] = jnp.min(p, axis=1, keepdims=True)
    mx_ref[...] = jnp.max(p, axis=1, keepdims=True)


def _pos_stats(positions):
    nb = positions.shape[0] // LANES
    mn, mx = pl.pallas_call(
        _pos_stats_kernel,
        out_shape=(jax.ShapeDtypeStruct((nb, 1), jnp.int32),) * 2,
        name="pos_stats",
    )(positions.reshape(nb, LANES))
    return mn.reshape(nb), mx.reshape(nb)


def _qkv_kernel(x_ref, wqt_ref, wk_ref, wvt_ref, qt_ref, k_ref, vt_ref):
    xb = x_ref[...].astype(BF16)
    tm = xb.shape[0]
    qt_ref[...] = _dot_nt(wqt_ref[...], xb).astype(BF16)
    k_ref[...] = _dot(xb, wk_ref[...]).astype(BF16)
    vt = _dot_nt(wvt_ref[...], xb).astype(BF16)
    heads = vt.shape[0] // V_DIM
    vt_ref[:, 0, 0:V_DIM, :] = vt.reshape(heads, V_DIM, tm)
    vt_ref[:, 0, V_DIM:, :] = jnp.ones((heads, ONES_ROWS, tm), BF16)


def _qkv(x, w_qkv):
    S, D = x.shape
    tm = TM_QKV
    heads = D // V_DIM
    wqt = (w_qkv[:, :D] * (HEAD_DIM ** -0.5 * LOG2_E)).T.astype(BF16)
    wk = w_qkv[:, D:2 * D].astype(BF16)
    wvt = w_qkv[:, 2 * D:].T.astype(BF16)
    return pl.pallas_call(
        _qkv_kernel,
        grid=(S // tm,),
        in_specs=[
            pl.BlockSpec((tm, D), lambda i: (i, 0)),
            _const_spec((D, D)),
            _const_spec((D, D)),
            _const_spec((D, D)),
        ],
        out_specs=[
            pl.BlockSpec((D, tm), lambda i: (0, i)),
            pl.BlockSpec((tm, D), lambda i: (i, 0)),
            pl.BlockSpec((heads, 1, V_DIM + ONES_ROWS, tm), lambda i: (0, i, 0, 0)),
        ],
        out_shape=[
            jax.ShapeDtypeStruct((D, S), BF16),
            jax.ShapeDtypeStruct((S, D), BF16),
            jax.ShapeDtypeStruct((heads, S // tm, V_DIM + ONES_ROWS, tm), BF16),
        ],
        compiler_params=_params("parallel"),
        name="qkv_proj",
    )(x, wqt, wk, wvt)


def _bias_tile(posk_row, posq_row, table_row):
    tk = posk_row.shape[1]
    tq = posq_row.shape[1]
    table_sq = jnp.broadcast_to(table_row, (LANES, LANES))
    rows = []
    for c in range(tk // LANES):
        pk = posk_row[:, c * LANES:(c + 1) * LANES]
        pk_col = jnp.transpose(jnp.broadcast_to(pk, (LANES, LANES)))
        cols = []
        for d in range(tq // LANES):
            rel = pk_col - posq_row[:, d * LANES:(d + 1) * LANES]
            n = jnp.abs(rel)
            large = jnp.full(rel.shape, REL_BUCKETS // 4, jnp.int32)
            for step in BUCKET_STEPS:
                large = large + jnp.where(n >= step, 1, 0)
            bucket = jnp.where(n < REL_BUCKETS // 4, n, large)
            bucket = bucket + jnp.where(rel > 0, REL_BUCKETS // 2, 0)
            cols.append(jnp.take_along_axis(table_sq, bucket, axis=1))
        rows.append(jnp.concatenate(cols, axis=1))
    return jnp.concatenate(rows, axis=0)


def _attn_kernel(bmin_ref, bmax_ref, tbl_ref,
                 qt_ref, k_ref, vt_ref, posq_ref, posk_ref, tblv_ref, lam_ref, sg_ref,
                 o_ref, qcat, s_a, s_b, smax_a, smax_b, m_scr, acc_scr, *, nk, lambda_init):
    h = pl.program_id(0)
    qi = pl.program_id(1)
    tq = qt_ref.shape[1]
    tk = vt_ref.shape[3]
    heads = pl.num_programs(0)

    q = qt_ref[...]
    row = lax.broadcasted_iota(jnp.int32, q.shape, 0)
    zero = jnp.zeros_like(q)
    qcat[:, 0:tq] = jnp.where(row < HEAD_DIM, q, zero)
    qcat[:, tq:2 * tq] = jnp.where(row >= HEAD_DIM, q, zero)

    qmin = bmin_ref[qi * (tq // LANES)]
    qmax = bmax_ref[qi * (tq // LANES)]
    for r in range(1, tq // LANES):
        qmin = jnp.minimum(qmin, bmin_ref[qi * (tq // LANES) + r])
        qmax = jnp.maximum(qmax, bmax_ref[qi * (tq // LANES) + r])
    bias_before = tbl_ref[(REL_BUCKETS // 2 - 1) * heads + h]
    bias_after = tbl_ref[(REL_BUCKETS - 1) * heads + h]

    acc_scr[...] = jnp.zeros_like(acc_scr)
    m_scr[...] = jnp.full(m_scr.shape, -1e30, F32)

    def classify(kt):
        kmin = bmin_ref[kt * (tk // LANES)]
        kmax = bmax_ref[kt * (tk // LANES)]
        for r in range(1, tk // LANES):
            kmin = jnp.minimum(kmin, bmin_ref[kt * (tk // LANES) + r])
            kmax = jnp.maximum(kmax, bmax_ref[kt * (tk // LANES) + r])
        all_after = kmin - qmax >= REL_FAR
        all_before = kmax - qmin <= -REL_FAR
        near = jnp.logical_not(jnp.logical_or(all_after, all_before))
        const_bias = jnp.where(all_after, bias_after, jnp.where(all_before, bias_before, 0.0))
        return near, const_bias

    def logits(kt, s_ref, smax_ref):
        kk = k_ref[pl.ds(pl.multiple_of(kt * tk, tk), tk), :]
        s = _dot(kk, qcat[...])
        s_ref[...] = s
        smax_ref[...] = jnp.max(s, axis=0, keepdims=True)

    def add_near_bias(kt, s_ref, smax_ref):
        bias = _bias_tile(posk_ref[kt], posq_ref[...], tblv_ref[pl.ds(h, 1), :])
        s = s_ref[...] + jnp.concatenate([bias, bias], axis=1)
        s_ref[...] = s
        smax_ref[...] = jnp.max(s, axis=0, keepdims=True)

    def softmax_update(kt, s_ref, smax_ref, const_bias):
        m_old = m_scr[...]
        m_new = jnp.maximum(m_old, smax_ref[...] + const_bias)
        p = jnp.exp2(s_ref[...] - (m_new - const_bias)).astype(BF16)
        acc_scr[...] = jnp.exp2(m_old - m_new) * acc_scr[...] + _dot(vt_ref[0, kt], p)
        m_scr[...] = m_new

    def stage(kt, cur, nxt, const_bias, fix_next):
        kn = jnp.minimum(kt + 1, nk - 1)
        logits(kn, *nxt)
        softmax_update(kt, *cur, const_bias)
        near_next, bias_next = classify(kn)
        if fix_next:
            @pl.when(near_next)
            def _():
                add_near_bias(kn, *nxt)
        return near_next, bias_next

    buf_a = (s_a, smax_a)
    buf_b = (s_b, smax_b)
    near0, bias0 = classify(0)
    logits(0, *buf_a)

    @pl.when(near0)
    def _():
        add_near_bias(0, *buf_a)

    def pair(kt, const_bias):
        _, const_bias = stage(kt, buf_a, buf_b, const_bias, True)
        _, const_bias = stage(kt + 1, buf_b, buf_a, const_bias, True)
        return const_bias

    def run_group(base, size, const_bias):
        inner_far = jnp.bool_(True)
        for i in range(1, size):
            inner_far = jnp.logical_and(inner_far, jnp.logical_not(classify(base + i)[0]))

        def branch_free():
            c = const_bias
            for i in range(size):
                cur, nxt = (buf_a, buf_b) if i % 2 == 0 else (buf_b, buf_a)
                _, c = stage(base + i, cur, nxt, c, i == size - 1)
            return c

        def split():
            if size == MIN_STAGE_GROUP:
                return lax.fori_loop(0, size // 2, lambda t, c: pair(base + 2 * t, c), const_bias)
            half = size // 2
            return lax.fori_loop(0, 2, lambda t, c: run_group(base + t * half, half, c), const_bias)

        return lax.cond(inner_far, branch_free, split)

    lax.fori_loop(0, nk // STAGE_GROUP, lambda j, c: run_group(j * STAGE_GROUP, STAGE_GROUP, c), bias0)

    lp = lam_ref[...]
    lam = (jnp.exp(jnp.sum(lp[0:1] * lp[1:2], axis=1, keepdims=True))
           - jnp.exp(jnp.sum(lp[2:3] * lp[3:4], axis=1, keepdims=True)) + lambda_init)
    o1 = acc_scr[0:V_DIM, 0:tq] / acc_scr[V_DIM:V_DIM + 1, 0:tq]
    o2 = acc_scr[0:V_DIM, tq:2 * tq] / acc_scr[V_DIM:V_DIM + 1, tq:2 * tq]
    o = o1 - lam * o2
    o = o * lax.rsqrt(jnp.mean(o * o, axis=0, keepdims=True) + LN_EPS)
    o = o * sg_ref[...] * (1.0 - lambda_init)
    o_ref[...] = jnp.transpose(o).astype(BF16)


def _attention(qt, k, vt, positions, bmin, bmax, rel_table, lam_params, subln_g, lambda_init):
    D, S = qt.shape
    heads = D // V_DIM
    tq, tk = TQ, TK
    nk = S // tk
    assert S % tq == 0 and nk % STAGE_GROUP == 0, (S, tq, tk)
    table2 =rel_table.astype(F32) * LOG2_E
    table_rows = jnp.zeros((heads, LANES), F32).at[:, :REL_BUCKETS].set(table2.T)
    kernel = functools.partial(_attn_kernel, nk=nk, lambda_init=lambda_init)
    grid_spec = pltpu.PrefetchScalarGridSpec(
        num_scalar_prefetch=3,
        grid=(heads, S // tq),
        in_specs=[
            pl.BlockSpec((V_DIM, tq), lambda h, i, *_: (h, i)),
            pl.BlockSpec((S, V_DIM), lambda h, i, *_: (0, h)),
            pl.BlockSpec((1, nk, V_DIM + ONES_ROWS, tk), lambda h, i, *_: (h, 0, 0, 0)),
            pl.BlockSpec((1, tq), lambda h, i, *_: (0, i)),
            pl.BlockSpec((nk, 1, tk), lambda h, i, *_: (0, 0, 0)),
            pl.BlockSpec((heads, LANES), lambda h, i, *_: (0, 0)),
            pl.BlockSpec((4, HEAD_DIM), lambda h, i, *_: (0, 0)),
            pl.BlockSpec((V_DIM, 1), lambda h, i, *_: (0, 0)),
        ],
        out_specs=pl.BlockSpec((tq, V_DIM), lambda h, i, *_: (i, h)),
        scratch_shapes=[
            pltpu.VMEM((V_DIM, 2 * tq), BF16),
            pltpu.VMEM((tk, 2 * tq), F32),
            pltpu.VMEM((tk, 2 * tq), F32),
            pltpu.VMEM((1, 2 * tq), F32),
            pltpu.VMEM((1, 2 * tq), F32),
            pltpu.VMEM((1, 2 * tq), F32),
            pltpu.VMEM((V_DIM + ONES_ROWS, 2 * tq), F32),
        ],
    )
    return pl.pallas_call(
        kernel,
        grid_spec=grid_spec,
        out_shape=jax.ShapeDtypeStruct((S, D), BF16),
        compiler_params=_params("parallel", "parallel"),
        name="diff_attention",
    )(bmin, bmax, table2.reshape(-1),
      qt, k, vt, positions.reshape(1, S), positions.reshape(nk, 1, tk), table_rows,
      lam_params, subln_g.reshape(V_DIM, 1))


def _proj_ln_kernel(a_ref, x_ref, w_ref, lg_ref, lb_ref, o_ref):
    z = DEEPNORM_ALPHA * x_ref[...] + _dot(a_ref[...], w_ref[...])
    o_ref[...] = _layer_norm(z, lg_ref[...], lb_ref[...])


def _proj_ln(a, x, w, ln_g, ln_b):
    S, D = x.shape
    K = a.shape[1]
    tm = TM_PROJ
    return pl.pallas_call(
        _proj_ln_kernel,
        grid=(S // tm,),
        in_specs=[
            pl.BlockSpec((tm, K), lambda i: (i, 0)),
            pl.BlockSpec((tm, D), lambda i: (i, 0)),
            _const_spec((K, D)),
            _const_spec((1, D)),
            _const_spec((1, D)),
        ],
        out_specs=pl.BlockSpec((tm, D), lambda i: (i, 0)),
        out_shape=jax.ShapeDtypeStruct((S, D), F32),
        compiler_params=_params("parallel"),
        name="attn_out_proj",
    )(a, x, w.astype(BF16), ln_g.reshape(1, D), ln_b.reshape(1, D))


def _attn_layer(x, positions, bmin, bmax, rel_table, w_qkv, lam_params, subln_g, w_out,
                lambda_init, ln_g, ln_b):
    qt, k, vt = _qkv(x, w_qkv)
    o = _attention(qt, k, vt, positions, bmin, bmax, rel_table, lam_params, subln_g, lambda_init)
    return _proj_ln(o, x, w_out, ln_g, ln_b)


def _ffn_kernel(x_ref, xp_ref, xn_ref, wi_ref, cw_ref, cb_ref, wd_ref, lg_ref, lb_ref,
                o_ref, g_a, g_b, *, chunks):
    i = pl.program_id(0)
    x = x_ref[...]
    tm = x.shape[0]
    halo = xp_ref.shape[0]
    F = wd_ref.shape[0]
    xp = jnp.where(i > 0, xp_ref[...], 0.0)
    xn = jnp.where(i < pl.num_programs(0) - 1, xn_ref[...], 0.0)
    xe = jnp.concatenate([xp, x, xn], axis=0).astype(BF16)
    xb = x.astype(BF16)
    acc = DEEPNORM_ALPHA * x
    start = 0
    for idx, width in enumerate(chunks):
        cols = slice(start, start + width)
        g_scr = g_b if idx % 2 else g_a
        g_scr[:, 0:width] = _dot(xe, wi_ref[:, cols])
        cw = cw_ref[:, cols]
        gate = (g_scr[halo - 1:halo - 1 + tm, 0:width] * cw[0:1]
                + g_scr[halo:halo + tm, 0:width] * cw[1:2]
                + g_scr[halo + 1:halo + 1 + tm, 0:width] * cw[2:3]
                + cb_ref[:, cols])
        up = _dot(xb, wi_ref[:, F + start:F + start + width])
        hidden = (_gelu(gate) * up).astype(BF16)
        acc = acc + _dot(hidden, wd_ref[cols, :])
        start += width
    o_ref[...] = _layer_norm(acc, lg_ref[...], lb_ref[...])


def _ffn_layer(x, w_in, conv_w, conv_b, w_down, ln_g, ln_b):
    S, D = x.shape
    F = w_down.shape[0]
    tm, halo = TM_FFN, SUBLANES
    chunks = (FFN_CHUNK,) * (F // FFN_CHUNK) + ((F % FFN_CHUNK,) if F % FFN_CHUNK else ())
    blocks_per_tile = tm // halo
    last_halo_block = S // halo - 1
    return pl.pallas_call(
        functools.partial(_ffn_kernel, chunks=chunks),
        grid=(S // tm,),
        in_specs=[
            pl.BlockSpec((tm, D), lambda i: (i, 0)),
            pl.BlockSpec((halo, D), lambda i: (jnp.maximum(i * blocks_per_tile - 1, 0), 0)),
            pl.BlockSpec((halo, D),
                         lambda i: (jnp.minimum((i + 1) * blocks_per_tile, last_halo_block), 0)),
            _const_spec((D, 2 * F)),
            _const_spec((conv_w.shape[0], F)),
            _const_spec((1, F)),
            _const_spec((F, D)),
            _const_spec((1, D)),
            _const_spec((1, D)),
        ],
        out_specs=pl.BlockSpec((tm, D), lambda i: (i, 0)),
        out_shape=jax.ShapeDtypeStruct((S, D), F32),
        scratch_shapes=[pltpu.VMEM((tm + 2 * halo, max(chunks)), F32)] * 2,
        compiler_params=_params("parallel"),
        name="conv_glu_ffn",
    )(x, x, x, w_in.astype(BF16), conv_w, conv_b.reshape(1, F), w_down.astype(BF16),
      ln_g.reshape(1, D), ln_b.reshape(1, D))


def kernel(x, positions, rel_bias_table, a_w_in, a_norm_g, a_norm_b, a_w_s, a_b_s, a_w_out,
           b_w_qkv, b_lambda, b_subln_g, b_w_out, f_w_in, f_conv_w, f_conv_b, f_w_down,
           ln_g, ln_b):
    B, S, D = x.shape
    outs = []
    for b in range(B):
        xs = x[b]
        pos = positions[b]
        bmin, bmax = _pos_stats(pos)
        for i in range(DEPTH):
            j = i // 2
            if i % 2 == 0:
                xs = _gmlp_layer(xs, a_w_in[j], a_norm_g[j], a_norm_b[j], a_w_s[j], a_b_s[j],
                                 a_w_out[j], ln_g[i, 0], ln_b[i, 0])
            else:
                xs = _attn_layer(xs, pos, bmin, bmax, rel_bias_table, b_w_qkv[j], b_lambda[j],
                                 b_subln_g[j], b_w_out[j], _lambda_init(i), ln_g[i, 0], ln_b[i, 0])
            xs = _ffn_layer(xs, f_w_in[i], f_conv_w[i], f_conv_b[i], f_w_down[i],
                            ln_g[i, 1], ln_b[i, 1])
        outs.append(xs)
    return jnp.stack(outs)
```

```python
import functools
import math

import jax
import jax.numpy as jnp
from jax import lax
from jax.experimental import pallas as pl
from jax.experimental.pallas import tpu as pltpu

F32 = jnp.float32
BF16 = jnp.bfloat16

DEPTH = 4
A_CHUNK = 128
A_GROUPS = 8
HEAD_DIM = 64
V_DIM = 2 * HEAD_DIM
REL_BUCKETS = 32
REL_FAR = 128
LN_EPS = 1e-5
LOG2_E = math.log2(math.e)
DEEPNORM_ALPHA = (2 * DEPTH) ** 0.25
REL_CLIP = 2047
F32_MANTISSA_BITS = 23
F32_EXP_BIAS = 127

LANES = 128
SUBLANES = 8
BF16_ROWS = 16
VMEM_LIMIT = 56 * 1024 * 1024

TM_GMLP = 256
TM_QKV = 512
TQ = 512
TK = 512
TM_PROJ = 512
TM_FFN = 512
FFN_CHUNK = 1024
STAGE_GROUP = 4
MIN_STAGE_GROUP = 2
ONES_ROWS = BF16_ROWS


def _lambda_init(layer_idx):
    return 0.8 - 0.6 * math.exp(-0.3 * layer_idx)


def _gelu(x):
    return 0.5 * x * (1.0 + lax.erf(x * (1.0 / math.sqrt(2.0))))


def _layer_norm(z, g, b):
    mu = jnp.mean(z, axis=-1, keepdims=True)
    zc = z - mu
    var = jnp.mean(zc * zc, axis=-1, keepdims=True)
    return zc * lax.rsqrt(var + LN_EPS) * g + b


def _dot(a, b):
    return jnp.dot(a, b, preferred_element_type=F32)


def _dot_nt(a, b):
    return lax.dot_general(a, b, (((1,), (1,)), ((), ())), preferred_element_type=F32)


def _const_spec(shape):
    nd = len(shape)
    return pl.BlockSpec(shape, lambda *_: (0,) * nd, pipeline_mode=pl.Buffered(1))


def _params(*sem):
    return pltpu.CompilerParams(dimension_semantics=sem, vmem_limit_bytes=VMEM_LIMIT)


def _gmlp_kernel(x_ref, wi_ref, ng_ref, nb_ref, ws_ref, bs_ref, wo_ref, lg_ref, lb_ref, o_ref):
    x = x_ref[...]
    tm = x.shape[0]
    W = wo_ref.shape[0]
    gd = W // A_GROUPS
    hidden = _gelu(_dot(x.astype(BF16), wi_ref[...]))
    v = _layer_norm(hidden[:, W:], ng_ref[...], nb_ref[...]).astype(BF16)
    mixed = []
    for c in range(tm // A_CHUNK):
        row = []
        for g in range(A_GROUPS):
            vc = v[c * A_CHUNK:(c + 1) * A_CHUNK, g * gd:(g + 1) * gd]
            row.append(_dot(ws_ref[g], vc) + bs_ref[g])
        mixed.append(jnp.concatenate(row, axis=1))
    y = (hidden[:, :W] * jnp.concatenate(mixed, axis=0)).astype(BF16)
    z = DEEPNORM_ALPHA * x + _dot(y, wo_ref[...])
    o_ref[...] = _layer_norm(z, lg_ref[...], lb_ref[...])


def _gmlp_layer(x, w_in, norm_g, norm_b, w_s, b_s, w_out, ln_g, ln_b):
    S, D = x.shape
    W = w_out.shape[0]
    tm = TM_GMLP
    return pl.pallas_call(
        _gmlp_kernel,
        grid=(S // tm,),
        in_specs=[
            pl.BlockSpec((tm, D), lambda i: (i, 0)),
            _const_spec((D, 2 * W)),
            _const_spec((1, W)),
            _const_spec((1, W)),
            _const_spec((A_GROUPS, A_CHUNK, A_CHUNK)),
            _const_spec((A_GROUPS, A_CHUNK, 1)),
            _const_spec((W, D)),
            _const_spec((1, D)),
            _const_spec((1, D)),
        ],
        out_specs=pl.BlockSpec((tm, D), lambda i: (i, 0)),
        out_shape=jax.ShapeDtypeStruct((S, D), F32),
        compiler_params=_params("parallel"),
        name="gmlp_layer",
    )(x, w_in.astype(BF16), norm_g.reshape(1, W), norm_b.reshape(1, W), w_s.astype(BF16),
      b_s.reshape(A_GROUPS, A_CHUNK, 1), w_out.astype(BF16), ln_g.reshape(1, D), ln_b.reshape(1, D))


def _pos_stats_kernel(p_ref, mn_ref, mx_ref):
    p = p_ref[...]
    mn_ref[...] = jnp.min(p, axis=1, keepdims=True)
    mx_ref[...] = jnp.max(p, axis=1, keepdims=True)


def _pos_stats(positions):
    nb = positions.shape[0] // LANES
    mn, mx = pl.pallas_call(
        _pos_stats_kernel,
        out_shape=(jax.ShapeDtypeStruct((nb, 1), jnp.int32),) * 2,
        name="pos_stats",
    )(positions.reshape(nb, LANES))
    return mn.reshape(nb), mx.reshape(nb)


def _qkv_kernel(x_ref, wqt_ref, wk_ref, wvt_ref, qt_ref, k_ref, vt_ref):
    xb = x_ref[...].astype(BF16)
    tm = xb.shape[0]
    qt_ref[...] = _dot_nt(wqt_ref[...], xb).astype(BF16)
    k_ref[...] = _dot(xb, wk_ref[...]).astype(BF16)
    vt = _dot_nt(wvt_ref[...], xb).astype(BF16)
    heads = vt.shape[0] // V_DIM
    vt_ref[:, 0, 0:V_DIM, :] = vt.reshape(heads, V_DIM, tm)
    vt_ref[:, 0, V_DIM:, :] = jnp.ones((heads, ONES_ROWS, tm), BF16)


def _qkv(x, w_qkv):
    S, D = x.shape
    tm = TM_QKV
    heads = D // V_DIM
    wqt = (w_qkv[:, :D] * (HEAD_DIM ** -0.5 * LOG2_E)).T.astype(BF16)
    wk = w_qkv[:, D:2 * D].astype(BF16)
    wvt = w_qkv[:, 2 * D:].T.astype(BF16)
    return pl.pallas_call(
        _qkv_kernel,
        grid=(S // tm,),
        in_specs=[
            pl.BlockSpec((tm, D), lambda i: (i, 0)),
            _const_spec((D, D)),
            _const_spec((D, D)),
            _const_spec((D, D)),
        ],
        out_specs=[
            pl.BlockSpec((D, tm), lambda i: (0, i)),
            pl.BlockSpec((tm, D), lambda i: (i, 0)),
            pl.BlockSpec((heads, 1, V_DIM + ONES_ROWS, tm), lambda i: (0, i, 0, 0)),
        ],
        out_shape=[
            jax.ShapeDtypeStruct((D, S), BF16),
            jax.ShapeDtypeStruct((S, D), BF16),
            jax.ShapeDtypeStruct((heads, S // tm, V_DIM + ONES_ROWS, tm), BF16),
        ],
        compiler_params=_params("parallel"),
        name="qkv_proj",
    )(x, wqt, wk, wvt)


def _bias_tile(posk_row, posq_row, table_row):
    tk = posk_row.shape[1]
    tq = posq_row.shape[1]
    table_sq = jnp.broadcast_to(table_row, (LANES, LANES))
    rows = []
    for c in range(tk // LANES):
        pk = posk_row[:, c * LANES:(c + 1) * LANES]
        pk_col = jnp.transpose(jnp.broadcast_to(pk, (LANES, LANES)))
        cols = []
        for d in range(tq // LANES):
            rel = pk_col - posq_row[:, d * LANES:(d + 1) * LANES]
            n = jnp.abs(rel)
            nsq = jnp.square(jnp.minimum(n, REL_CLIP)).astype(F32)
            log2_nsq = (lax.bitcast_convert_type(nsq, jnp.int32) >> F32_MANTISSA_BITS) - F32_EXP_BIAS
            large = jnp.minimum(log2_nsq + 2, REL_BUCKETS // 2 - 1)
            bucket = jnp.where(n < REL_BUCKETS // 4, n, large)
            bucket = bucket + jnp.where(rel > 0, REL_BUCKETS // 2, 0)
            cols.append(jnp.take_along_axis(table_sq, bucket, axis=1))
        rows.append(jnp.concatenate(cols, axis=1))
    return jnp.concatenate(rows, axis=0)


def _attn_kernel(bmin_ref, bmax_ref, tbl_ref,
                 qt_ref, k_ref, vt_ref, posq_ref, posk_ref, tblv_ref, lam_ref, sg_ref,
                 o_ref, qcat, s_a, s_b, smax_a, smax_b, m_scr, acc_scr, *, nk, lambda_init):
    h = pl.program_id(0)
    qi = pl.program_id(1)
    tq = qt_ref.shape[1]
    tk = vt_ref.shape[3]
    heads = pl.num_programs(0)

    q = qt_ref[...]
    row = lax.broadcasted_iota(jnp.int32, q.shape, 0)
    zero = jnp.zeros_like(q)
    qcat[:, 0:tq] = jnp.where(row < HEAD_DIM, q, zero)
    qcat[:, tq:2 * tq] = jnp.where(row >= HEAD_DIM, q, zero)

    qmin = bmin_ref[qi * (tq // LANES)]
    qmax = bmax_ref[qi * (tq // LANES)]
    for r in range(1, tq // LANES):
        qmin = jnp.minimum(qmin, bmin_ref[qi * (tq // LANES) + r])
        qmax = jnp.maximum(qmax, bmax_ref[qi * (tq // LANES) + r])
    bias_before = tbl_ref[(REL_BUCKETS // 2 - 1) * heads + h]
    bias_after = tbl_ref[(REL_BUCKETS - 1) * heads + h]

    acc_scr[...] = jnp.zeros_like(acc_scr)
    m_scr[...] = jnp.full(m_scr.shape, -1e30, F32)

    def classify(kt):
        kmin = bmin_ref[kt * (tk // LANES)]
        kmax = bmax_ref[kt * (tk // LANES)]
        for r in range(1, tk // LANES):
            kmin = jnp.minimum(kmin, bmin_ref[kt * (tk // LANES) + r])
            kmax = jnp.maximum(kmax, bmax_ref[kt * (tk // LANES) + r])
        all_after = kmin - qmax >= REL_FAR
        all_before = kmax - qmin <= -REL_FAR
        near = jnp.logical_not(jnp.logical_or(all_after, all_before))
        const_bias = jnp.where(all_after, bias_after, jnp.where(all_before, bias_before, 0.0))
        return near, const_bias

    def logits(kt, s_ref, smax_ref):
        kk = k_ref[pl.ds(pl.multiple_of(kt * tk, tk), tk), :]
        s = _dot(kk, qcat[...])
        s_ref[...] = s
        smax_ref[...] = jnp.max(s, axis=0, keepdims=True)

    def add_near_bias(kt, s_ref, smax_ref):
        bias = _bias_tile(posk_ref[kt], posq_ref[...], tblv_ref[pl.ds(h, 1), :])
        s = s_ref[...] + jnp.concatenate([bias, bias], axis=1)
        s_ref[...] = s
        smax_ref[...] = jnp.max(s, axis=0, keepdims=True)

    def softmax_update(kt, s_ref, smax_ref, const_bias):
        m_old = m_scr[...]
        m_new = jnp.maximum(m_old, smax_ref[...] + const_bias)
        p = jnp.exp2(s_ref[...] - (m_new - const_bias)).astype(BF16)
        acc_scr[...] = jnp.exp2(m_old - m_new) * acc_scr[...] + _dot(vt_ref[0, kt], p)
        m_scr[...] = m_new

    def stage(kt, cur, nxt, const_bias, fix_next):
        kn = jnp.minimum(kt + 1, nk - 1)
        logits(kn, *nxt)
        softmax_update(kt, *cur, const_bias)
        near_next, bias_next = classify(kn)
        if fix_next:
            @pl.when(near_next)
            def _():
                add_near_bias(kn, *nxt)
        return near_next, bias_next

    buf_a = (s_a, smax_a)
    buf_b = (s_b, smax_b)
    near0, bias0 = classify(0)
    logits(0, *buf_a)

    @pl.when(near0)
    def _():
        add_near_bias(0, *buf_a)

    def pair(kt, const_bias):
        _, const_bias = stage(kt, buf_a, buf_b, const_bias, True)
        _, const_bias = stage(kt + 1, buf_b, buf_a, const_bias, True)
        return const_bias

    def run_group(base, size, const_bias):
        inner_far = jnp.bool_(True)
        for i in range(1, size):
            inner_far = jnp.logical_and(inner_far, jnp.logical_not(classify(base + i)[0]))

        def branch_free():
            c = const_bias
            for i in range(size):
                cur, nxt = (buf_a, buf_b) if i % 2 == 0 else (buf_b, buf_a)
                _, c = stage(base + i, cur, nxt, c, i == size - 1)
            return c

        def split():
            if size == MIN_STAGE_GROUP:
                return lax.fori_loop(0, size // 2, lambda t, c: pair(base + 2 * t, c), const_bias)
            half = size // 2
            return lax.fori_loop(0, 2, lambda t, c: run_group(base + t * half, half, c), const_bias)

        return lax.cond(inner_far, branch_free, split)

    lax.fori_loop(0, nk // STAGE_GROUP, lambda j, c: run_group(j * STAGE_GROUP, STAGE_GROUP, c), bias0)

    lp = lam_ref[...]
    lam = (jnp.exp(jnp.sum(lp[0:1] * lp[1:2], axis=1, keepdims=True))
           - jnp.exp(jnp.sum(lp[2:3] * lp[3:4], axis=1, keepdims=True)) + lambda_init)
    o1 = acc_scr[0:V_DIM, 0:tq] / acc_scr[V_DIM:V_DIM + 1, 0:tq]
    o2 = acc_scr[0:V_DIM, tq:2 * tq] / acc_scr[V_DIM:V_DIM + 1, tq:2 * tq]
    o = o1 - lam * o2
    o = o * lax.rsqrt(jnp.mean(o * o, axis=0, keepdims=True) + LN_EPS)
    o = o * sg_ref[...] * (1.0 - lambda_init)
    o_ref[...] = jnp.transpose(o).astype(BF16)


def _attention(qt, k, vt, positions, bmin, bmax, rel_table, lam_params, subln_g, lambda_init):
    D, S = qt.shape
    heads = D // V_DIM
    tq, tk = TQ, TK
    nk = S // tk
    assert S % tq == 0 and nk % STAGE_GROUP == 0, (S, tq, tk)
    table2 = rel_table.astype(F32) * LOG2_E
    table_rows = jnp.zeros((heads, LANES), F32).at[:, :REL_BUCKETS].set(table2.T)
    kernel = functools.partial(_attn_kernel, nk=nk, lambda_init=lambda_init)
    grid_spec = pltpu.PrefetchScalarGridSpec(
        num_scalar_prefetch=3,
        grid=(heads, S // tq),
        in_specs=[
            pl.BlockSpec((V_DIM, tq), lambda h, i, *_: (h, i)),
            pl.BlockSpec((S, V_DIM), lambda h, i, *_: (0, h)),
            pl.BlockSpec((1, nk, V_DIM + ONES_ROWS, tk), lambda h, i, *_: (h, 0, 0, 0)),
            pl.BlockSpec((1, tq), lambda h, i, *_: (0, i)),
            pl.BlockSpec((nk, 1, tk), lambda h, i, *_: (0, 0, 0)),
            pl.BlockSpec((heads, LANES), lambda h, i, *_: (0, 0)),
            pl.BlockSpec((4, HEAD_DIM), lambda h, i, *_: (0, 0)),
            pl.BlockSpec((V_DIM, 1), lambda h, i, *_: (0, 0)),
        ],
        out_specs=pl.BlockSpec((tq, V_DIM), lambda h, i, *_: (i, h)),
        scratch_shapes=[
            pltpu.VMEM((V_DIM, 2 * tq), BF16),
            pltpu.VMEM((tk, 2 * tq), F32),
            pltpu.VMEM((tk, 2 * tq), F32),
            pltpu.VMEM((1, 2 * tq), F32),
            pltpu.VMEM((1, 2 * tq), F32),
            pltpu.VMEM((1, 2 * tq), F32),
            pltpu.VMEM((V_DIM + ONES_ROWS, 2 * tq), F32),
        ],
    )
    return pl.pallas_call(
        kernel,
        grid_spec=grid_spec,
        out_shape=jax.ShapeDtypeStruct((S, D), BF16),
        compiler_params=_params("parallel", "parallel"),
        name="diff_attention",
    )(bmin, bmax, table2.reshape(-1),
      qt, k, vt, positions.reshape(1, S), positions.reshape(nk, 1, tk), table_rows,
      lam_params, subln_g.reshape(V_DIM, 1))


def _proj_ln_kernel(a_ref, x_ref, w_ref, lg_ref, lb_ref, o_ref):
    z = DEEPNORM_ALPHA * x_ref[...] + _dot(a_ref[...], w_ref[...])
    o_ref[...] = _layer_norm(z, lg_ref[...], lb_ref[...])


def _proj_ln(a, x, w, ln_g, ln_b):
    S, D = x.shape
    K = a.shape[1]
    tm = TM_PROJ
    return pl.pallas_call(
        _proj_ln_kernel,
        grid=(S // tm,),
        in_specs=[
            pl.BlockSpec((tm, K), lambda i: (i, 0)),
            pl.BlockSpec((tm, D), lambda i: (i, 0)),
            _const_spec((K, D)),
            _const_spec((1, D)),
            _const_spec((1, D)),
        ],
        out_specs=pl.BlockSpec((tm, D), lambda i: (i, 0)),
        out_shape=jax.ShapeDtypeStruct((S, D), F32),
        compiler_params=_params("parallel"),
        name="attn_out_proj",
    )(a, x, w.astype(BF16), ln_g.reshape(1, D), ln_b.reshape(1, D))


def _attn_layer(x, positions, bmin, bmax, rel_table, w_qkv, lam_params, subln_g, w_out,
                lambda_init, ln_g, ln_b):
    qt, k, vt = _qkv(x, w_qkv)
    o = _attention(qt, k, vt, positions, bmin, bmax, rel_table, lam_params, subln_g, lambda_init)
    return _proj_ln(o, x, w_out, ln_g, ln_b)


def _ffn_kernel(x_ref, xp_ref, xn_ref, wi_ref, cw_ref, cb_ref, wd_ref, lg_ref, lb_ref,
                o_ref, g_a, g_b, *, chunks):
    i = pl.program_id(0)
    x = x_ref[...]
    tm = x.shape[0]
    halo = xp_ref.shape[0]
    F = wd_ref.shape[0]
    xp = jnp.where(i > 0, xp_ref[...], 0.0)
    xn = jnp.where(i < pl.num_programs(0) - 1, xn_ref[...], 0.0)
    xe = jnp.concatenate([xp, x, xn], axis=0).astype(BF16)
    xb = x.astype(BF16)
    acc = DEEPNORM_ALPHA * x
    start = 0
    for idx, width in enumerate(chunks):
        cols = slice(start, start + width)
        g_scr = g_b if idx % 2 else g_a
        g_scr[:, 0:width] = _dot(xe, wi_ref[:, cols])
        cw = cw_ref[:, cols]
        gate = (g_scr[halo - 1:halo - 1 + tm, 0:width] * cw[0:1]
                + g_scr[halo:halo + tm, 0:width] * cw[1:2]
                + g_scr[halo + 1:halo + 1 + tm, 0:width] * cw[2:3]
                + cb_ref[:, cols])
        up = _dot(xb, wi_ref[:, F + start:F + start + width])
        hidden = (_gelu(gate) * up).astype(BF16)
        acc = acc + _dot(hidden, wd_ref[cols, :])
        start += width
    o_ref[...] = _layer_norm(acc, lg_ref[...], lb_ref[...])


def _ffn_layer(x, w_in, conv_w, conv_b, w_down, ln_g, ln_b):
    S, D = x.shape
    F = w_down.shape[0]
    tm, halo = TM_FFN, SUBLANES
    chunks = (FFN_CHUNK,) * (F // FFN_CHUNK) + ((F % FFN_CHUNK,) if F % FFN_CHUNK else ())
    blocks_per_tile = tm // halo
    last_halo_block = S // halo - 1
    return pl.pallas_call(
        functools.partial(_ffn_kernel, chunks=chunks),
        grid=(S // tm,),
        in_specs=[
            pl.BlockSpec((tm, D), lambda i: (i, 0)),
            pl.BlockSpec((halo, D), lambda i: (jnp.maximum(i * blocks_per_tile - 1, 0), 0)),
            pl.BlockSpec((halo, D),
                         lambda i: (jnp.minimum((i + 1) * blocks_per_tile, last_halo_block), 0)),
            _const_spec((D, 2 * F)),
            _const_spec((conv_w.shape[0], F)),
            _const_spec((1, F)),
            _const_spec((F, D)),
            _const_spec((1, D)),
            _const_spec((1, D)),
        ],
        out_specs=pl.BlockSpec((tm, D), lambda i: (i, 0)),
        out_shape=jax.ShapeDtypeStruct((S, D), F32),
        scratch_shapes=[pltpu.VMEM((tm + 2 * halo, max(chunks)), F32)] * 2,
        compiler_params=_params("parallel"),
        name="conv_glu_ffn",
    )(x, x, x, w_in.astype(BF16), conv_w, conv_b.reshape(1, F), w_down.astype(BF16),
      ln_g.reshape(1, D), ln_b.reshape(1, D))


def kernel(x, positions, rel_bias_table, a_w_in, a_norm_g, a_norm_b, a_w_s, a_b_s, a_w_out,
           b_w_qkv, b_lambda, b_subln_g, b_w_out, f_w_in, f_conv_w, f_conv_b, f_w_down,
           ln_g, ln_b):
    B, S, D = x.shape
    outs = []
    for b in range(B):
        xs = x[b]
        pos = positions[b]
        bmin, bmax = _pos_stats(pos)
        for i in range(DEPTH):
            j = i // 2
            if i % 2 == 0:
                xs = _gmlp_layer(xs, a_w_in[j], a_norm_g[j], a_norm_b[j], a_w_s[j], a_b_s[j],
                                 a_w_out[j], ln_g[i, 0], ln_b[i, 0])
            else:
                xs = _attn_layer(xs, pos, bmin, bmax, rel_bias_table, b_w_qkv[j], b_lambda[j],
                                 b_subln_g[j], b_w_out[j], _lambda_init(i), ln_g[i, 0], ln_b[i, 0])
            xs = _ffn_layer(xs, f_w_in[i], f_conv_w[i], f_conv_b[i], f_w_down[i],
                            ln_g[i, 1], ln_b[i, 1])
        outs.append(xs)
    return jnp.stack(outs)
```

```python
import functools
import math

import jax
import jax.numpy as jnp
from jax import lax
from jax.experimental import pallas as pl
from jax.experimental.pallas import tpu as pltpu

F32 = jnp.float32
BF16 = jnp.bfloat16

DEPTH = 4
A_CHUNK = 128
A_GROUPS = 8
HEAD_DIM = 64
V_DIM = 2 * HEAD_DIM
REL_BUCKETS = 32
REL_FAR = 128
LN_EPS = 1e-5
LOG2_E = math.log2(math.e)
DEEPNORM_ALPHA = (2 * DEPTH) ** 0.25
REL_CLIP = 2047
F32_MANTISSA_BITS = 23
F32_EXP_BIAS = 127

LANES = 128
SUBLANES = 8
BF16_ROWS = 16
VMEM_LIMIT = 56 * 1024 * 1024

TM_GMLP = 256
TM_QKV = 512
TK = 512
TM_PROJ = 512
TM_FFN = 512
FFN_CHUNK = 1024
STAGE_GROUP = 4
MIN_STAGE_GROUP = 2
ONES_ROWS = BF16_ROWS


def _lambda_init(layer_idx):
    return 0.8 - 0.6 * math.exp(-0.3 * layer_idx)


def _gelu(x):
    return 0.5 * x * (1.0 + lax.erf(x * (1.0 / math.sqrt(2.0))))


def _layer_norm(z, g, b):
    mu = jnp.mean(z, axis=-1, keepdims=True)
    zc = z - mu
    var = jnp.mean(zc * zc, axis=-1, keepdims=True)
    return zc * lax.rsqrt(var + LN_EPS) * g + b


def _dot(a, b):
    return jnp.dot(a, b, preferred_element_type=F32)


def _dot_nt(a, b):
    return lax.dot_general(a, b, (((1,), (1,)), ((), ())), preferred_element_type=F32)


def _const_spec(shape):
    nd = len(shape)
    return pl.BlockSpec(shape, lambda *_: (0,) * nd, pipeline_mode=pl.Buffered(1))


def _params(*sem):
    return pltpu.CompilerParams(dimension_semantics=sem, vmem_limit_bytes=VMEM_LIMIT)


def _gmlp_kernel(x_ref, wi_ref, ng_ref, nb_ref, ws_ref, bs_ref, wo_ref, lg_ref, lb_ref, o_ref):
    x = x_ref[...]
    tm = x.shape[0]
    W = wo_ref.shape[0]
    gd = W // A_GROUPS
    hidden = _gelu(_dot(x.astype(BF16), wi_ref[...]))
    v = _layer_norm(hidden[:, W:], ng_ref[...], nb_ref[...]).astype(BF16)
    mixed = []
    for c in range(tm // A_CHUNK):
        row = []
        for g in range(A_GROUPS):
            vc = v[c * A_CHUNK:(c + 1) * A_CHUNK, g * gd:(g + 1) * gd]
            row.append(_dot(ws_ref[g], vc) + bs_ref[g])
        mixed.append(jnp.concatenate(row, axis=1))
    y = (hidden[:, :W] * jnp.concatenate(mixed, axis=0)).astype(BF16)
    z = DEEPNORM_ALPHA * x + _dot(y, wo_ref[...])
    o_ref[...] = _layer_norm(z, lg_ref[...], lb_ref[...])


def _gmlp_layer(x, w_in, norm_g, norm_b, w_s, b_s, w_out, ln_g, ln_b):
    S, D = x.shape
    W = w_out.shape[0]
    tm = TM_GMLP
    return pl.pallas_call(
        _gmlp_kernel,
        grid=(S // tm,),
        in_specs=[
            pl.BlockSpec((tm, D), lambda i: (i, 0)),
            _const_spec((D, 2 * W)),
            _const_spec((1, W)),
            _const_spec((1, W)),
            _const_spec((A_GROUPS, A_CHUNK, A_CHUNK)),
            _const_spec((A_GROUPS, A_CHUNK, 1)),
            _const_spec((W, D)),
            _const_spec((1, D)),
            _const_spec((1, D)),
        ],
        out_specs=pl.BlockSpec((tm, D), lambda i: (i, 0)),
        out_shape=jax.ShapeDtypeStruct((S, D), F32),
        compiler_params=_params("parallel"),
        name="gmlp_layer",
    )(x, w_in.astype(BF16), norm_g.reshape(1, W), norm_b.reshape(1, W), w_s.astype(BF16),
      b_s.reshape(A_GROUPS, A_CHUNK, 1), w_out.astype(BF16), ln_g.reshape(1, D), ln_b.reshape(1, D))


def _pos_stats_kernel(p_ref, mn_ref, mx_ref):
    p = p_ref[...]
    mn_ref[...] = jnp.min(p, axis=1, keepdims=True)
    mx_ref[...] = jnp.max(p, axis=1, keepdims=True)


def _pos_stats(positions):
    nb = positions.shape[0] // LANES
    mn, mx = pl.pallas_call(
        _pos_stats_kernel,
        out_shape=(jax.ShapeDtypeStruct((nb, 1), jnp.int32),) * 2,
        name="pos_stats",
    )(positions.reshape(nb, LANES))
    return mn.reshape(nb), mx.reshape(nb)


def _qkv_kernel(x_ref, wqt_ref, wk_ref, wvt_ref, qt_ref, k_ref, vt_ref):
    xb = x_ref[...].astype(BF16)
    tm = xb.shape[0]
    qt_ref[0] = _dot_nt(wqt_ref[...], xb).astype(BF16)
    k_ref[...] = _dot(xb, wk_ref[...]).astype(BF16)
    vt = _dot_nt(wvt_ref[...], xb).astype(BF16)
    heads = vt.shape[0] // V_DIM
    vt_ref[:, 0, 0:V_DIM, :] = vt.reshape(heads, V_DIM, tm)
    vt_ref[:, 0, V_DIM:, :] = jnp.ones((heads, ONES_ROWS, tm), BF16)


def _qkv(x, w_qkv):
    S, D = x.shape
    tm = TM_QKV
    heads = D // V_DIM
    wqt = (w_qkv[:, :D] * (HEAD_DIM ** -0.5 * LOG2_E)).T.astype(BF16)
    wk = w_qkv[:, D:2 * D].astype(BF16)
    wvt = w_qkv[:, 2 * D:].T.astype(BF16)
    return pl.pallas_call(
        _qkv_kernel,
        grid=(S // tm,),
        in_specs=[
            pl.BlockSpec((tm, D), lambda i: (i, 0)),
            _const_spec((D, D)),
            _const_spec((D, D)),
            _const_spec((D, D)),
        ],
        out_specs=[
            pl.BlockSpec((1, D, tm), lambda i: (i, 0, 0)),
            pl.BlockSpec((tm, D), lambda i: (i, 0)),
            pl.BlockSpec((heads, 1, V_DIM + ONES_ROWS, tm), lambda i: (0, i, 0, 0)),
        ],
        out_shape=[
            jax.ShapeDtypeStruct((S // tm, D, tm), BF16),
            jax.ShapeDtypeStruct((S, D), BF16),
            jax.ShapeDtypeStruct((heads, S // tm, V_DIM + ONES_ROWS, tm), BF16),
        ],
        compiler_params=_params("parallel"),
        name="qkv_proj",
    )(x, wqt, wk, wvt)


def _bias_tile(posk_row, posq_row, table_row):
    tk = posk_row.shape[1]
    tq = posq_row.shape[1]
    table_sq = jnp.broadcast_to(table_row, (LANES, LANES))
    rows = []
    for c in range(tk // LANES):
        pk = posk_row[:, c * LANES:(c + 1) * LANES]
        pk_col = jnp.transpose(jnp.broadcast_to(pk, (LANES, LANES)))
        cols = []
        for d in range(tq // LANES):
            rel = pk_col - posq_row[:, d * LANES:(d + 1) * LANES]
            n = jnp.abs(rel)
            nsq = jnp.square(jnp.minimum(n, REL_CLIP)).astype(F32)
            log2_nsq = (lax.bitcast_convert_type(nsq, jnp.int32) >> F32_MANTISSA_BITS) - F32_EXP_BIAS
            large = jnp.minimum(log2_nsq + 2, REL_BUCKETS // 2 - 1)
            bucket = jnp.where(n < REL_BUCKETS // 4, n, large)
            bucket = bucket + jnp.where(rel > 0, REL_BUCKETS // 2, 0)
            cols.append(jnp.take_along_axis(table_sq, bucket, axis=1))
        rows.append(jnp.concatenate(cols, axis=1))
    return jnp.concatenate(rows, axis=0)


def _attn_kernel(bmin_ref, bmax_ref, tbl_ref,
                 qt_ref, k_ref, vt_ref, posq_ref, posk_ref, tblv_ref, lam_ref, sg_ref,
                 o_ref, qcat, s_a, s_b, smax_a, smax_b, m_scr, acc_scr, *, lambda_init):
    h = pl.program_id(0)
    heads = pl.num_programs(0)
    lp = lam_ref[...]
    lam = (jnp.exp(jnp.sum(lp[0:1] * lp[1:2], axis=1, keepdims=True))
           - jnp.exp(jnp.sum(lp[2:3] * lp[3:4], axis=1, keepdims=True)) + lambda_init)
    bias_before = tbl_ref[(REL_BUCKETS // 2 - 1) * heads + h]
    bias_after = tbl_ref[(REL_BUCKETS - 1) * heads + h]

    def query_tile(qi, carry):
        _attn_query_tile(qi, h, lam, bias_before, bias_after, lambda_init,
                         bmin_ref, bmax_ref, qt_ref, k_ref, vt_ref, posq_ref, posk_ref, tblv_ref,
                         sg_ref, o_ref, qcat, s_a, s_b, smax_a, smax_b, m_scr, acc_scr)
        return carry

    lax.fori_loop(0, qt_ref.shape[0], query_tile, 0)


def _attn_query_tile(qi, h, lam, bias_before, bias_after, lambda_init,
                     bmin_ref, bmax_ref, qt_ref, k_ref, vt_ref, posq_ref, posk_ref, tblv_ref,
                     sg_ref, o_ref, qcat, s_a, s_b, smax_a, smax_b, m_scr, acc_scr):
    tq = qt_ref.shape[2]
    nk, _, tk = posk_ref.shape
    q = qt_ref[qi]
    row = lax.broadcasted_iota(jnp.int32, q.shape, 0)
    zero = jnp.zeros_like(q)
    qcat[:, 0:tq] = jnp.where(row < HEAD_DIM, q, zero)
    qcat[:, tq:2 * tq] = jnp.where(row >= HEAD_DIM, q, zero)

    qmin = bmin_ref[qi * (tq // LANES)]
    qmax = bmax_ref[qi * (tq // LANES)]
    for r in range(1, tq // LANES):
        qmin = jnp.minimum(qmin, bmin_ref[qi * (tq // LANES) + r])
        qmax = jnp.maximum(qmax, bmax_ref[qi * (tq // LANES) + r])

    acc_scr[...] = jnp.zeros_like(acc_scr)
    m_scr[...] = jnp.full(m_scr.shape, -1e30, F32)

    def classify(kt):
        kmin = bmin_ref[kt * (tk // LANES)]
        kmax = bmax_ref[kt * (tk // LANES)]
        for r in range(1, tk // LANES):
            kmin = jnp.minimum(kmin, bmin_ref[kt * (tk // LANES) + r])
            kmax = jnp.maximum(kmax, bmax_ref[kt * (tk // LANES) + r])
        all_after = kmin - qmax >= REL_FAR
        all_before = kmax - qmin <= -REL_FAR
        near = jnp.logical_not(jnp.logical_or(all_after, all_before))
        const_bias = jnp.where(all_after, bias_after, jnp.where(all_before, bias_before, 0.0))
        return near, const_bias

    def logits(kt, s_ref, smax_ref):
        kk = k_ref[pl.ds(pl.multiple_of(kt * tk, tk), tk), :]
        s = _dot(kk, qcat[...])
        s_ref[...] = s
        smax_ref[...] = jnp.max(s, axis=0, keepdims=True)

    def add_near_bias(kt, s_ref, smax_ref):
        bias = _bias_tile(posk_ref[kt], posq_ref[qi], tblv_ref[pl.ds(h, 1), :])
        s = s_ref[...] + jnp.concatenate([bias, bias], axis=1)
        s_ref[...] = s
        smax_ref[...] = jnp.max(s, axis=0, keepdims=True)

    def softmax_update(kt, s_ref, smax_ref, const_bias):
        m_old = m_scr[...]
        m_new = jnp.maximum(m_old, smax_ref[...] + const_bias)
        p = jnp.exp2(s_ref[...] - (m_new - const_bias)).astype(BF16)
        acc_scr[...] = jnp.exp2(m_old - m_new) * acc_scr[...] + _dot(vt_ref[0, kt], p)
        m_scr[...] = m_new

    def stage(kt, cur, nxt, const_bias, fix_next):
        kn = jnp.minimum(kt + 1, nk - 1)
        logits(kn, *nxt)
        softmax_update(kt, *cur, const_bias)
        near_next, bias_next = classify(kn)
        if fix_next:
            @pl.when(near_next)
            def _():
                add_near_bias(kn, *nxt)
        return near_next, bias_next

    buf_a = (s_a, smax_a)
    buf_b = (s_b, smax_b)
    near0, bias0 = classify(0)
    logits(0, *buf_a)

    @pl.when(near0)
    def _():
        add_near_bias(0, *buf_a)

    def pair(kt, const_bias):
        _, const_bias = stage(kt, buf_a, buf_b, const_bias, True)
        _, const_bias = stage(kt + 1, buf_b, buf_a, const_bias, True)
        return const_bias

    def run_group(base, size, const_bias):
        inner_far = jnp.bool_(True)
        for i in range(1, size):
            inner_far = jnp.logical_and(inner_far, jnp.logical_not(classify(base + i)[0]))

        def branch_free():
            c = const_bias
            for i in range(size):
                cur, nxt = (buf_a, buf_b) if i % 2 == 0 else (buf_b, buf_a)
                _, c = stage(base + i, cur, nxt, c, i == size - 1)
            return c

        def split():
            if size == MIN_STAGE_GROUP:
                return lax.fori_loop(0, size // 2, lambda t, c: pair(base + 2 * t, c), const_bias)
            half = size // 2
            return lax.fori_loop(0, 2, lambda t, c: run_group(base + t * half, half, c), const_bias)

        return lax.cond(inner_far, branch_free, split)

    lax.fori_loop(0, nk // STAGE_GROUP, lambda j, c: run_group(j * STAGE_GROUP, STAGE_GROUP, c), bias0)

    o1 = acc_scr[0:V_DIM, 0:tq] / acc_scr[V_DIM:V_DIM + 1, 0:tq]
    o2 = acc_scr[0:V_DIM, tq:2 * tq] / acc_scr[V_DIM:V_DIM + 1, tq:2 * tq]
    o = o1 - lam * o2
    o = o * lax.rsqrt(jnp.mean(o * o, axis=0, keepdims=True) + LN_EPS)
    o = o * sg_ref[...] * (1.0 - lambda_init)
    o_ref[pl.ds(pl.multiple_of(qi * tq, tq), tq), :] = jnp.transpose(o).astype(BF16)


def _attention(qt, k, vt, positions, bmin, bmax, rel_table, lam_params, subln_g, lambda_init):
    nq, D, tq = qt.shape
    S = nq * tq
    heads = D // V_DIM
    tk = TK
    nk = S // tk
    assert nk % STAGE_GROUP == 0, (S, tk)
    table2 = rel_table.astype(F32) * LOG2_E
    table_rows = jnp.zeros((heads, LANES), F32).at[:, :REL_BUCKETS].set(table2.T)
    kernel = functools.partial(_attn_kernel, lambda_init=lambda_init)
    grid_spec = pltpu.PrefetchScalarGridSpec(
        num_scalar_prefetch=3,
        grid=(heads,),
        in_specs=[
            pl.BlockSpec((nq, V_DIM, tq), lambda h, *_: (0, h, 0)),
            pl.BlockSpec((S, V_DIM), lambda h, *_: (0, h)),
            pl.BlockSpec((1, nk, V_DIM + ONES_ROWS, tk), lambda h, *_: (h, 0, 0, 0)),
            pl.BlockSpec((nq, 1, tq), lambda h, *_: (0, 0, 0)),
            pl.BlockSpec((nk, 1, tk), lambda h, *_: (0, 0, 0)),
            pl.BlockSpec((heads, LANES), lambda h, *_: (0, 0)),
            pl.BlockSpec((4, HEAD_DIM), lambda h, *_: (0, 0)),
            pl.BlockSpec((V_DIM, 1), lambda h, *_: (0, 0)),
        ],
        out_specs=pl.BlockSpec((S, V_DIM), lambda h, *_: (0, h)),
        scratch_shapes=[
            pltpu.VMEM((V_DIM, 2 * tq), BF16),
            pltpu.VMEM((tk, 2 * tq), F32),
            pltpu.VMEM((tk, 2 * tq), F32),
            pltpu.VMEM((1, 2 * tq), F32),
            pltpu.VMEM((1, 2 * tq), F32),
            pltpu.VMEM((1, 2 * tq), F32),
            pltpu.VMEM((V_DIM + ONES_ROWS, 2 * tq), F32),
        ],
    )
    return pl.pallas_call(
        kernel,
        grid_spec=grid_spec,
        out_shape=jax.ShapeDtypeStruct((S, D), BF16),
        compiler_params=_params("parallel"),
        name="diff_attention",
    )(bmin, bmax, table2.reshape(-1),
      qt, k, vt, positions.reshape(nq, 1, tq), positions.reshape(nk, 1, tk), table_rows,
      lam_params, subln_g.reshape(V_DIM, 1))


def _proj_ln_kernel(a_ref, x_ref, w_ref, lg_ref, lb_ref, o_ref):
    z = DEEPNORM_ALPHA * x_ref[...] + _dot(a_ref[...], w_ref[...])
    o_ref[...] = _layer_norm(z, lg_ref[...], lb_ref[...])


def _proj_ln(a, x, w, ln_g, ln_b):
    S, D = x.shape
    K = a.shape[1]
    tm = TM_PROJ
    return pl.pallas_call(
        _proj_ln_kernel,
        grid=(S // tm,),
        in_specs=[
            pl.BlockSpec((tm, K), lambda i: (i, 0)),
            pl.BlockSpec((tm, D), lambda i: (i, 0)),
            _const_spec((K, D)),
            _const_spec((1, D)),
            _const_spec((1, D)),
        ],
        out_specs=pl.BlockSpec((tm, D), lambda i: (i, 0)),
        out_shape=jax.ShapeDtypeStruct((S, D), F32),
        compiler_params=_params("parallel"),
        name="attn_out_proj",
    )(a, x, w.astype(BF16), ln_g.reshape(1, D), ln_b.reshape(1, D))


def _attn_layer(x, positions, bmin, bmax, rel_table, w_qkv, lam_params, subln_g, w_out,
                lambda_init, ln_g, ln_b):
    qt, k, vt = _qkv(x, w_qkv)
    o = _attention(qt, k, vt, positions, bmin, bmax, rel_table, lam_params, subln_g, lambda_init)
    return _proj_ln(o, x, w_out, ln_g, ln_b)


def _ffn_kernel(x_ref, xp_ref, xn_ref, wi_ref, cw_ref, cb_ref, wd_ref, lg_ref, lb_ref,
                o_ref, g_a, g_b, *, chunks):
    i = pl.program_id(0)
    x = x_ref[...]
    tm = x.shape[0]
    halo = xp_ref.shape[0]
    F = wd_ref.shape[0]
    xp = jnp.where(i > 0, xp_ref[...], 0.0)
    xn = jnp.where(i < pl.num_programs(0) - 1, xn_ref[...], 0.0)
    xe = jnp.concatenate([xp, x, xn], axis=0).astype(BF16)
    xb = x.astype(BF16)
    acc = DEEPNORM_ALPHA * x
    start = 0
    for idx, width in enumerate(chunks):
        cols = slice(start, start + width)
        g_scr = g_b if idx % 2 else g_a
        g_scr[:, 0:width] = _dot(xe, wi_ref[:, cols])
        cw = cw_ref[:, cols]
        gate = (g_scr[halo - 1:halo - 1 + tm, 0:width] * cw[0:1]
                + g_scr[halo:halo + tm, 0:width] * cw[1:2]
                + g_scr[halo + 1:halo + 1 + tm, 0:width] * cw[2:3]
                + cb_ref[:, cols])
        up = _dot(xb, wi_ref[:, F + start:F + start + width])
        hidden = (_gelu(gate) * up).astype(BF16)
        acc = acc + _dot(hidden, wd_ref[cols, :])
        start += width
    o_ref[...] = _layer_norm(acc, lg_ref[...], lb_ref[...])


def _ffn_layer(x, w_in, conv_w, conv_b, w_down, ln_g, ln_b):
    S, D = x.shape
    F = w_down.shape[0]
    tm, halo = TM_FFN, SUBLANES
    chunks = (FFN_CHUNK,) * (F // FFN_CHUNK) + ((F % FFN_CHUNK,) if F % FFN_CHUNK else ())
    blocks_per_tile = tm // halo
    last_halo_block = S // halo - 1
    return pl.pallas_call(
        functools.partial(_ffn_kernel, chunks=chunks),
        grid=(S // tm,),
        in_specs=[
            pl.BlockSpec((tm, D), lambda i: (i, 0)),
            pl.BlockSpec((halo, D), lambda i: (jnp.maximum(i * blocks_per_tile - 1, 0), 0)),
            pl.BlockSpec((halo, D),
                         lambda i: (jnp.minimum((i + 1) * blocks_per_tile, last_halo_block), 0)),
            _const_spec((D, 2 * F)),
            _const_spec((conv_w.shape[0], F)),
            _const_spec((1, F)),
            _const_spec((F, D)),
            _const_spec((1, D)),
            _const_spec((1, D)),
        ],
        out_specs=pl.BlockSpec((tm, D), lambda i: (i, 0)),
        out_shape=jax.ShapeDtypeStruct((S, D), F32),
        scratch_shapes=[pltpu.VMEM((tm + 2 * halo, max(chunks)), F32)] * 2,
        compiler_params=_params("parallel"),
        name="conv_glu_ffn",
    )(x, x, x, w_in.astype(BF16), conv_w, conv_b.reshape(1, F), w_down.astype(BF16),
      ln_g.reshape(1, D), ln_b.reshape(1, D))


def kernel(x, positions, rel_bias_table, a_w_in, a_norm_g, a_norm_b, a_w_s, a_b_s, a_w_out,
           b_w_qkv, b_lambda, b_subln_g, b_w_out, f_w_in, f_conv_w, f_conv_b, f_w_down,
           ln_g, ln_b):
    B, S, D = x.shape
    outs = []
    for b in range(B):
        xs = x[b]
        pos = positions[b]
        bmin, bmax = _pos_stats(pos)
        for i in range(DEPTH):
            j = i // 2
            if i % 2 == 0:
                xs = _gmlp_layer(xs, a_w_in[j], a_norm_g[j], a_norm_b[j], a_w_s[j], a_b_s[j],
                                 a_w_out[j], ln_g[i, 0], ln_b[i, 0])
            else:
                xs = _attn_layer(xs, pos, bmin, bmax, rel_bias_table, b_w_qkv[j], b_lambda[j],
                                 b_subln_g[j], b_w_out[j], _lambda_init(i), ln_g[i, 0], ln_b[i, 0])
            xs = _ffn_layer(xs, f_w_in[i], f_conv_w[i], f_conv_b[i], f_w_down[i],
                            ln_g[i, 1], ln_b[i, 1])
        outs.append(xs)
    return jnp.stack(outs)
```

```python
import functools
import math

import jax
import jax.numpy as jnp
from jax import lax
from jax.experimental import pallas as pl
from jax.experimental.pallas import tpu as pltpu

F32 = jnp.float32
BF16 = jnp.bfloat16

DEPTH = 4
A_CHUNK = 128
A_GROUPS = 8
HEAD_DIM = 64
V_DIM = 2 * HEAD_DIM
REL_BUCKETS = 32
REL_FAR = 128
LN_EPS = 1e-5
LOG2_E = math.log2(math.e)
DEEPNORM_ALPHA = (2 * DEPTH) ** 0.25
REL_CLIP = 2047
F32_MANTISSA_BITS = 23
F32_EXP_BIAS = 127

LANES = 128
SUBLANES = 8
BF16_ROWS = 16
VMEM_LIMIT = 56 * 1024 * 1024

TM_GMLP = 256
TQ = 512
TK = 256
TM_QKV = TK
TM_PROJ = 512
TM_FFN = 512
FFN_CHUNK = 1024
STAGE_GROUP = 8
MIN_STAGE_GROUP = 2
ONES_ROWS = BF16_ROWS


def _lambda_init(layer_idx):
    return 0.8 - 0.6 * math.exp(-0.3 * layer_idx)


def _gelu(x):
    return 0.5 * x * (1.0 + lax.erf(x * (1.0 / math.sqrt(2.0))))


def _layer_norm(z, g, b):
    mu = jnp.mean(z, axis=-1, keepdims=True)
    zc = z - mu
    var = jnp.mean(zc * zc, axis=-1, keepdims=True)
    return zc * lax.rsqrt(var + LN_EPS) * g + b


def _dot(a, b):
    return jnp.dot(a, b, preferred_element_type=F32)


def _dot_nt(a, b):
    return lax.dot_general(a, b, (((1,), (1,)), ((), ())), preferred_element_type=F32)


def _const_spec(shape):
    nd = len(shape)
    return pl.BlockSpec(shape, lambda *_: (0,) * nd, pipeline_mode=pl.Buffered(1))


def _params(*sem):
    return pltpu.CompilerParams(dimension_semantics=sem, vmem_limit_bytes=VMEM_LIMIT)


def _gmlp_kernel(x_ref, wi_ref, ng_ref, nb_ref, ws_ref, bs_ref, wo_ref, lg_ref, lb_ref, o_ref):
    x = x_ref[...]
    tm = x.shape[0]
    W = wo_ref.shape[0]
    gd = W // A_GROUPS
    hidden = _gelu(_dot(x.astype(BF16), wi_ref[...]))
    v = _layer_norm(hidden[:, W:], ng_ref[...], nb_ref[...]).astype(BF16)
    mixed = []
    for c in range(tm // A_CHUNK):
        row = []
        for g in range(A_GROUPS):
            vc = v[c * A_CHUNK:(c + 1) * A_CHUNK, g * gd:(g + 1) * gd]
            row.append(_dot(ws_ref[g], vc) + bs_ref[g])
        mixed.append(jnp.concatenate(row, axis=1))
    y = (hidden[:, :W] * jnp.concatenate(mixed, axis=0)).astype(BF16)
    z = DEEPNORM_ALPHA * x + _dot(y, wo_ref[...])
    o_ref[...] = _layer_norm(z, lg_ref[...], lb_ref[...])


def _gmlp_layer(x, w_in, norm_g, norm_b, w_s, b_s, w_out, ln_g, ln_b):
    S, D = x.shape
    W = w_out.shape[0]
    tm = TM_GMLP
    return pl.pallas_call(
        _gmlp_kernel,
        grid=(S // tm,),
        in_specs=[
            pl.BlockSpec((tm, D), lambda i: (i, 0)),
            _const_spec((D, 2 * W)),
            _const_spec((1, W)),
            _const_spec((1, W)),
            _const_spec((A_GROUPS, A_CHUNK, A_CHUNK)),
            _const_spec((A_GROUPS, A_CHUNK, 1)),
            _const_spec((W, D)),
            _const_spec((1, D)),
            _const_spec((1, D)),
        ],
        out_specs=pl.BlockSpec((tm, D), lambda i: (i, 0)),
        out_shape=jax.ShapeDtypeStruct((S, D), F32),
        compiler_params=_params("parallel"),
        name="gmlp_layer",
    )(x, w_in.astype(BF16), norm_g.reshape(1, W), norm_b.reshape(1, W), w_s.astype(BF16),
      b_s.reshape(A_GROUPS, A_CHUNK, 1), w_out.astype(BF16), ln_g.reshape(1, D), ln_b.reshape(1, D))


def _pos_stats_kernel(p_ref, mn_ref, mx_ref):
    p = p_ref[...]
    mn_ref[...] = jnp.min(p, axis=1, keepdims=True)
    mx_ref[...] = jnp.max(p, axis=1, keepdims=True)


def _pos_stats(positions):
    nb = positions.shape[0] // LANES
    mn, mx = pl.pallas_call(
        _pos_stats_kernel,
        out_shape=(jax.ShapeDtypeStruct((nb, 1), jnp.int32),) * 2,
        name="pos_stats",
    )(positions.reshape(nb, LANES))
    return mn.reshape(nb), mx.reshape(nb)


def _qkv_kernel(x_ref, wqt_ref, wk_ref, wvt_ref, qt_ref, k_ref, vt_ref):
    xb = x_ref[...].astype(BF16)
    tm = xb.shape[0]
    qt_ref[...] = _dot_nt(wqt_ref[...], xb).astype(BF16)
    k_ref[...] = _dot(xb, wk_ref[...]).astype(BF16)
    vt = _dot_nt(wvt_ref[...], xb).astype(BF16)
    heads = vt.shape[0] // V_DIM
    vt_ref[:, 0, 0:V_DIM, :] = vt.reshape(heads, V_DIM, tm)
    vt_ref[:, 0, V_DIM:, :] = jnp.ones((heads, ONES_ROWS, tm), BF16)


def _qkv(x, w_qkv):
    S, D = x.shape
    tm = TM_QKV
    heads = D // V_DIM
    wqt = (w_qkv[:, :D] * (HEAD_DIM ** -0.5 * LOG2_E)).T.astype(BF16)
    wk = w_qkv[:, D:2 * D].astype(BF16)
    wvt = w_qkv[:, 2 * D:].T.astype(BF16)
    return pl.pallas_call(
        _qkv_kernel,
        grid=(S // tm,),
        in_specs=[
            pl.BlockSpec((tm, D), lambda i: (i, 0)),
            _const_spec((D, D)),
            _const_spec((D, D)),
            _const_spec((D, D)),
        ],
        out_specs=[
            pl.BlockSpec((D, tm), lambda i: (0, i)),
            pl.BlockSpec((tm, D), lambda i: (i, 0)),
            pl.BlockSpec((heads, 1, V_DIM + ONES_ROWS, tm), lambda i: (0, i, 0, 0)),
        ],
        out_shape=[
            jax.ShapeDtypeStruct((D, S), BF16),
            jax.ShapeDtypeStruct((S, D), BF16),
            jax.ShapeDtypeStruct((heads, S // tm, V_DIM + ONES_ROWS, tm), BF16),
        ],
        compiler_params=_params("parallel"),
        name="qkv_proj",
    )(x, wqt, wk, wvt)


def _bias_tile(posk_row, posq_row, table_row):
    tk = posk_row.shape[1]
    tq = posq_row.shape[1]
    table_sq = jnp.broadcast_to(table_row, (LANES, LANES))
    rows = []
    for c in range(tk // LANES):
        pk = posk_row[:, c * LANES:(c + 1) * LANES]
        pk_col = jnp.transpose(jnp.broadcast_to(pk, (LANES, LANES)))
        cols = []
        for d in range(tq // LANES):
            rel = pk_col - posq_row[:, d * LANES:(d + 1) * LANES]
            n = jnp.abs(rel)
            nsq = jnp.square(jnp.minimum(n, REL_CLIP)).astype(F32)
            log2_nsq = (lax.bitcast_convert_type(nsq, jnp.int32) >> F32_MANTISSA_BITS) - F32_EXP_BIAS
            large = jnp.minimum(log2_nsq + 2, REL_BUCKETS // 2 - 1)
            bucket = jnp.where(n < REL_BUCKETS // 4, n, large)
            bucket = bucket + jnp.where(rel > 0, REL_BUCKETS // 2, 0)
            cols.append(jnp.take_along_axis(table_sq, bucket, axis=1))
        rows.append(jnp.concatenate(cols, axis=1))
    return jnp.concatenate(rows, axis=0)


def _attn_kernel(bmin_ref, bmax_ref, tbl_ref,
                 qt_ref, k_ref, vt_ref, posq_ref, posk_ref, tblv_ref, lam_ref, sg_ref,
                 o_ref, qcat, s_a, s_b, smax_a, smax_b, m_scr, acc_scr, *, nk, lambda_init):
    h = pl.program_id(0)
    qi = pl.program_id(1)
    tq = qt_ref.shape[1]
    tk = vt_ref.shape[3]
    heads = pl.num_programs(0)

    q = qt_ref[...]
    row = lax.broadcasted_iota(jnp.int32, q.shape, 0)
    zero = jnp.zeros_like(q)
    qcat[:, 0:tq] = jnp.where(row < HEAD_DIM, q, zero)
    qcat[:, tq:2 * tq] = jnp.where(row >= HEAD_DIM, q, zero)

    qmin = bmin_ref[qi * (tq // LANES)]
    qmax = bmax_ref[qi * (tq // LANES)]
    for r in range(1, tq // LANES):
        qmin = jnp.minimum(qmin, bmin_ref[qi * (tq // LANES) + r])
        qmax = jnp.maximum(qmax, bmax_ref[qi * (tq // LANES) + r])
    bias_before = tbl_ref[(REL_BUCKETS // 2 - 1) * heads + h]
    bias_after = tbl_ref[(REL_BUCKETS - 1) * heads + h]

    acc_scr[...] = jnp.zeros_like(acc_scr)
    m_scr[...] = jnp.full(m_scr.shape, -1e30, F32)

    def classify(kt):
        kmin = bmin_ref[kt * (tk // LANES)]
        kmax = bmax_ref[kt * (tk // LANES)]
        for r in range(1, tk // LANES):
            kmin = jnp.minimum(kmin, bmin_ref[kt * (tk // LANES) + r])
            kmax = jnp.maximum(kmax, bmax_ref[kt * (tk // LANES) + r])
        all_after = kmin - qmax >= REL_FAR
        all_before = kmax - qmin <= -REL_FAR
        near = jnp.logical_not(jnp.logical_or(all_after, all_before))
        const_bias = jnp.where(all_after, bias_after, jnp.where(all_before, bias_before, 0.0))
        return near, const_bias

    def logits(kt, s_ref, smax_ref):
        kk = k_ref[pl.ds(pl.multiple_of(kt * tk, tk), tk), :]
        s = _dot(kk, qcat[...])
        s_ref[...] = s
        smax_ref[...] = jnp.max(s, axis=0, keepdims=True)

    def add_near_bias(kt, s_ref, smax_ref):
        bias = _bias_tile(posk_ref[kt], posq_ref[...], tblv_ref[pl.ds(h, 1), :])
        s = s_ref[...] + jnp.concatenate([bias, bias], axis=1)
        s_ref[...] = s
        smax_ref[...] = jnp.max(s, axis=0, keepdims=True)

    def softmax_update(kt, s_ref, smax_ref, const_bias):
        m_old = m_scr[...]
        m_new = jnp.maximum(m_old, smax_ref[...] + const_bias)
        p = jnp.exp2(s_ref[...] - (m_new - const_bias)).astype(BF16)
        acc_scr[...] = jnp.exp2(m_old - m_new) * acc_scr[...] + _dot(vt_ref[0, kt], p)
        m_scr[...] = m_new

    def stage(kt, cur, nxt, const_bias, fix_next):
        kn = jnp.minimum(kt + 1, nk - 1)
        logits(kn, *nxt)
        softmax_update(kt, *cur, const_bias)
        near_next, bias_next = classify(kn)
        if fix_next:
            @pl.when(near_next)
            def _():
                add_near_bias(kn, *nxt)
        return near_next, bias_next

    buf_a = (s_a, smax_a)
    buf_b = (s_b, smax_b)
    near0, bias0 = classify(0)
    logits(0, *buf_a)

    @pl.when(near0)
    def _():
        add_near_bias(0, *buf_a)

    def pair(kt, const_bias):
        _, const_bias = stage(kt, buf_a, buf_b, const_bias, True)
        _, const_bias = stage(kt + 1, buf_b, buf_a, const_bias, True)
        return const_bias

    def run_group(base, size, const_bias):
        inner_far = jnp.bool_(True)
        for i in range(1, size):
            inner_far = jnp.logical_and(inner_far, jnp.logical_not(classify(base + i)[0]))

        def branch_free():
            c = const_bias
            for i in range(size):
                cur, nxt = (buf_a, buf_b) if i % 2 == 0 else (buf_b, buf_a)
                _, c = stage(base + i, cur, nxt, c, i == size - 1)
            return c

        def split():
            if size == MIN_STAGE_GROUP:
                return lax.fori_loop(0, size // 2, lambda t, c: pair(base + 2 * t, c), const_bias)
            half = size // 2
            return lax.fori_loop(0, 2, lambda t, c: run_group(base + t * half, half, c), const_bias)

        return lax.cond(inner_far, branch_free, split)

    lax.fori_loop(0, nk // STAGE_GROUP, lambda j, c: run_group(j * STAGE_GROUP, STAGE_GROUP, c), bias0)

    lp = lam_ref[...]
    lam = (jnp.exp(jnp.sum(lp[0:1] * lp[1:2], axis=1, keepdims=True))
           - jnp.exp(jnp.sum(lp[2:3] * lp[3:4], axis=1, keepdims=True)) + lambda_init)
    o1 = acc_scr[0:V_DIM, 0:tq] / acc_scr[V_DIM:V_DIM + 1, 0:tq]
    o2 = acc_scr[0:V_DIM, tq:2 * tq] / acc_scr[V_DIM:V_DIM + 1, tq:2 * tq]
    o = o1 - lam * o2
    o = o * lax.rsqrt(jnp.mean(o * o, axis=0, keepdims=True) + LN_EPS)
    o = o * sg_ref[...] * (1.0 - lambda_init)
    o_ref[...] = jnp.transpose(o).astype(BF16)


def _attention(qt, k, vt, positions, bmin, bmax, rel_table, lam_params, subln_g, lambda_init):
    D, S = qt.shape
    heads = D // V_DIM
    tq, tk = TQ, TK
    nk = S // tk
    assert S % tq == 0 and nk % STAGE_GROUP == 0, (S, tq, tk)
    table2 = rel_table.astype(F32) * LOG2_E
    table_rows = jnp.zeros((heads, LANES), F32).at[:, :REL_BUCKETS].set(table2.T)
    kernel = functools.partial(_attn_kernel, nk=nk, lambda_init=lambda_init)
    grid_spec = pltpu.PrefetchScalarGridSpec(
        num_scalar_prefetch=3,
        grid=(heads, S // tq),
        in_specs=[
            pl.BlockSpec((V_DIM, tq), lambda h, i, *_: (h, i)),
            pl.BlockSpec((S, V_DIM), lambda h, i, *_: (0, h)),
            pl.BlockSpec((1, nk, V_DIM + ONES_ROWS, tk), lambda h, i, *_: (h, 0, 0, 0)),
            pl.BlockSpec((1, tq), lambda h, i, *_: (0, i)),
            pl.BlockSpec((nk, 1, tk), lambda h, i, *_: (0, 0, 0)),
            pl.BlockSpec((heads, LANES), lambda h, i, *_: (0, 0)),
            pl.BlockSpec((4, HEAD_DIM), lambda h, i, *_: (0, 0)),
            pl.BlockSpec((V_DIM, 1), lambda h, i, *_: (0, 0)),
        ],
        out_specs=pl.BlockSpec((tq, V_DIM), lambda h, i, *_: (i, h)),
        scratch_shapes=[
            pltpu.VMEM((V_DIM, 2 * tq), BF16),
            pltpu.VMEM((tk, 2 * tq), F32),
            pltpu.VMEM((tk, 2 * tq), F32),
            pltpu.VMEM((1, 2 * tq), F32),
            pltpu.VMEM((1, 2 * tq), F32),
            pltpu.VMEM((1, 2 * tq), F32),
            pltpu.VMEM((V_DIM + ONES_ROWS, 2 * tq), F32),
        ],
    )
    return pl.pallas_call(
        kernel,
        grid_spec=grid_spec,
        out_shape=jax.ShapeDtypeStruct((S, D), BF16),
        compiler_params=_params("parallel", "parallel"),
        name="diff_attention",
    )(bmin, bmax, table2.reshape(-1),
      qt, k, vt, positions.reshape(1, S), positions.reshape(nk, 1, tk), table_rows,
      lam_params, subln_g.reshape(V_DIM, 1))


def _proj_ln_kernel(a_ref, x_ref, w_ref, lg_ref, lb_ref, o_ref):
    z = DEEPNORM_ALPHA * x_ref[...] + _dot(a_ref[...], w_ref[...])
    o_ref[...] = _layer_norm(z, lg_ref[...], lb_ref[...])


def _proj_ln(a, x, w, ln_g, ln_b):
    S, D = x.shape
    K = a.shape[1]
    tm = TM_PROJ
    return pl.pallas_call(
        _proj_ln_kernel,
        grid=(S // tm,),
        in_specs=[
            pl.BlockSpec((tm, K), lambda i: (i, 0)),
            pl.BlockSpec((tm, D), lambda i: (i, 0)),
            _const_spec((K, D)),
            _const_spec((1, D)),
            _const_spec((1, D)),
        ],
        out_specs=pl.BlockSpec((tm, D), lambda i: (i, 0)),
        out_shape=jax.ShapeDtypeStruct((S, D), F32),
        compiler_params=_params("parallel"),
        name="attn_out_proj",
    )(a, x, w.astype(BF16), ln_g.reshape(1, D), ln_b.reshape(1, D))


def _attn_layer(x, positions, bmin, bmax, rel_table, w_qkv, lam_params, subln_g, w_out,
                lambda_init, ln_g, ln_b):
    qt, k, vt = _qkv(x, w_qkv)
    o = _attention(qt, k, vt, positions, bmin, bmax, rel_table, lam_params, subln_g, lambda_init)
    return _proj_ln(o, x, w_out, ln_g, ln_b)


def _ffn_kernel(x_ref, xp_ref, xn_ref, wi_ref, cw_ref, cb_ref, wd_ref, lg_ref, lb_ref,
                o_ref, g_a, g_b, *, chunks):
    i = pl.program_id(0)
    x = x_ref[...]
    tm = x.shape[0]
    halo = xp_ref.shape[0]
    F = wd_ref.shape[0]
    xp = jnp.where(i > 0, xp_ref[...], 0.0)
    xn = jnp.where(i < pl.num_programs(0) - 1, xn_ref[...], 0.0)
    xe = jnp.concatenate([xp, x, xn], axis=0).astype(BF16)
    xb = x.astype(BF16)
    acc = DEEPNORM_ALPHA * x
    start = 0
    for idx, width in enumerate(chunks):
        cols = slice(start, start + width)
        g_scr = g_b if idx % 2 else g_a
        g_scr[:, 0:width] = _dot(xe, wi_ref[:, cols])
        cw = cw_ref[:, cols]
        gate = (g_scr[halo - 1:halo - 1 + tm, 0:width] * cw[0:1]
                + g_scr[halo:halo + tm, 0:width] * cw[1:2]
                + g_scr[halo + 1:halo + 1 + tm, 0:width] * cw[2:3]
                + cb_ref[:, cols])
        up = _dot(xb, wi_ref[:, F + start:F + start + width])
        hidden = (_gelu(gate) * up).astype(BF16)
        acc = acc + _dot(hidden, wd_ref[cols, :])
        start += width
    o_ref[...] = _layer_norm(acc, lg_ref[...], lb_ref[...])


def _ffn_layer(x, w_in, conv_w, conv_b, w_down, ln_g, ln_b):
    S, D = x.shape
    F = w_down.shape[0]
    tm, halo = TM_FFN, SUBLANES
    chunks = (FFN_CHUNK,) * (F // FFN_CHUNK) + ((F % FFN_CHUNK,) if F % FFN_CHUNK else ())
    blocks_per_tile = tm // halo
    last_halo_block = S // halo - 1
    return pl.pallas_call(
        functools.partial(_ffn_kernel, chunks=chunks),
        grid=(S // tm,),
        in_specs=[
            pl.BlockSpec((tm, D), lambda i: (i, 0)),
            pl.BlockSpec((halo, D), lambda i: (jnp.maximum(i * blocks_per_tile - 1, 0), 0)),
            pl.BlockSpec((halo, D),
                         lambda i: (jnp.minimum((i + 1) * blocks_per_tile, last_halo_block), 0)),
            _const_spec((D, 2 * F)),
            _const_spec((conv_w.shape[0], F)),
            _const_spec((1, F)),
            _const_spec((F, D)),
            _const_spec((1, D)),
            _const_spec((1, D)),
        ],
        out_specs=pl.BlockSpec((tm, D), lambda i: (i, 0)),
        out_shape=jax.ShapeDtypeStruct((S, D), F32),
        scratch_shapes=[pltpu.VMEM((tm + 2 * halo, max(chunks)), F32)] * 2,
        compiler_params=_params("parallel"),
        name="conv_glu_ffn",
    )(x, x, x, w_in.astype(BF16), conv_w, conv_b.reshape(1, F), w_down.astype(BF16),
      ln_g.reshape(1, D), ln_b.reshape(1, D))


def kernel(x, positions, rel_bias_table, a_w_in, a_norm_g, a_norm_b, a_w_s, a_b_s, a_w_out,
           b_w_qkv, b_lambda, b_subln_g, b_w_out, f_w_in, f_conv_w, f_conv_b, f_w_down,
           ln_g, ln_b):
    B, S, D = x.shape
    outs = []
    for b in range(B):
        xs = x[b]
        pos = positions[b]
        bmin, bmax = _pos_stats(pos)
        for i in range(DEPTH):
            j = i // 2
            if i % 2 == 0:
                xs = _gmlp_layer(xs, a_w_in[j], a_norm_g[j], a_norm_b[j], a_w_s[j], a_b_s[j],
                                 a_w_out[j], ln_g[i, 0], ln_b[i, 0])
            else:
                xs = _attn_layer(xs, pos, bmin, bmax, rel_bias_table, b_w_qkv[j], b_lambda[j],
                                 b_subln_g[j], b_w_out[j], _lambda_init(i), ln_g[i, 0], ln_b[i, 0])
            xs = _ffn_layer(xs, f_w_in[i], f_conv_w[i], f_conv_b[i], f_w_down[i],
                            ln_g[i, 1], ln_b[i, 1])
        outs.append(xs)
    return jnp.stack(outs)
```

```python
import functools
import math

import jax
import jax.numpy as jnp
from jax import lax
from jax.experimental import pallas as pl
from jax.experimental.pallas import tpu as pltpu

F32 = jnp.float32
BF16 = jnp.bfloat16

DEPTH = 4
A_CHUNK = 128
A_GROUPS = 8
HEAD_DIM = 64
V_DIM = 2 * HEAD_DIM
REL_BUCKETS = 32
REL_FAR = 128
LN_EPS = 1e-5
LOG2_E = math.log2(math.e)
DEEPNORM_ALPHA = (2 * DEPTH) ** 0.25
REL_CLIP = 2047
F32_MANTISSA_BITS = 23
F32_EXP_BIAS = 127

LANES = 128
SUBLANES = 8
BF16_ROWS = 16
VMEM_LIMIT = 56 * 1024 * 1024

TM_GMLP = 256
TQ = 512
TK = 512
TM_QKV = TK
TM_PROJ = 512
TM_FFN = 512
FFN_CHUNK = 1024
STAGE_GROUP = 4
ONES_ROWS = BF16_ROWS


def _lambda_init(layer_idx):
    return 0.8 - 0.6 * math.exp(-0.3 * layer_idx)


def _gelu(x):
    return 0.5 * x * (1.0 + lax.erf(x * (1.0 / math.sqrt(2.0))))


def _layer_norm(z, g, b):
    mu = jnp.mean(z, axis=-1, keepdims=True)
    zc = z - mu
    var = jnp.mean(zc * zc, axis=-1, keepdims=True)
    return zc * lax.rsqrt(var + LN_EPS) * g + b


def _dot(a, b):
    return jnp.dot(a, b, preferred_element_type=F32)


def _dot_nt(a, b):
    return lax.dot_general(a, b, (((1,), (1,)), ((), ())), preferred_element_type=F32)


def _const_spec(shape):
    nd = len(shape)
    return pl.BlockSpec(shape, lambda *_: (0,) * nd, pipeline_mode=pl.Buffered(1))


def _params(*sem):
    return pltpu.CompilerParams(dimension_semantics=sem, vmem_limit_bytes=VMEM_LIMIT)


def _gmlp_kernel(x_ref, wi_ref, ng_ref, nb_ref, ws_ref, bs_ref, wo_ref, lg_ref, lb_ref, o_ref):
    x = x_ref[...]
    tm = x.shape[0]
    W = wo_ref.shape[0]
    gd = W // A_GROUPS
    hidden = _gelu(_dot(x.astype(BF16), wi_ref[...]))
    v = _layer_norm(hidden[:, W:], ng_ref[...], nb_ref[...]).astype(BF16)
    mixed = []
    for c in range(tm // A_CHUNK):
        row = []
        for g in range(A_GROUPS):
            vc = v[c * A_CHUNK:(c + 1) * A_CHUNK, g * gd:(g + 1) * gd]
            row.append(_dot(ws_ref[g], vc) + bs_ref[g])
        mixed.append(jnp.concatenate(row, axis=1))
    y = (hidden[:, :W] * jnp.concatenate(mixed, axis=0)).astype(BF16)
    z = DEEPNORM_ALPHA * x + _dot(y, wo_ref[...])
    o_ref[...] = _layer_norm(z, lg_ref[...], lb_ref[...])


def _gmlp_layer(x, w_in, norm_g, norm_b, w_s, b_s, w_out, ln_g, ln_b):
    S, D = x.shape
    W = w_out.shape[0]
    tm = TM_GMLP
    return pl.pallas_call(
        _gmlp_kernel,
        grid=(S // tm,),
        in_specs=[
            pl.BlockSpec((tm, D), lambda i: (i, 0)),
            _const_spec((D, 2 * W)),
            _const_spec((1, W)),
            _const_spec((1, W)),
            _const_spec((A_GROUPS, A_CHUNK, A_CHUNK)),
            _const_spec((A_GROUPS, A_CHUNK, 1)),
            _const_spec((W, D)),
            _const_spec((1, D)),
            _const_spec((1, D)),
        ],
        out_specs=pl.BlockSpec((tm, D), lambda i: (i, 0)),
        out_shape=jax.ShapeDtypeStruct((S, D), F32),
        compiler_params=_params("parallel"),
        name="gmlp_layer",
    )(x, w_in.astype(BF16), norm_g.reshape(1, W), norm_b.reshape(1, W), w_s.astype(BF16),
      b_s.reshape(A_GROUPS, A_CHUNK, 1), w_out.astype(BF16), ln_g.reshape(1, D), ln_b.reshape(1, D))


def _pos_stats_kernel(p_ref, mn_ref, mx_ref):
    p = p_ref[...]
    mn_ref[...] = jnp.min(p, axis=1, keepdims=True)
    mx_ref[...] = jnp.max(p, axis=1, keepdims=True)


def _pos_stats(positions):
    nb = positions.shape[0] // LANES
    mn, mx = pl.pallas_call(
        _pos_stats_kernel,
        out_shape=(jax.ShapeDtypeStruct((nb, 1), jnp.int32),) * 2,
        name="pos_stats",
    )(positions.reshape(nb, LANES))
    return mn.reshape(nb), mx.reshape(nb)


def _qkv_kernel(x_ref, wqt_ref, wk_ref, wvt_ref, qt_ref, k_ref, vt_ref):
    xb = x_ref[...].astype(BF16)
    tm = xb.shape[0]
    qt_ref[...] = _dot_nt(wqt_ref[...], xb).astype(BF16)
    k_ref[...] = _dot(xb, wk_ref[...]).astype(BF16)
    vt = _dot_nt(wvt_ref[...], xb).astype(BF16)
    heads = vt.shape[0] // V_DIM
    vt_ref[:, 0, 0:V_DIM, :] = vt.reshape(heads, V_DIM, tm)
    vt_ref[:, 0, V_DIM:, :] = jnp.ones((heads, ONES_ROWS, tm), BF16)


def _qkv(x, w_qkv):
    S, D = x.shape
    tm = TM_QKV
    heads = D // V_DIM
    wqt = (w_qkv[:, :D] * (HEAD_DIM ** -0.5 * LOG2_E)).T.astype(BF16)
    wk = w_qkv[:, D:2 * D].astype(BF16)
    wvt = w_qkv[:, 2 * D:].T.astype(BF16)
    return pl.pallas_call(
        _qkv_kernel,
        grid=(S // tm,),
        in_specs=[
            pl.BlockSpec((tm, D), lambda i: (i, 0)),
            _const_spec((D, D)),
            _const_spec((D, D)),
            _const_spec((D, D)),
        ],
        out_specs=[
            pl.BlockSpec((D, tm), lambda i: (0, i)),
            pl.BlockSpec((tm, D), lambda i: (i, 0)),
            pl.BlockSpec((heads, 1, V_DIM + ONES_ROWS, tm), lambda i: (0, i, 0, 0)),
        ],
        out_shape=[
            jax.ShapeDtypeStruct((D, S), BF16),
            jax.ShapeDtypeStruct((S, D), BF16),
            jax.ShapeDtypeStruct((heads, S // tm, V_DIM + ONES_ROWS, tm), BF16),
        ],
        compiler_params=_params("parallel"),
        name="qkv_proj",
    )(x, wqt, wk, wvt)


def _bias_tile(posk_row, posq_row, table_row):
    tk = posk_row.shape[1]
    tq = posq_row.shape[1]
    table_sq = jnp.broadcast_to(table_row, (LANES, LANES))
    rows = []
    for c in range(tk // LANES):
        pk = posk_row[:, c * LANES:(c + 1) * LANES]
        pk_col = jnp.transpose(jnp.broadcast_to(pk, (LANES, LANES)))
        cols = []
        for d in range(tq // LANES):
            rel = pk_col - posq_row[:, d * LANES:(d + 1) * LANES]
            n = jnp.abs(rel)
            nsq = jnp.square(jnp.minimum(n, REL_CLIP)).astype(F32)
            log2_nsq = (lax.bitcast_convert_type(nsq, jnp.int32) >> F32_MANTISSA_BITS) - F32_EXP_BIAS
            large = jnp.minimum(log2_nsq + 2, REL_BUCKETS // 2 - 1)
            bucket = jnp.where(n < REL_BUCKETS // 4, n, large)
            bucket = bucket + jnp.where(rel > 0, REL_BUCKETS // 2, 0)
            cols.append(jnp.take_along_axis(table_sq, bucket, axis=1))
        rows.append(jnp.concatenate(cols, axis=1))
    return jnp.concatenate(rows, axis=0)


def _attn_kernel(bmin_ref, bmax_ref, tbl_ref,
                 qt_ref, k_ref, vt_ref, posq_ref, posk_ref, tblv_ref, lam_ref, sg_ref,
                 o_ref, qcat, s_bufs, smax_bufs, m_scr, acc_scr, *, nk, lambda_init):
    h = pl.program_id(0)
    qi = pl.program_id(1)
    tq = qt_ref.shape[1]
    tk = vt_ref.shape[3]
    heads = pl.num_programs(0)

    q = qt_ref[...]
    row = lax.broadcasted_iota(jnp.int32, q.shape, 0)
    zero = jnp.zeros_like(q)
    qcat[:, 0:tq] = jnp.where(row < HEAD_DIM, q, zero)
    qcat[:, tq:2 * tq] = jnp.where(row >= HEAD_DIM, q, zero)

    qmin = bmin_ref[qi * (tq // LANES)]
    qmax = bmax_ref[qi * (tq // LANES)]
    for r in range(1, tq // LANES):
        qmin = jnp.minimum(qmin, bmin_ref[qi * (tq // LANES) + r])
        qmax = jnp.maximum(qmax, bmax_ref[qi * (tq // LANES) + r])
    bias_before = tbl_ref[(REL_BUCKETS // 2 - 1) * heads + h]
    bias_after = tbl_ref[(REL_BUCKETS - 1) * heads + h]

    acc_scr[...] = jnp.zeros_like(acc_scr)
    m_scr[...] = jnp.full(m_scr.shape, -1e30, F32)

    def classify(kt):
        kmin = bmin_ref[kt * (tk // LANES)]
        kmax = bmax_ref[kt * (tk // LANES)]
        for r in range(1, tk // LANES):
            kmin = jnp.minimum(kmin, bmin_ref[kt * (tk // LANES) + r])
            kmax = jnp.maximum(kmax, bmax_ref[kt * (tk // LANES) + r])
        all_after = kmin - qmax >= REL_FAR
        all_before = kmax - qmin <= -REL_FAR
        near = jnp.logical_not(jnp.logical_or(all_after, all_before))
        const_bias = jnp.where(all_after, bias_after, jnp.where(all_before, bias_before, 0.0))
        return near, const_bias

    def logits(kt, slot):
        kk = k_ref[pl.ds(pl.multiple_of(kt * tk, tk), tk), :]
        s = _dot(kk, qcat[...])
        s_bufs[slot] = s
        smax_bufs[slot] = jnp.max(s, axis=0, keepdims=True)

    def add_near_bias(kt, slot):
        bias = _bias_tile(posk_ref[kt], posq_ref[...], tblv_ref[pl.ds(h, 1), :])
        s = s_bufs[slot] + jnp.concatenate([bias, bias], axis=1)
        s_bufs[slot] = s
        smax_bufs[slot] = jnp.max(s, axis=0, keepdims=True)

    def softmax_update(kt, slot):
        const_bias = classify(kt)[1]
        m_old = m_scr[...]
        m_new = jnp.maximum(m_old, smax_bufs[slot] + const_bias)
        p = jnp.exp2(s_bufs[slot] - (m_new - const_bias)).astype(BF16)
        acc_scr[...] = jnp.exp2(m_old - m_new) * acc_scr[...] + _dot(vt_ref[0, kt], p)
        m_scr[...] = m_new

    def fix_near_tiles(base, first_slot):
        def one(i, carry):
            @pl.when(classify(base + i)[0])
            def _():
                add_near_bias(base + i, first_slot + i)
            return carry

        lax.fori_loop(0, STAGE_GROUP, one, 0)

    def group(base, cur, nxt, prefetch):
        for i in range(STAGE_GROUP):
            if prefetch:
                logits(base + STAGE_GROUP + i, nxt + i)
            softmax_update(base + i, cur + i)
        if prefetch:
            fix_near_tiles(base + STAGE_GROUP, nxt)

    for i in range(STAGE_GROUP):
        logits(i, i)
    fix_near_tiles(0, 0)

    def two_groups(j, carry):
        base = 2 * STAGE_GROUP * j
        group(base, 0, STAGE_GROUP, True)
        group(base + STAGE_GROUP, STAGE_GROUP, 0, True)
        return carry

    n_groups = nk // STAGE_GROUP
    lax.fori_loop(0, n_groups // 2 - 1, two_groups, 0)
    group((n_groups - 2) * STAGE_GROUP, 0, STAGE_GROUP, True)
    group((n_groups - 1) * STAGE_GROUP, STAGE_GROUP, 0, False)

    lp = lam_ref[...]
    lam = (jnp.exp(jnp.sum(lp[0:1] * lp[1:2], axis=1, keepdims=True))
           - jnp.exp(jnp.sum(lp[2:3] * lp[3:4], axis=1, keepdims=True)) + lambda_init)
    o1 = acc_scr[0:V_DIM, 0:tq] / acc_scr[V_DIM:V_DIM + 1, 0:tq]
    o2 = acc_scr[0:V_DIM, tq:2 * tq] / acc_scr[V_DIM:V_DIM + 1, tq:2 * tq]
    o = o1 - lam * o2
    o = o * lax.rsqrt(jnp.mean(o * o, axis=0, keepdims=True) + LN_EPS)
    o = o * sg_ref[...] * (1.0 - lambda_init)
    o_ref[...] = jnp.transpose(o).astype(BF16)


def _attention(qt, k, vt, positions, bmin, bmax, rel_table, lam_params, subln_g, lambda_init):
    D, S = qt.shape
    heads = D // V_DIM
    tq, tk = TQ, TK
    nk = S // tk
    assert S % tq == 0 and nk % (2 * STAGE_GROUP) == 0, (S, tq, tk)
    table2 = rel_table.astype(F32) * LOG2_E
    table_rows = jnp.zeros((heads, LANES), F32).at[:, :REL_BUCKETS].set(table2.T)
    kernel = functools.partial(_attn_kernel, nk=nk, lambda_init=lambda_init)
    grid_spec = pltpu.PrefetchScalarGridSpec(
        num_scalar_prefetch=3,
        grid=(heads, S // tq),
        in_specs=[
            pl.BlockSpec((V_DIM, tq), lambda h, i, *_: (h, i)),
            pl.BlockSpec((S, V_DIM), lambda h, i, *_: (0, h)),
            pl.BlockSpec((1, nk, V_DIM + ONES_ROWS, tk), lambda h, i, *_: (h, 0, 0, 0)),
            pl.BlockSpec((1, tq), lambda h, i, *_: (0, i)),
            pl.BlockSpec((nk, 1, tk), lambda h, i, *_: (0, 0, 0)),
            pl.BlockSpec((heads, LANES), lambda h, i, *_: (0, 0)),
            pl.BlockSpec((4, HEAD_DIM), lambda h, i, *_: (0, 0)),
            pl.BlockSpec((V_DIM, 1), lambda h, i, *_: (0, 0)),
        ],
        out_specs=pl.BlockSpec((tq, V_DIM), lambda h, i, *_: (i, h)),
        scratch_shapes=[
            pltpu.VMEM((V_DIM, 2 * tq), BF16),
            pltpu.VMEM((2 * STAGE_GROUP, tk, 2 * tq), F32),
            pltpu.VMEM((2 * STAGE_GROUP, 1, 2 * tq), F32),
            pltpu.VMEM((1, 2 * tq), F32),
            pltpu.VMEM((V_DIM + ONES_ROWS, 2 * tq), F32),
        ],
    )
    return pl.pallas_call(
        kernel,
        grid_spec=grid_spec,
        out_shape=jax.ShapeDtypeStruct((S, D), BF16),
        compiler_params=_params("parallel", "parallel"),
        name="diff_attention",
    )(bmin, bmax, table2.reshape(-1),
      qt, k, vt, positions.reshape(1, S), positions.reshape(nk, 1, tk), table_rows,
      lam_params, subln_g.reshape(V_DIM, 1))


def _proj_ln_kernel(a_ref, x_ref, w_ref, lg_ref, lb_ref, o_ref):
    z = DEEPNORM_ALPHA * x_ref[...] + _dot(a_ref[...], w_ref[...])
    o_ref[...] = _layer_norm(z, lg_ref[...], lb_ref[...])


def _proj_ln(a, x, w, ln_g, ln_b):
    S, D = x.shape
    K = a.shape[1]
    tm = TM_PROJ
    return pl.pallas_call(
        _proj_ln_kernel,
        grid=(S // tm,),
        in_specs=[
            pl.BlockSpec((tm, K), lambda i: (i, 0)),
            pl.BlockSpec((tm, D), lambda i: (i, 0)),
            _const_spec((K, D)),
            _const_spec((1, D)),
            _const_spec((1, D)),
        ],
        out_specs=pl.BlockSpec((tm, D), lambda i: (i, 0)),
        out_shape=jax.ShapeDtypeStruct((S, D), F32),
        compiler_params=_params("parallel"),
        name="attn_out_proj",
    )(a, x, w.astype(BF16), ln_g.reshape(1, D), ln_b.reshape(1, D))


def _attn_layer(x, positions, bmin, bmax, rel_table, w_qkv, lam_params, subln_g, w_out,
                lambda_init, ln_g, ln_b):
    qt, k, vt = _qkv(x, w_qkv)
    o = _attention(qt, k, vt, positions, bmin, bmax, rel_table, lam_params, subln_g, lambda_init)
    return _proj_ln(o, x, w_out, ln_g, ln_b)


def _ffn_kernel(x_ref, xp_ref, xn_ref, wi_ref, cw_ref, cb_ref, wd_ref, lg_ref, lb_ref,
                o_ref, g_a, g_b, *, chunks):
    i = pl.program_id(0)
    x = x_ref[...]
    tm = x.shape[0]
    halo = xp_ref.shape[0]
    F = wd_ref.shape[0]
    xp = jnp.where(i > 0, xp_ref[...], 0.0)
    xn = jnp.where(i < pl.num_programs(0) - 1, xn_ref[...], 0.0)
    xe = jnp.concatenate([xp, x, xn], axis=0).astype(BF16)
    xb = x.astype(BF16)
    acc = DEEPNORM_ALPHA * x
    start = 0
    for idx, width in enumerate(chunks):
        cols = slice(start, start + width)
        g_scr = g_b if idx % 2 else g_a
        g_scr[:, 0:width] = _dot(xe, wi_ref[:, cols])
        cw = cw_ref[:, cols]
        gate = (g_scr[halo - 1:halo - 1 + tm, 0:width] * cw[0:1]
                + g_scr[halo:halo + tm, 0:width] * cw[1:2]
                + g_scr[halo + 1:halo + 1 + tm, 0:width] * cw[2:3]
                + cb_ref[:, cols])
        up = _dot(xb, wi_ref[:, F + start:F + start + width])
        hidden = (_gelu(gate) * up).astype(BF16)
        acc = acc + _dot(hidden, wd_ref[cols, :])
        start += width
    o_ref[...] = _layer_norm(acc, lg_ref[...], lb_ref[...])


def _ffn_layer(x, w_in, conv_w, conv_b, w_down, ln_g, ln_b):
    S, D = x.shape
    F = w_down.shape[0]
    tm, halo = TM_FFN, SUBLANES
    chunks = (FFN_CHUNK,) * (F // FFN_CHUNK) + ((F % FFN_CHUNK,) if F % FFN_CHUNK else ())
    blocks_per_tile = tm // halo
    last_halo_block = S // halo - 1
    return pl.pallas_call(
        functools.partial(_ffn_kernel, chunks=chunks),
        grid=(S // tm,),
        in_specs=[
            pl.BlockSpec((tm, D), lambda i: (i, 0)),
            pl.BlockSpec((halo, D), lambda i: (jnp.maximum(i * blocks_per_tile - 1, 0), 0)),
            pl.BlockSpec((halo, D),
                         lambda i: (jnp.minimum((i + 1) * blocks_per_tile, last_halo_block), 0)),
            _const_spec((D, 2 * F)),
            _const_spec((conv_w.shape[0], F)),
            _const_spec((1, F)),
            _const_spec((F, D)),
            _const_spec((1, D)),
            _const_spec((1, D)),
        ],
        out_specs=pl.BlockSpec((tm, D), lambda i: (i, 0)),
        out_shape=jax.ShapeDtypeStruct((S, D), F32),
        scratch_shapes=[pltpu.VMEM((tm + 2 * halo, max(chunks)), F32)] * 2,
        compiler_params=_params("parallel"),
        name="conv_glu_ffn",
    )(x, x, x, w_in.astype(BF16), conv_w, conv_b.reshape(1, F), w_down.astype(BF16),
      ln_g.reshape(1, D), ln_b.reshape(1, D))


def kernel(x, positions, rel_bias_table, a_w_in, a_norm_g, a_norm_b, a_w_s, a_b_s, a_w_out,
           b_w_qkv, b_lambda, b_subln_g, b_w_out, f_w_in, f_conv_w, f_conv_b, f_w_down,
           ln_g, ln_b):
    B, S, D = x.shape
    outs = []
    for b in range(B):
        xs = x[b]
        pos = positions[b]
        bmin, bmax = _pos_stats(pos)
        for i in range(DEPTH):
            j = i // 2
            if i % 2 == 0:
                xs = _gmlp_layer(xs, a_w_in[j], a_norm_g[j], a_norm_b[j], a_w_s[j], a_b_s[j],
                                 a_w_out[j], ln_g[i, 0], ln_b[i, 0])
            else:
                xs = _attn_layer(xs, pos, bmin, bmax, rel_bias_table, b_w_qkv[j], b_lambda[j],
                                 b_subln_g[j], b_w_out[j], _lambda_init(i), ln_g[i, 0], ln_b[i, 0])
            xs = _ffn_layer(xs, f_w_in[i], f_conv_w[i], f_conv_b[i], f_w_down[i],
                            ln_g[i, 1], ln_b[i, 1])
        outs.append(xs)
    return jnp.stack(outs)
```

```python
import functools
import math

import jax
import jax.numpy as jnp
from jax import lax
from jax.experimental import pallas as pl
from jax.experimental.pallas import tpu as pltpu

F32 = jnp.float32
BF16 = jnp.bfloat16

DEPTH = 4
A_CHUNK = 128
A_GROUPS = 8
HEAD_DIM = 64
V_DIM = 2 * HEAD_DIM
REL_BUCKETS = 32
REL_FAR = 128
LN_EPS = 1e-5
LOG2_E = math.log2(math.e)
DEEPNORM_ALPHA = (2 * DEPTH) ** 0.25
REL_CLIP = 2047
F32_MANTISSA_BITS = 23
F32_EXP_BIAS = 127

LANES = 128
SUBLANES = 8
BF16_ROWS = 16
VMEM_LIMIT = 56 * 1024 * 1024

TM_GMLP = 256
TQ = 1024
TK = 512
TM_QKV = TK
TM_PROJ = 512
TM_FFN = 512
FFN_CHUNK = 1024
STAGE_GROUP = 4
MIN_STAGE_GROUP = 2
ONES_ROWS = BF16_ROWS


def _lambda_init(layer_idx):
    return 0.8 - 0.6 * math.exp(-0.3 * layer_idx)


def _gelu(x):
    return 0.5 * x * (1.0 + lax.erf(x * (1.0 / math.sqrt(2.0))))


def _layer_norm(z, g, b):
    mu = jnp.mean(z, axis=-1, keepdims=True)
    zc = z - mu
    var = jnp.mean(zc * zc, axis=-1, keepdims=True)
    return zc * lax.rsqrt(var + LN_EPS) * g + b


def _dot(a, b):
    return jnp.dot(a, b, preferred_element_type=F32)


def _dot_nt(a, b):
    return lax.dot_general(a, b, (((1,), (1,)), ((), ())), preferred_element_type=F32)


def _const_spec(shape):
    nd = len(shape)
    return pl.BlockSpec(shape, lambda *_: (0,) * nd, pipeline_mode=pl.Buffered(1))


def _params(*sem):
    return pltpu.CompilerParams(dimension_semantics=sem, vmem_limit_bytes=VMEM_LIMIT)


def _gmlp_kernel(x_ref, wi_ref, ng_ref, nb_ref, ws_ref, bs_ref, wo_ref, lg_ref, lb_ref, o_ref):
    x = x_ref[...]
    tm = x.shape[0]
    W = wo_ref.shape[0]
    gd = W // A_GROUPS
    hidden = _gelu(_dot(x.astype(BF16), wi_ref[...]))
    v = _layer_norm(hidden[:, W:], ng_ref[...], nb_ref[...]).astype(BF16)
    mixed = []
    for c in range(tm // A_CHUNK):
        row = []
        for g in range(A_GROUPS):
            vc = v[c * A_CHUNK:(c + 1) * A_CHUNK, g * gd:(g + 1) * gd]
            row.append(_dot(ws_ref[g], vc) + bs_ref[g])
        mixed.append(jnp.concatenate(row, axis=1))
    y = (hidden[:, :W] * jnp.concatenate(mixed, axis=0)).astype(BF16)
    z = DEEPNORM_ALPHA * x + _dot(y, wo_ref[...])
    o_ref[...] = _layer_norm(z, lg_ref[...], lb_ref[...])


def _gmlp_layer(x, w_in, norm_g, norm_b, w_s, b_s, w_out, ln_g, ln_b):
    S, D = x.shape
    W = w_out.shape[0]
    tm = TM_GMLP
    return pl.pallas_call(
        _gmlp_kernel,
        grid=(S // tm,),
        in_specs=[
            pl.BlockSpec((tm, D), lambda i: (i, 0)),
            _const_spec((D, 2 * W)),
            _const_spec((1, W)),
            _const_spec((1, W)),
            _const_spec((A_GROUPS, A_CHUNK, A_CHUNK)),
            _const_spec((A_GROUPS, A_CHUNK, 1)),
            _const_spec((W, D)),
            _const_spec((1, D)),
            _const_spec((1, D)),
        ],
        out_specs=pl.BlockSpec((tm, D), lambda i: (i, 0)),
        out_shape=jax.ShapeDtypeStruct((S, D), F32),
        compiler_params=_params("parallel"),
        name="gmlp_layer",
    )(x, w_in.astype(BF16), norm_g.reshape(1, W), norm_b.reshape(1, W), w_s.astype(BF16),
      b_s.reshape(A_GROUPS, A_CHUNK, 1), w_out.astype(BF16), ln_g.reshape(1, D), ln_b.reshape(1, D))


def _pos_stats_kernel(p_ref, mn_ref, mx_ref):
    p = p_ref[...]
    mn_ref[...] = jnp.min(p, axis=1, keepdims=True)
    mx_ref[...] = jnp.max(p, axis=1, keepdims=True)


def _pos_stats(positions):
    nb = positions.shape[0] // LANES
    mn, mx = pl.pallas_call(
        _pos_stats_kernel,
        out_shape=(jax.ShapeDtypeStruct((nb, 1), jnp.int32),) * 2,
        name="pos_stats",
    )(positions.reshape(nb, LANES))
    return mn.reshape(nb), mx.reshape(nb)


def _qkv_kernel(x_ref, wqt_ref, wk_ref, wvt_ref, qt_ref, k_ref, vt_ref):
    xb = x_ref[...].astype(BF16)
    tm = xb.shape[0]
    qt_ref[...] = _dot_nt(wqt_ref[...], xb).astype(BF16)
    k_ref[...] = _dot(xb, wk_ref[...]).astype(BF16)
    vt = _dot_nt(wvt_ref[...], xb).astype(BF16)
    heads = vt.shape[0] // V_DIM
    vt_ref[:, 0, 0:V_DIM, :] = vt.reshape(heads, V_DIM, tm)
    vt_ref[:, 0, V_DIM:, :] = jnp.ones((heads, ONES_ROWS, tm), BF16)


def _qkv(x, w_qkv):
    S, D = x.shape
    tm = TM_QKV
    heads = D // V_DIM
    wqt = (w_qkv[:, :D] * (HEAD_DIM ** -0.5 * LOG2_E)).T.astype(BF16)
    wk = w_qkv[:, D:2 * D].astype(BF16)
    wvt = w_qkv[:, 2 * D:].T.astype(BF16)
    return pl.pallas_call(
        _qkv_kernel,
        grid=(S // tm,),
        in_specs=[
            pl.BlockSpec((tm, D), lambda i: (i, 0)),
            _const_spec((D, D)),
            _const_spec((D, D)),
            _const_spec((D, D)),
        ],
        out_specs=[
            pl.BlockSpec((D, tm), lambda i: (0, i)),
            pl.BlockSpec((tm, D), lambda i: (i, 0)),
            pl.BlockSpec((heads, 1, V_DIM + ONES_ROWS, tm), lambda i: (0, i, 0, 0)),
        ],
        out_shape=[
            jax.ShapeDtypeStruct((D, S), BF16),
            jax.ShapeDtypeStruct((S, D), BF16),
            jax.ShapeDtypeStruct((heads, S // tm, V_DIM + ONES_ROWS, tm), BF16),
        ],
        compiler_params=_params("parallel"),
        name="qkv_proj",
    )(x, wqt, wk, wvt)


def _bias_tile(posk_row, posq_row, table_row):
    tk = posk_row.shape[1]
    tq = posq_row.shape[1]
    table_sq = jnp.broadcast_to(table_row, (LANES, LANES))
    rows = []
    for c in range(tk // LANES):
        pk = posk_row[:, c * LANES:(c + 1) * LANES]
        pk_col = jnp.transpose(jnp.broadcast_to(pk, (LANES, LANES)))
        cols = []
        for d in range(tq // LANES):
            rel = pk_col - posq_row[:, d * LANES:(d + 1) * LANES]
            n = jnp.abs(rel)
            nsq = jnp.square(jnp.minimum(n, REL_CLIP)).astype(F32)
            log2_nsq = (lax.bitcast_convert_type(nsq, jnp.int32) >> F32_MANTISSA_BITS) - F32_EXP_BIAS
            large = jnp.minimum(log2_nsq + 2, REL_BUCKETS // 2 - 1)
            bucket = jnp.where(n < REL_BUCKETS // 4, n, large)
            bucket = bucket + jnp.where(rel > 0, REL_BUCKETS // 2, 0)
            cols.append(jnp.take_along_axis(table_sq, bucket, axis=1))
        rows.append(jnp.concatenate(cols, axis=1))
    return jnp.concatenate(rows, axis=0)


def _attn_kernel(bmin_ref, bmax_ref, tbl_ref,
                 qt_ref, k_ref, vt_ref, posq_ref, posk_ref, tblv_ref, lam_ref, sg_ref,
                 o_ref, qcat, s_a, s_b, smax_a, smax_b, m_scr, acc_scr, *, nk, lambda_init):
    h = pl.program_id(0)
    qi = pl.program_id(1)
    tq = qt_ref.shape[1]
    tk = vt_ref.shape[3]
    heads = pl.num_programs(0)

    q = qt_ref[...]
    row = lax.broadcasted_iota(jnp.int32, q.shape, 0)
    zero = jnp.zeros_like(q)
    qcat[:, 0:tq] = jnp.where(row < HEAD_DIM, q, zero)
    qcat[:, tq:2 * tq] = jnp.where(row >= HEAD_DIM, q, zero)

    qmin = bmin_ref[qi * (tq // LANES)]
    qmax = bmax_ref[qi * (tq // LANES)]
    for r in range(1, tq // LANES):
        qmin = jnp.minimum(qmin, bmin_ref[qi * (tq // LANES) + r])
        qmax = jnp.maximum(qmax, bmax_ref[qi * (tq // LANES) + r])
    bias_before = tbl_ref[(REL_BUCKETS // 2 - 1) * heads + h]
    bias_after = tbl_ref[(REL_BUCKETS - 1) * heads + h]

    acc_scr[...] = jnp.zeros_like(acc_scr)
    m_scr[...] = jnp.full(m_scr.shape, -1e30, F32)

    def classify(kt):
        kmin = bmin_ref[kt * (tk // LANES)]
        kmax = bmax_ref[kt * (tk // LANES)]
        for r in range(1, tk // LANES):
            kmin = jnp.minimum(kmin, bmin_ref[kt * (tk // LANES) + r])
            kmax = jnp.maximum(kmax, bmax_ref[kt * (tk // LANES) + r])
        all_after = kmin - qmax >= REL_FAR
        all_before = kmax - qmin <= -REL_FAR
        near = jnp.logical_not(jnp.logical_or(all_after, all_before))
        const_bias = jnp.where(all_after, bias_after, jnp.where(all_before, bias_before, 0.0))
        return near, const_bias

    def logits(kt, s_ref, smax_ref):
        kk = k_ref[pl.ds(pl.multiple_of(kt * tk, tk), tk), :]
        s = _dot(kk, qcat[...])
        s_ref[...] = s
        smax_ref[...] = jnp.max(s, axis=0, keepdims=True)

    def add_near_bias(kt, s_ref, smax_ref):
        bias = _bias_tile(posk_ref[kt], posq_ref[...], tblv_ref[pl.ds(h, 1), :])
        s = s_ref[...] + jnp.concatenate([bias, bias], axis=1)
        s_ref[...] = s
        smax_ref[...] = jnp.max(s, axis=0, keepdims=True)

    def softmax_update(kt, s_ref, smax_ref, const_bias):
        m_old = m_scr[...]
        m_new = jnp.maximum(m_old, smax_ref[...] + const_bias)
        p = jnp.exp2(s_ref[...] - (m_new - const_bias)).astype(BF16)
        acc_scr[...] = jnp.exp2(m_old - m_new) * acc_scr[...] + _dot(vt_ref[0, kt], p)
        m_scr[...] = m_new

    def stage(kt, cur, nxt, const_bias, fix_next):
        kn = jnp.minimum(kt + 1, nk - 1)
        logits(kn, *nxt)
        softmax_update(kt, *cur, const_bias)
        near_next, bias_next = classify(kn)
        if fix_next:
            @pl.when(near_next)
            def _():
                add_near_bias(kn, *nxt)
        return near_next, bias_next

    buf_a = (s_a, smax_a)
    buf_b = (s_b, smax_b)
    near0, bias0 = classify(0)
    logits(0, *buf_a)

    @pl.when(near0)
    def _():
        add_near_bias(0, *buf_a)

    def pair(kt, const_bias):
        _, const_bias = stage(kt, buf_a, buf_b, const_bias, True)
        _, const_bias = stage(kt + 1, buf_b, buf_a, const_bias, True)
        return const_bias

    def run_group(base, size, const_bias):
        inner_far = jnp.bool_(True)
        for i in range(1, size):
            inner_far = jnp.logical_and(inner_far, jnp.logical_not(classify(base + i)[0]))

        def branch_free():
            c = const_bias
            for i in range(size):
                cur, nxt = (buf_a, buf_b) if i % 2 == 0 else (buf_b, buf_a)
                _, c = stage(base + i, cur, nxt, c, i == size - 1)
            return c

        def split():
            if size == MIN_STAGE_GROUP:
                return lax.fori_loop(0, size // 2, lambda t, c: pair(base + 2 * t, c), const_bias)
            half = size // 2
            return lax.fori_loop(0, 2, lambda t, c: run_group(base + t * half, half, c), const_bias)

        return lax.cond(inner_far, branch_free, split)

    lax.fori_loop(0, nk // STAGE_GROUP, lambda j, c: run_group(j * STAGE_GROUP, STAGE_GROUP, c), bias0)

    lp = lam_ref[...]
    lam = (jnp.exp(jnp.sum(lp[0:1] * lp[1:2], axis=1, keepdims=True))
           - jnp.exp(jnp.sum(lp[2:3] * lp[3:4], axis=1, keepdims=True)) + lambda_init)
    o1 = acc_scr[0:V_DIM, 0:tq] / acc_scr[V_DIM:V_DIM + 1, 0:tq]
    o2 = acc_scr[0:V_DIM, tq:2 * tq] / acc_scr[V_DIM:V_DIM + 1, tq:2 * tq]
    o = o1 - lam * o2
    o = o * lax.rsqrt(jnp.mean(o * o, axis=0, keepdims=True) + LN_EPS)
    o = o * sg_ref[...] * (1.0 - lambda_init)
    o_ref[...] = jnp.transpose(o).astype(BF16)


def _attention(qt, k, vt, positions, bmin, bmax, rel_table, lam_params, subln_g, lambda_init):
    D, S = qt.shape
    heads = D // V_DIM
    tq, tk = TQ, TK
    nk = S // tk
    assert S % tq == 0 and nk % STAGE_GROUP == 0, (S, tq, tk)
    table2 = rel_table.astype(F32) * LOG2_E
    table_rows = jnp.zeros((heads, LANES), F32).at[:, :REL_BUCKETS].set(table2.T)
    kernel = functools.partial(_attn_kernel, nk=nk, lambda_init=lambda_init)
    grid_spec = pltpu.PrefetchScalarGridSpec(
        num_scalar_prefetch=3,
        grid=(heads, S // tq),
        in_specs=[
            pl.BlockSpec((V_DIM, tq), lambda h, i, *_: (h, i)),
            pl.BlockSpec((S, V_DIM), lambda h, i, *_: (0, h)),
            pl.BlockSpec((1, nk, V_DIM + ONES_ROWS, tk), lambda h, i, *_: (h, 0, 0, 0)),
            pl.BlockSpec((1, tq), lambda h, i, *_: (0, i)),
            pl.BlockSpec((nk, 1, tk), lambda h, i, *_: (0, 0, 0)),
            pl.BlockSpec((heads, LANES), lambda h, i, *_: (0, 0)),
            pl.BlockSpec((4, HEAD_DIM), lambda h, i, *_: (0, 0)),
            pl.BlockSpec((V_DIM, 1), lambda h, i, *_: (0, 0)),
        ],
        out_specs=pl.BlockSpec((tq, V_DIM), lambda h, i, *_: (i, h)),
        scratch_shapes=[
            pltpu.VMEM((V_DIM, 2 * tq), BF16),
            pltpu.VMEM((tk, 2 * tq), F32),
            pltpu.VMEM((tk, 2 * tq), F32),
            pltpu.VMEM((1, 2 * tq), F32),
            pltpu.VMEM((1, 2 * tq), F32),
            pltpu.VMEM((1, 2 * tq), F32),
            pltpu.VMEM((V_DIM + ONES_ROWS, 2 * tq), F32),
        ],
    )
    return pl.pallas_call(
        kernel,
        grid_spec=grid_spec,
        out_shape=jax.ShapeDtypeStruct((S, D), BF16),
        compiler_params=_params("parallel", "parallel"),
        name="diff_attention",
    )(bmin, bmax, table2.reshape(-1),
      qt, k, vt, positions.reshape(1, S), positions.reshape(nk, 1, tk), table_rows,
      lam_params, subln_g.reshape(V_DIM, 1))


def _proj_ln_kernel(a_ref, x_ref, w_ref, lg_ref, lb_ref, o_ref):
    z = DEEPNORM_ALPHA * x_ref[...] + _dot(a_ref[...], w_ref[...])
    o_ref[...] = _layer_norm(z, lg_ref[...], lb_ref[...])


def _proj_ln(a, x, w, ln_g, ln_b):
    S, D = x.shape
    K = a.shape[1]
    tm = TM_PROJ
    return pl.pallas_call(
        _proj_ln_kernel,
        grid=(S // tm,),
        in_specs=[
            pl.BlockSpec((tm, K), lambda i: (i, 0)),
            pl.BlockSpec((tm, D), lambda i: (i, 0)),
            _const_spec((K, D)),
            _const_spec((1, D)),
            _const_spec((1, D)),
        ],
        out_specs=pl.BlockSpec((tm, D), lambda i: (i, 0)),
        out_shape=jax.ShapeDtypeStruct((S, D), F32),
        compiler_params=_params("parallel"),
        name="attn_out_proj",
    )(a, x, w.astype(BF16), ln_g.reshape(1, D), ln_b.reshape(1, D))


def _attn_layer(x, positions, bmin, bmax, rel_table, w_qkv, lam_params, subln_g, w_out,
                lambda_init, ln_g, ln_b):
    qt, k, vt = _qkv(x, w_qkv)
    o = _attention(qt, k, vt, positions, bmin, bmax, rel_table, lam_params, subln_g, lambda_init)
    return _proj_ln(o, x, w_out, ln_g, ln_b)


def _ffn_kernel(x_ref, xp_ref, xn_ref, wi_ref, cw_ref, cb_ref, wd_ref, lg_ref, lb_ref,
                o_ref, g_a, g_b, *, chunks):
    i = pl.program_id(0)
    x = x_ref[...]
    tm = x.shape[0]
    halo = xp_ref.shape[0]
    F = wd_ref.shape[0]
    xp = jnp.where(i > 0, xp_ref[...], 0.0)
    xn = jnp.where(i < pl.num_programs(0) - 1, xn_ref[...], 0.0)
    xe = jnp.concatenate([xp, x, xn], axis=0).astype(BF16)
    xb = x.astype(BF16)
    acc = DEEPNORM_ALPHA * x
    start = 0
    for idx, width in enumerate(chunks):
        cols = slice(start, start + width)
        g_scr = g_b if idx % 2 else g_a
        g_scr[:, 0:width] = _dot(xe, wi_ref[:, cols])
        cw = cw_ref[:, cols]
        gate = (g_scr[halo - 1:halo - 1 + tm, 0:width] * cw[0:1]
                + g_scr[halo:halo + tm, 0:width] * cw[1:2]
                + g_scr[halo + 1:halo + 1 + tm, 0:width] * cw[2:3]
                + cb_ref[:, cols])
        up = _dot(xb, wi_ref[:, F + start:F + start + width])
        hidden = (_gelu(gate) * up).astype(BF16)
        acc = acc + _dot(hidden, wd_ref[cols, :])
        start += width
    o_ref[...] = _layer_norm(acc, lg_ref[...], lb_ref[...])


def _ffn_layer(x, w_in, conv_w, conv_b, w_down, ln_g, ln_b):
    S, D = x.shape
    F = w_down.shape[0]
    tm, halo = TM_FFN, SUBLANES
    chunks = (FFN_CHUNK,) * (F // FFN_CHUNK) + ((F % FFN_CHUNK,) if F % FFN_CHUNK else ())
    blocks_per_tile = tm // halo
    last_halo_block = S // halo - 1
    return pl.pallas_call(
        functools.partial(_ffn_kernel, chunks=chunks),
        grid=(S // tm,),
        in_specs=[
            pl.BlockSpec((tm, D), lambda i: (i, 0)),
            pl.BlockSpec((halo, D), lambda i: (jnp.maximum(i * blocks_per_tile - 1, 0), 0)),
            pl.BlockSpec((halo, D),
                         lambda i: (jnp.minimum((i + 1) * blocks_per_tile, last_halo_block), 0)),
            _const_spec((D, 2 * F)),
            _const_spec((conv_w.shape[0], F)),
            _const_spec((1, F)),
            _const_spec((F, D)),
            _const_spec((1, D)),
            _const_spec((1, D)),
        ],
        out_specs=pl.BlockSpec((tm, D), lambda i: (i, 0)),
        out_shape=jax.ShapeDtypeStruct((S, D), F32),
        scratch_shapes=[pltpu.VMEM((tm + 2 * halo, max(chunks)), F32)] * 2,
        compiler_params=_params("parallel"),
        name="conv_glu_ffn",
    )(x, x, x, w_in.astype(BF16), conv_w, conv_b.reshape(1, F), w_down.astype(BF16),
      ln_g.reshape(1, D), ln_b.reshape(1, D))


def kernel(x, positions, rel_bias_table, a_w_in, a_norm_g, a_norm_b, a_w_s, a_b_s, a_w_out,
           b_w_qkv, b_lambda, b_subln_g, b_w_out, f_w_in, f_conv_w, f_conv_b, f_w_down,
           ln_g, ln_b):
    B, S, D = x.shape
    outs = []
    for b in range(B):
        xs = x[b]
        pos = positions[b]
        bmin, bmax = _pos_stats(pos)
        for i in range(DEPTH):
            j = i // 2
            if i % 2 == 0:
                xs = _gmlp_layer(xs, a_w_in[j], a_norm_g[j], a_norm_b[j], a_w_s[j], a_b_s[j],
                                 a_w_out[j], ln_g[i, 0], ln_b[i, 0])
            else:
                xs = _attn_layer(xs, pos, bmin, bmax, rel_bias_table, b_w_qkv[j], b_lambda[j],
                                 b_subln_g[j], b_w_out[j], _lambda_init(i), ln_g[i, 0], ln_b[i, 0])
            xs = _ffn_layer(xs, f_w_in[i], f_conv_w[i], f_conv_b[i], f_w_down[i],
                            ln_g[i, 1], ln_b[i, 1])
        outs.append(xs)
    return jnp.stack(outs)
```

```python
import functools
import math

import jax
import jax.numpy as jnp
from jax import lax
from jax.experimental import pallas as pl
from jax.experimental.pallas import tpu as pltpu

F32 = jnp.float32
BF16 = jnp.bfloat16

DEPTH = 4
A_CHUNK = 128
A_GROUPS = 8
HEAD_DIM = 64
V_DIM = 2 * HEAD_DIM
REL_BUCKETS = 32
REL_FAR = 128
LN_EPS = 1e-5
LOG2_E = math.log2(math.e)
DEEPNORM_ALPHA = (2 * DEPTH) ** 0.25
REL_CLIP = 2047
F32_MANTISSA_BITS = 23
F32_EXP_BIAS = 127

LANES = 128
SUBLANES = 8
BF16_ROWS = 16
VMEM_LIMIT = 56 * 1024 * 1024

TM_GMLP = 512
TQ = 512
TK = 512
TM_QKV = TK
TM_PROJ = 512
TM_FFN = 512
STAGE_GROUP = 4
MIN_STAGE_GROUP = 2
ONES_ROWS = BF16_ROWS


def _lambda_init(layer_idx):
    return 0.8 - 0.6 * math.exp(-0.3 * layer_idx)


def _gelu(x):
    return 0.5 * x * (1.0 + lax.erf(x * (1.0 / math.sqrt(2.0))))


def _layer_norm(z, g, b):
    mu = jnp.mean(z, axis=-1, keepdims=True)
    zc = z - mu
    var = jnp.mean(zc * zc, axis=-1, keepdims=True)
    return zc * lax.rsqrt(var + LN_EPS) * g + b


def _dot(a, b):
    return jnp.dot(a, b, preferred_element_type=F32)


def _dot_nt(a, b):
    return lax.dot_general(a, b, (((1,), (1,)), ((), ())), preferred_element_type=F32)


def _const_spec(shape):
    nd = len(shape)
    return pl.BlockSpec(shape, lambda *_: (0,) * nd, pipeline_mode=pl.Buffered(1))


def _params(*sem):
    return pltpu.CompilerParams(dimension_semantics=sem, vmem_limit_bytes=VMEM_LIMIT)


def _gmlp_kernel(x_ref, wi_ref, ng_ref, nb_ref, ws_ref, bs_ref, wo_ref, lg_ref, lb_ref, o_ref):
    x = x_ref[...]
    tm = x.shape[0]
    W = wo_ref.shape[0]
    gd = W // A_GROUPS
    hidden = _gelu(_dot(x.astype(BF16), wi_ref[...]))
    v = _layer_norm(hidden[:, W:], ng_ref[...], nb_ref[...]).astype(BF16)
    mixed = []
    for c in range(tm // A_CHUNK):
        row = []
        for g in range(A_GROUPS):
            vc = v[c * A_CHUNK:(c + 1) * A_CHUNK, g * gd:(g + 1) * gd]
            row.append(_dot(ws_ref[g], vc) + bs_ref[g])
        mixed.append(jnp.concatenate(row, axis=1))
    y = (hidden[:, :W] * jnp.concatenate(mixed, axis=0)).astype(BF16)
    z = DEEPNORM_ALPHA * x + _dot(y, wo_ref[...])
    o_ref[...] = _layer_norm(z, lg_ref[...], lb_ref[...])


def _gmlp_layer(x, w_in, norm_g, norm_b, w_s, b_s, w_out, ln_g, ln_b):
    S, D = x.shape
    W = w_out.shape[0]
    tm = TM_GMLP
    return pl.pallas_call(
        _gmlp_kernel,
        grid=(S // tm,),
        in_specs=[
            pl.BlockSpec((tm, D), lambda i: (i, 0)),
            _const_spec((D, 2 * W)),
            _const_spec((1, W)),
            _const_spec((1, W)),
            _const_spec((A_GROUPS, A_CHUNK, A_CHUNK)),
            _const_spec((A_GROUPS, A_CHUNK, 1)),
            _const_spec((W, D)),
            _const_spec((1, D)),
            _const_spec((1, D)),
        ],
        out_specs=pl.BlockSpec((tm, D), lambda i: (i, 0)),
        out_shape=jax.ShapeDtypeStruct((S, D), F32),
        compiler_params=_params("parallel"),
        name="gmlp_layer",
    )(x, w_in.astype(BF16), norm_g.reshape(1, W), norm_b.reshape(1, W), w_s.astype(BF16),
      b_s.reshape(A_GROUPS, A_CHUNK, 1), w_out.astype(BF16), ln_g.reshape(1, D), ln_b.reshape(1, D))


def _pos_stats_kernel(p_ref, mn_ref, mx_ref):
    p = p_ref[...]
    mn_ref[...] = jnp.min(p, axis=1, keepdims=True)
    mx_ref[...] = jnp.max(p, axis=1, keepdims=True)


def _pos_stats(positions):
    nb = positions.shape[0] // LANES
    mn, mx = pl.pallas_call(
        _pos_stats_kernel,
        out_shape=(jax.ShapeDtypeStruct((nb, 1), jnp.int32),) * 2,
        name="pos_stats",
    )(positions.reshape(nb, LANES))
    return mn.reshape(nb), mx.reshape(nb)


def _qkv_kernel(x_ref, wqt_ref, wk_ref, wvt_ref, qt_ref, k_ref, vt_ref):
    xb = x_ref[...].astype(BF16)
    tm = xb.shape[0]
    qt_ref[...] = _dot_nt(wqt_ref[...], xb).astype(BF16)
    k_ref[...] = _dot(xb, wk_ref[...]).astype(BF16)
    vt = _dot_nt(wvt_ref[...], xb).astype(BF16)
    heads = vt.shape[0] // V_DIM
    vt_ref[:, 0, 0:V_DIM, :] = vt.reshape(heads, V_DIM, tm)
    vt_ref[:, 0, V_DIM:, :] = jnp.ones((heads, ONES_ROWS, tm), BF16)


def _qkv(x, w_qkv):
    S, D = x.shape
    tm = TM_QKV
    heads = D // V_DIM
    wqt = (w_qkv[:, :D] * (HEAD_DIM ** -0.5 * LOG2_E)).T.astype(BF16)
    wk = w_qkv[:, D:2 * D].astype(BF16)
    wvt = w_qkv[:, 2 * D:].T.astype(BF16)
    return pl.pallas_call(
        _qkv_kernel,
        grid=(S // tm,),
        in_specs=[
            pl.BlockSpec((tm, D), lambda i: (i, 0)),
            _const_spec((D, D)),
            _const_spec((D, D)),
            _const_spec((D, D)),
        ],
        out_specs=[
            pl.BlockSpec((D, tm), lambda i: (0, i)),
            pl.BlockSpec((tm, D), lambda i: (i, 0)),
            pl.BlockSpec((heads, 1, V_DIM + ONES_ROWS, tm), lambda i: (0, i, 0, 0)),
        ],
        out_shape=[
            jax.ShapeDtypeStruct((D, S), BF16),
            jax.ShapeDtypeStruct((S, D), BF16),
            jax.ShapeDtypeStruct((heads, S // tm, V_DIM + ONES_ROWS, tm), BF16),
        ],
        compiler_params=_params("parallel"),
        name="qkv_proj",
    )(x, wqt, wk, wvt)


def _bias_tile(posk_row, posq_row, table_row):
    tk = posk_row.shape[1]
    tq = posq_row.shape[1]
    table_sq = jnp.broadcast_to(table_row, (LANES, LANES))
    rows = []
    for c in range(tk // LANES):
        pk = posk_row[:, c * LANES:(c + 1) * LANES]
        pk_col = jnp.transpose(jnp.broadcast_to(pk, (LANES, LANES)))
        cols = []
        for d in range(tq // LANES):
            rel = pk_col - posq_row[:, d * LANES:(d + 1) * LANES]
            n = jnp.abs(rel)
            nsq = jnp.square(jnp.minimum(n, REL_CLIP)).astype(F32)
            log2_nsq = (lax.bitcast_convert_type(nsq, jnp.int32) >> F32_MANTISSA_BITS) - F32_EXP_BIAS
            large = jnp.minimum(log2_nsq + 2, REL_BUCKETS // 2 - 1)
            bucket = jnp.where(n < REL_BUCKETS // 4, n, large)
            bucket = bucket + jnp.where(rel > 0, REL_BUCKETS // 2, 0)
            cols.append(jnp.take_along_axis(table_sq, bucket, axis=1))
        rows.append(jnp.concatenate(cols, axis=1))
    return jnp.concatenate(rows, axis=0)


def _attn_kernel(bmin_ref, bmax_ref, tbl_ref,
                 qt_ref, k_ref, vt_ref, posq_ref, posk_ref, tblv_ref, lam_ref, sg_ref,
                 o_ref, qcat, s_a, s_b, smax_a, smax_b, m_scr, acc_scr, *, nk, lambda_init):
    h = pl.program_id(0)
    qi = pl.program_id(1)
    tq = qt_ref.shape[1]
    tk = vt_ref.shape[3]
    heads = pl.num_programs(0)

    q = qt_ref[...]
    row = lax.broadcasted_iota(jnp.int32, q.shape, 0)
    zero = jnp.zeros_like(q)
    qcat[:, 0:tq] = jnp.where(row < HEAD_DIM, q, zero)
    qcat[:, tq:2 * tq] = jnp.where(row >= HEAD_DIM, q, zero)

    qmin = bmin_ref[qi * (tq // LANES)]
    qmax = bmax_ref[qi * (tq // LANES)]
    for r in range(1, tq // LANES):
        qmin = jnp.minimum(qmin, bmin_ref[qi * (tq // LANES) + r])
        qmax = jnp.maximum(qmax, bmax_ref[qi * (tq // LANES) + r])
    bias_before = tbl_ref[(REL_BUCKETS // 2 - 1) * heads + h]
    bias_after = tbl_ref[(REL_BUCKETS - 1) * heads + h]

    acc_scr[...] = jnp.zeros_like(acc_scr)
    m_scr[...] = jnp.full(m_scr.shape, -1e30, F32)

    def classify(kt):
        kmin = bmin_ref[kt * (tk // LANES)]
        kmax = bmax_ref[kt * (tk // LANES)]
        for r in range(1, tk // LANES):
            kmin = jnp.minimum(kmin, bmin_ref[kt * (tk // LANES) + r])
            kmax = jnp.maximum(kmax, bmax_ref[kt * (tk // LANES) + r])
        all_after = kmin - qmax >= REL_FAR
        all_before = kmax - qmin <= -REL_FAR
        near = jnp.logical_not(jnp.logical_or(all_after, all_before))
        const_bias = jnp.where(all_after, bias_after, jnp.where(all_before, bias_before, 0.0))
        return near, const_bias

    def logits(kt, s_ref, smax_ref):
        kk = k_ref[pl.ds(pl.multiple_of(kt * tk, tk), tk), :]
        s = _dot(kk, qcat[...])
        s_ref[...] = s
        smax_ref[...] = jnp.max(s, axis=0, keepdims=True)

    def add_near_bias(kt, s_ref, smax_ref):
        bias = _bias_tile(posk_ref[kt], posq_ref[...], tblv_ref[pl.ds(h, 1), :])
        s = s_ref[...] + jnp.concatenate([bias, bias], axis=1)
        s_ref[...] = s
        smax_ref[...] = jnp.max(s, axis=0, keepdims=True)

    def softmax_update(kt, s_ref, smax_ref, const_bias):
        m_old = m_scr[...]
        m_new = jnp.maximum(m_old, smax_ref[...] + const_bias)
        p = jnp.exp2(s_ref[...] - (m_new - const_bias)).astype(BF16)
        acc_scr[...] = jnp.exp2(m_old - m_new) * acc_scr[...] + _dot(vt_ref[0, kt], p)
        m_scr[...] = m_new

    def stage(kt, cur, nxt, const_bias, fix_next):
        kn = jnp.minimum(kt + 1, nk - 1)
        logits(kn, *nxt)
        softmax_update(kt, *cur, const_bias)
        near_next, bias_next = classify(kn)
        if fix_next:
            @pl.when(near_next)
            def _():
                add_near_bias(kn, *nxt)
        return near_next, bias_next

    buf_a = (s_a, smax_a)
    buf_b = (s_b, smax_b)
    near0, bias0 = classify(0)
    logits(0, *buf_a)

    @pl.when(near0)
    def _():
        add_near_bias(0, *buf_a)

    def pair(kt, const_bias):
        _, const_bias = stage(kt, buf_a, buf_b, const_bias, True)
        _, const_bias = stage(kt + 1, buf_b, buf_a, const_bias, True)
        return const_bias

    def run_group(base, size, const_bias):
        inner_far = jnp.bool_(True)
        for i in range(1, size):
            inner_far = jnp.logical_and(inner_far, jnp.logical_not(classify(base + i)[0]))

        def branch_free():
            c = const_bias
            for i in range(size):
                cur, nxt = (buf_a, buf_b) if i % 2 == 0 else (buf_b, buf_a)
                _, c = stage(base + i, cur, nxt, c, i == size - 1)
            return c

        def split():
            if size == MIN_STAGE_GROUP:
                return lax.fori_loop(0, size // 2, lambda t, c: pair(base + 2 * t, c), const_bias)
            half = size // 2
            return lax.fori_loop(0, 2, lambda t, c: run_group(base + t * half, half, c), const_bias)

        return lax.cond(inner_far, branch_free, split)

    lax.fori_loop(0, nk // STAGE_GROUP, lambda j, c: run_group(j * STAGE_GROUP, STAGE_GROUP, c), bias0)

    lp = lam_ref[...]
    lam = (jnp.exp(jnp.sum(lp[0:1] * lp[1:2], axis=1, keepdims=True))
           - jnp.exp(jnp.sum(lp[2:3] * lp[3:4], axis=1, keepdims=True)) + lambda_init)
    o1 = acc_scr[0:V_DIM, 0:tq] / acc_scr[V_DIM:V_DIM + 1, 0:tq]
    o2 = acc_scr[0:V_DIM, tq:2 * tq] / acc_scr[V_DIM:V_DIM + 1, tq:2 * tq]
    o = o1 - lam * o2
    o = o * lax.rsqrt(jnp.mean(o * o, axis=0, keepdims=True) + LN_EPS)
    o = o * sg_ref[...] * (1.0 - lambda_init)
    o_ref[...] = jnp.transpose(o).astype(BF16)


def _attention(qt, k, vt, positions, bmin, bmax, rel_table, lam_params, subln_g, lambda_init):
    D, S = qt.shape
    heads = D // V_DIM
    tq, tk = TQ, TK
    nk = S // tk
    assert S % tq == 0 and nk % STAGE_GROUP == 0, (S, tq, tk)
    table2 = rel_table.astype(F32) * LOG2_E
    table_rows = jnp.zeros((heads, LANES), F32).at[:, :REL_BUCKETS].set(table2.T)
    kernel = functools.partial(_attn_kernel, nk=nk, lambda_init=lambda_init)
    grid_spec = pltpu.PrefetchScalarGridSpec(
        num_scalar_prefetch=3,
        grid=(heads, S // tq),
        in_specs=[
            pl.BlockSpec((V_DIM, tq), lambda h, i, *_: (h, i)),
            pl.BlockSpec((S, V_DIM), lambda h, i, *_: (0, h)),
            pl.BlockSpec((1, nk, V_DIM + ONES_ROWS, tk), lambda h, i, *_: (h, 0, 0, 0)),
            pl.BlockSpec((1, tq), lambda h, i, *_: (0, i)),
            pl.BlockSpec((nk, 1, tk), lambda h, i, *_: (0, 0, 0)),
            pl.BlockSpec((heads, LANES), lambda h, i, *_: (0, 0)),
            pl.BlockSpec((4, HEAD_DIM), lambda h, i, *_: (0, 0)),
            pl.BlockSpec((V_DIM, 1), lambda h, i, *_: (0, 0)),
        ],
        out_specs=pl.BlockSpec((tq, V_DIM), lambda h, i, *_: (i, h)),
        scratch_shapes=[
            pltpu.VMEM((V_DIM, 2 * tq), BF16),
            pltpu.VMEM((tk, 2 * tq), F32),
            pltpu.VMEM((tk, 2 * tq), F32),
            pltpu.VMEM((1, 2 * tq), F32),
            pltpu.VMEM((1, 2 * tq), F32),
            pltpu.VMEM((1, 2 * tq), F32),
            pltpu.VMEM((V_DIM + ONES_ROWS, 2 * tq), F32),
        ],
    )
    return pl.pallas_call(
        kernel,
        grid_spec=grid_spec,
        out_shape=jax.ShapeDtypeStruct((S, D), BF16),
        compiler_params=_params("parallel", "parallel"),
        name="diff_attention",
    )(bmin, bmax, table2.reshape(-1),
      qt, k, vt, positions.reshape(1, S), positions.reshape(nk, 1, tk), table_rows,
      lam_params, subln_g.reshape(V_DIM, 1))


def _proj_ln_kernel(a_ref, x_ref, w_ref, lg_ref, lb_ref, o_ref):
    z = DEEPNORM_ALPHA * x_ref[...] + _dot(a_ref[...], w_ref[...])
    o_ref[...] = _layer_norm(z, lg_ref[...], lb_ref[...])


def _proj_ln(a, x, w, ln_g, ln_b):
    S, D = x.shape
    K = a.shape[1]
    tm = TM_PROJ
    return pl.pallas_call(
        _proj_ln_kernel,
        grid=(S // tm,),
        in_specs=[
            pl.BlockSpec((tm, K), lambda i: (i, 0)),
            pl.BlockSpec((tm, D), lambda i: (i, 0)),
            _const_spec((K, D)),
            _const_spec((1, D)),
            _const_spec((1, D)),
        ],
        out_specs=pl.BlockSpec((tm, D), lambda i: (i, 0)),
        out_shape=jax.ShapeDtypeStruct((S, D), F32),
        compiler_params=_params("parallel"),
        name="attn_out_proj",
    )(a, x, w.astype(BF16), ln_g.reshape(1, D), ln_b.reshape(1, D))


def _attn_layer(x, positions, bmin, bmax, rel_table, w_qkv, lam_params, subln_g, w_out,
                lambda_init, ln_g, ln_b):
    qt, k, vt = _qkv(x, w_qkv)
    o = _attention(qt, k, vt, positions, bmin, bmax, rel_table, lam_params, subln_g, lambda_init)
    return _proj_ln(o, x, w_out, ln_g, ln_b)


def _ffn_kernel(x_ref, xp_ref, xn_ref, wi_ref, cw_ref, cb_ref, wd_ref, lg_ref, lb_ref,
                o_ref, g_scr):
    i = pl.program_id(0)
    x = x_ref[...]
    tm = x.shape[0]
    halo = xp_ref.shape[0]
    F = wd_ref.shape[0]
    xp = jnp.where(i > 0, xp_ref[...], 0.0)
    xn = jnp.where(i < pl.num_programs(0) - 1, xn_ref[...], 0.0)
    xe = jnp.concatenate([xp, x, xn], axis=0).astype(BF16)
    g_scr[...] = _dot(xe, wi_ref[:, 0:F])
    cw = cw_ref[...]
    gate = (g_scr[halo - 1:halo - 1 + tm, :] * cw[0:1]
            + g_scr[halo:halo + tm, :] * cw[1:2]
            + g_scr[halo + 1:halo + 1 + tm, :] * cw[2:3]
            + cb_ref[...])
    up = _dot(x.astype(BF16), wi_ref[:, F:2 * F])
    hidden = (_gelu(gate) * up).astype(BF16)
    z = DEEPNORM_ALPHA * x + _dot(hidden, wd_ref[...])
    o_ref[...] = _layer_norm(z, lg_ref[...], lb_ref[...])


def _ffn_layer(x, w_in, conv_w, conv_b, w_down, ln_g, ln_b):
    S, D = x.shape
    F = w_down.shape[0]
    tm, halo = TM_FFN, SUBLANES
    blocks_per_tile = tm // halo
    last_halo_block = S // halo - 1
    return pl.pallas_call(
        _ffn_kernel,
        grid=(S // tm,),
        in_specs=[
            pl.BlockSpec((tm, D), lambda i: (i, 0)),
            pl.BlockSpec((halo, D), lambda i: (jnp.maximum(i * blocks_per_tile - 1, 0), 0)),
            pl.BlockSpec((halo, D),
                         lambda i: (jnp.minimum((i + 1) * blocks_per_tile, last_halo_block), 0)),
            _const_spec((D, 2 * F)),
            _const_spec((conv_w.shape[0], F)),
            _const_spec((1, F)),
            _const_spec((F, D)),
            _const_spec((1, D)),
            _const_spec((1, D)),
        ],
        out_specs=pl.BlockSpec((tm, D), lambda i: (i, 0)),
        out_shape=jax.ShapeDtypeStruct((S, D), F32),
        scratch_shapes=[pltpu.VMEM((tm + 2 * halo, F), F32)],
        compiler_params=_params("parallel"),
        name="conv_glu_ffn",
    )(x, x, x, w_in.astype(BF16), conv_w, conv_b.reshape(1, F), w_down.astype(BF16),
      ln_g.reshape(1, D), ln_b.reshape(1, D))


def kernel(x, positions, rel_bias_table, a_w_in, a_norm_g, a_norm_b, a_w_s, a_b_s, a_w_out,
           b_w_qkv, b_lambda, b_subln_g, b_w_out, f_w_in, f_conv_w, f_conv_b, f_w_down,
           ln_g, ln_b):
    B, S, D = x.shape
    outs = []
    for b in range(B):
        xs = x[b]
        pos = positions[b]
        bmin, bmax = _pos_stats(pos)
        for i in range(DEPTH):
            j = i // 2
            if i % 2 == 0:
                xs = _gmlp_layer(xs, a_w_in[j], a_norm_g[j], a_norm_b[j], a_w_s[j], a_b_s[j],
                                 a_w_out[j], ln_g[i, 0], ln_b[i, 0])
            else:
                xs = _attn_layer(xs, pos, bmin, bmax, rel_bias_table, b_w_qkv[j], b_lambda[j],
                                 b_subln_g[j], b_w_out[j], _lambda_init(i), ln_g[i, 0], ln_b[i, 0])
            xs = _ffn_layer(xs, f_w_in[i], f_conv_w[i], f_conv_b[i], f_w_down[i],
                            ln_g[i, 1], ln_b[i, 1])
        outs.append(xs)
    return jnp.stack(outs)
```

```python
import functools
import math

import jax
import jax.numpy as jnp
from jax import lax
from jax.experimental import pallas as pl
from jax.experimental.pallas import tpu as pltpu

F32 = jnp.float32
BF16 = jnp.bfloat16

DEPTH = 4
A_CHUNK = 128
A_GROUPS = 8
HEAD_DIM = 64
V_DIM = 2 * HEAD_DIM
REL_BUCKETS = 32
REL_FAR = 128
LN_EPS = 1e-5
LOG2_E = math.log2(math.e)
DEEPNORM_ALPHA = (2 * DEPTH) ** 0.25
REL_CLIP = 2047
F32_MANTISSA_BITS = 23
F32_EXP_BIAS = 127

LANES = 128
SUBLANES = 8
BF16_ROWS = 16
VMEM_LIMIT = 56 * 1024 * 1024

TM_GMLP = 512
TQ = 512
TK = 512
TM_QKV = TK
TM_PROJ = 512
TM_FFN = 512
STAGE_GROUP = 4
MIN_STAGE_GROUP = 2
ONES_ROWS = BF16_ROWS


def _lambda_init(layer_idx):
    return 0.8 - 0.6 * math.exp(-0.3 * layer_idx)


def _gelu(x):
    return 0.5 * x * (1.0 + lax.erf(x * (1.0 / math.sqrt(2.0))))


def _layer_norm(z, g, b):
    mu = jnp.mean(z, axis=-1, keepdims=True)
    zc = z - mu
    var = jnp.mean(zc * zc, axis=-1, keepdims=True)
    return zc * lax.rsqrt(var + LN_EPS) * g + b


def _dot(a, b):
    return jnp.dot(a, b, preferred_element_type=F32)


def _dot_nt(a, b):
    return lax.dot_general(a, b, (((1,), (1,)), ((), ())), preferred_element_type=F32)


def _const_spec(shape):
    nd = len(shape)
    return pl.BlockSpec(shape, lambda *_: (0,) * nd, pipeline_mode=pl.Buffered(1))


def _params(*sem):
    return pltpu.CompilerParams(dimension_semantics=sem, vmem_limit_bytes=VMEM_LIMIT)


def _gmlp_kernel(x_ref, wi_ref, ng_ref, nb_ref, ws_ref, bs_ref, wo_ref, lg_ref, lb_ref, o_ref):
    x = x_ref[...]
    tm = x.shape[0]
    W = wo_ref.shape[0]
    gd = W // A_GROUPS
    hidden = _gelu(_dot(x.astype(BF16), wi_ref[...]))
    v = _layer_norm(hidden[:, W:], ng_ref[...], nb_ref[...]).astype(BF16)
    mixed = []
    for c in range(tm // A_CHUNK):
        row = []
        for g in range(A_GROUPS):
            vc = v[c * A_CHUNK:(c + 1) * A_CHUNK, g * gd:(g + 1) * gd]
            row.append(_dot(ws_ref[g], vc) + bs_ref[g])
        mixed.append(jnp.concatenate(row, axis=1))
    y = (hidden[:, :W] * jnp.concatenate(mixed, axis=0)).astype(BF16)
    z = DEEPNORM_ALPHA * x + _dot(y, wo_ref[...])
    o_ref[...] = _layer_norm(z, lg_ref[...], lb_ref[...])


def _gmlp_layer(x, w_in, norm_g, norm_b, w_s, b_s, w_out, ln_g, ln_b):
    S, D = x.shape
    W = w_out.shape[0]
    tm = TM_GMLP
    return pl.pallas_call(
        _gmlp_kernel,
        grid=(S // tm,),
        in_specs=[
            pl.BlockSpec((tm, D), lambda i: (i, 0)),
            _const_spec((D, 2 * W)),
            _const_spec((1, W)),
            _const_spec((1, W)),
            _const_spec((A_GROUPS, A_CHUNK, A_CHUNK)),
            _const_spec((A_GROUPS, A_CHUNK, 1)),
            _const_spec((W, D)),
            _const_spec((1, D)),
            _const_spec((1, D)),
        ],
        out_specs=pl.BlockSpec((tm, D), lambda i: (i, 0)),
        out_shape=jax.ShapeDtypeStruct((S, D), F32),
        compiler_params=_params("parallel"),
        name="gmlp_layer",
    )(x, w_in.astype(BF16), norm_g.reshape(1, W), norm_b.reshape(1, W), w_s.astype(BF16),
      b_s.reshape(A_GROUPS, A_CHUNK, 1), w_out.astype(BF16), ln_g.reshape(1, D), ln_b.reshape(1, D))


def _pos_stats_kernel(p_ref, mn_ref, mx_ref):
    p = p_ref[...]
    mn_ref[...] = jnp.min(p, axis=1, keepdims=True)
    mx_ref[...] = jnp.max(p, axis=1, keepdims=True)


def _pos_stats(positions):
    nb = positions.shape[0] // LANES
    mn, mx = pl.pallas_call(
        _pos_stats_kernel,
        out_shape=(jax.ShapeDtypeStruct((nb, 1), jnp.int32),) * 2,
        name="pos_stats",
    )(positions.reshape(nb, LANES))
    return mn.reshape(nb), mx.reshape(nb)


def _tile_order_kernel(bmin_ref, bmax_ref, order_ref, *, nq, nk, q_blocks, k_blocks):
    n_groups = nk // STAGE_GROUP

    def per_query_tile(qi, carry):
        qmin = bmin_ref[qi * q_blocks]
        qmax = bmax_ref[qi * q_blocks]
        for r in range(1, q_blocks):
            qmin = jnp.minimum(qmin, bmin_ref[qi * q_blocks + r])
            qmax = jnp.maximum(qmax, bmax_ref[qi * q_blocks + r])

        def near(kt):
            kmin = bmin_ref[kt * k_blocks]
            kmax = bmax_ref[kt * k_blocks]
            for r in range(1, k_blocks):
                kmin = jnp.minimum(kmin, bmin_ref[kt * k_blocks + r])
                kmax = jnp.maximum(kmax, bmax_ref[kt * k_blocks + r])
            far = jnp.logical_or(kmin - qmax >= REL_FAR, kmax - qmin <= -REL_FAR)
            return jnp.logical_not(far).astype(jnp.int32)

        n_near = lax.fori_loop(0, nk, lambda kt, c: c + near(kt), 0)
        reorder = n_near <= n_groups

        def place(kt, state):
            near_seen, far_slot = state
            is_near = near(kt)
            slot = jnp.where(is_near == 1, near_seen * STAGE_GROUP, far_slot)
            order_ref[qi * nk + jnp.where(reorder, slot, kt)] = kt
            nxt = far_slot + 1
            held = jnp.logical_and(nxt % STAGE_GROUP == 0, nxt // STAGE_GROUP < n_near)
            nxt = jnp.where(held, nxt + 1, nxt)
            return near_seen + is_near, jnp.where(is_near == 1, far_slot, nxt)

        lax.fori_loop(0, nk, place, (0, jnp.where(n_near > 0, 1, 0)))
        return carry

    lax.fori_loop(0, nq, per_query_tile, 0)


def _tile_order(bmin, bmax, nq, nk):
    smem = pl.BlockSpec(memory_space=pltpu.SMEM)
    kernel = functools.partial(_tile_order_kernel, nq=nq, nk=nk,
                               q_blocks=TQ // LANES, k_blocks=TK // LANES)
    return pl.pallas_call(
        kernel,
        in_specs=[smem, smem],
        out_specs=smem,
        out_shape=jax.ShapeDtypeStruct((nq * nk,), jnp.int32),
        name="tile_order",
    )(bmin, bmax)


def _qkv_kernel(x_ref, wqt_ref, wk_ref, wvt_ref, qt_ref, k_ref, vt_ref):
    xb = x_ref[...].astype(BF16)
    tm = xb.shape[0]
    qt_ref[...] = _dot_nt(wqt_ref[...], xb).astype(BF16)
    k_ref[...] = _dot(xb, wk_ref[...]).astype(BF16)
    vt = _dot_nt(wvt_ref[...], xb).astype(BF16)
    heads = vt.shape[0] // V_DIM
    vt_ref[:, 0, 0:V_DIM, :] = vt.reshape(heads, V_DIM, tm)
    vt_ref[:, 0, V_DIM:, :] = jnp.ones((heads, ONES_ROWS, tm), BF16)


def _qkv(x, w_qkv):
    S, D = x.shape
    tm = TM_QKV
    heads = D // V_DIM
    wqt = (w_qkv[:, :D] * (HEAD_DIM ** -0.5 * LOG2_E)).T.astype(BF16)
    wk = w_qkv[:, D:2 * D].astype(BF16)
    wvt = w_qkv[:, 2 * D:].T.astype(BF16)
    return pl.pallas_call(
        _qkv_kernel,
        grid=(S // tm,),
        in_specs=[
            pl.BlockSpec((tm, D), lambda i: (i, 0)),
            _const_spec((D, D)),
            _const_spec((D, D)),
            _const_spec((D, D)),
        ],
        out_specs=[
            pl.BlockSpec((D, tm), lambda i: (0, i)),
            pl.BlockSpec((tm, D), lambda i: (i, 0)),
            pl.BlockSpec((heads, 1, V_DIM + ONES_ROWS, tm), lambda i: (0, i, 0, 0)),
        ],
        out_shape=[
            jax.ShapeDtypeStruct((D, S), BF16),
            jax.ShapeDtypeStruct((S, D), BF16),
            jax.ShapeDtypeStruct((heads, S // tm, V_DIM + ONES_ROWS, tm), BF16),
        ],
        compiler_params=_params("parallel"),
        name="qkv_proj",
    )(x, wqt, wk, wvt)


def _bias_tile(posk_row, posq_row, table_row):
    tk = posk_row.shape[1]
    tq = posq_row.shape[1]
    table_sq = jnp.broadcast_to(table_row, (LANES, LANES))
    rows = []
    for c in range(tk // LANES):
        pk = posk_row[:, c * LANES:(c + 1) * LANES]
        pk_col = jnp.transpose(jnp.broadcast_to(pk, (LANES, LANES)))
        cols = []
        for d in range(tq // LANES):
            rel = pk_col - posq_row[:, d * LANES:(d + 1) * LANES]
            n = jnp.abs(rel)
            nsq = jnp.square(jnp.minimum(n, REL_CLIP)).astype(F32)
            log2_nsq = (lax.bitcast_convert_type(nsq, jnp.int32) >> F32_MANTISSA_BITS) - F32_EXP_BIAS
            large = jnp.minimum(log2_nsq + 2, REL_BUCKETS // 2 - 1)
            bucket = jnp.where(n < REL_BUCKETS // 4, n, large)
            bucket = bucket + jnp.where(rel > 0, REL_BUCKETS // 2, 0)
            cols.append(jnp.take_along_axis(table_sq, bucket, axis=1))
        rows.append(jnp.concatenate(cols, axis=1))
    return jnp.concatenate(rows, axis=0)


def _attn_kernel(bmin_ref, bmax_ref, tbl_ref, order_ref,
                 qt_ref, k_ref, vt_ref, posq_ref, posk_ref, tblv_ref, lam_ref, sg_ref,
                 o_ref, qcat, s_a, s_b, smax_a, smax_b, m_scr, acc_scr, *, nk, lambda_init):
    h = pl.program_id(0)
    qi = pl.program_id(1)
    tq = qt_ref.shape[1]
    tk = vt_ref.shape[3]
    heads = pl.num_programs(0)

    q = qt_ref[...]
    row = lax.broadcasted_iota(jnp.int32, q.shape, 0)
    zero = jnp.zeros_like(q)
    qcat[:, 0:tq] = jnp.where(row < HEAD_DIM, q, zero)
    qcat[:, tq:2 * tq] = jnp.where(row >= HEAD_DIM, q, zero)

    qmin = bmin_ref[qi * (tq // LANES)]
    qmax = bmax_ref[qi * (tq // LANES)]
    for r in range(1, tq // LANES):
        qmin = jnp.minimum(qmin, bmin_ref[qi * (tq // LANES) + r])
        qmax = jnp.maximum(qmax, bmax_ref[qi * (tq // LANES) + r])
    bias_before = tbl_ref[(REL_BUCKETS // 2 - 1) * heads + h]
    bias_after = tbl_ref[(REL_BUCKETS - 1) * heads + h]

    acc_scr[...] = jnp.zeros_like(acc_scr)
    m_scr[...] = jnp.full(m_scr.shape, -1e30, F32)

    def classify(kt):
        kmin = bmin_ref[kt * (tk // LANES)]
        kmax = bmax_ref[kt * (tk // LANES)]
        for r in range(1, tk // LANES):
            kmin = jnp.minimum(kmin, bmin_ref[kt * (tk // LANES) + r])
            kmax = jnp.maximum(kmax, bmax_ref[kt * (tk // LANES) + r])
        all_after = kmin - qmax >= REL_FAR
        all_before = kmax - qmin <= -REL_FAR
        near = jnp.logical_not(jnp.logical_or(all_after, all_before))
        const_bias = jnp.where(all_after, bias_after, jnp.where(all_before, bias_before, 0.0))
        return near, const_bias

    def logits(kt, s_ref, smax_ref):
        kk = k_ref[pl.ds(pl.multiple_of(kt * tk, tk), tk), :]
        s = _dot(kk, qcat[...])
        s_ref[...] = s
        smax_ref[...] = jnp.max(s, axis=0, keepdims=True)

    def add_near_bias(kt, s_ref, smax_ref):
        bias = _bias_tile(posk_ref[kt], posq_ref[...], tblv_ref[pl.ds(h, 1), :])
        s = s_ref[...] + jnp.concatenate([bias, bias], axis=1)
        s_ref[...] = s
        smax_ref[...] = jnp.max(s, axis=0, keepdims=True)

    def softmax_update(kt, s_ref, smax_ref, const_bias):
        m_old = m_scr[...]
        m_new = jnp.maximum(m_old, smax_ref[...] + const_bias)
        p = jnp.exp2(s_ref[...] - (m_new - const_bias)).astype(BF16)
        acc_scr[...] = jnp.exp2(m_old - m_new) * acc_scr[...] + _dot(vt_ref[0, kt], p)
        m_scr[...] = m_new

    def tile_at(step):
        return order_ref[qi * nk + step]

    def stage(step, cur, nxt, const_bias, fix_next):
        kt = tile_at(step)
        kn = tile_at(jnp.minimum(step + 1, nk - 1))
        logits(kn, *nxt)
        softmax_update(kt, *cur, const_bias)
        near_next, bias_next = classify(kn)
        if fix_next:
            @pl.when(near_next)
            def _():
                add_near_bias(kn, *nxt)
        return near_next, bias_next

    buf_a = (s_a, smax_a)
    buf_b = (s_b, smax_b)
    near0, bias0 = classify(tile_at(0))
    logits(tile_at(0), *buf_a)

    @pl.when(near0)
    def _():
        add_near_bias(tile_at(0), *buf_a)

    def pair(step, const_bias):
        _, const_bias = stage(step, buf_a, buf_b, const_bias, True)
        _, const_bias = stage(step + 1, buf_b, buf_a, const_bias, True)
        return const_bias

    def run_group(base, size, const_bias):
        inner_far = jnp.bool_(True)
        for i in range(1, size):
            inner_far = jnp.logical_and(inner_far, jnp.logical_not(classify(tile_at(base + i))[0]))

        def branch_free():
            c = const_bias
            for i in range(size):
                cur, nxt = (buf_a, buf_b) if i % 2 == 0 else (buf_b, buf_a)
                _, c = stage(base + i, cur, nxt, c, i == size - 1)
            return c

        def split():
            if size == MIN_STAGE_GROUP:
                return lax.fori_loop(0, size // 2, lambda t, c: pair(base + 2 * t, c), const_bias)
            half = size // 2
            return lax.fori_loop(0, 2, lambda t, c: run_group(base + t * half, half, c), const_bias)

        return lax.cond(inner_far, branch_free, split)

    lax.fori_loop(0, nk // STAGE_GROUP, lambda j, c: run_group(j * STAGE_GROUP, STAGE_GROUP, c), bias0)

    lp = lam_ref[...]
    lam = (jnp.exp(jnp.sum(lp[0:1] * lp[1:2], axis=1, keepdims=True))
           - jnp.exp(jnp.sum(lp[2:3] * lp[3:4], axis=1, keepdims=True)) + lambda_init)
    o1 = acc_scr[0:V_DIM, 0:tq] / acc_scr[V_DIM:V_DIM + 1, 0:tq]
    o2 = acc_scr[0:V_DIM, tq:2 * tq] / acc_scr[V_DIM:V_DIM + 1, tq:2 * tq]
    o = o1 - lam * o2
    o = o * lax.rsqrt(jnp.mean(o * o, axis=0, keepdims=True) + LN_EPS)
    o = o * sg_ref[...] * (1.0 - lambda_init)
    o_ref[...] = jnp.transpose(o).astype(BF16)


def _attention(qt, k, vt, positions, bmin, bmax, order, rel_table, lam_params, subln_g, lambda_init):
    D, S = qt.shape
    heads = D // V_DIM
    tq, tk = TQ, TK
    nk = S // tk
    assert S % tq == 0 and nk % STAGE_GROUP == 0, (S, tq, tk)
    table2 = rel_table.astype(F32) * LOG2_E
    table_rows = jnp.zeros((heads, LANES), F32).at[:, :REL_BUCKETS].set(table2.T)
    kernel = functools.partial(_attn_kernel, nk=nk, lambda_init=lambda_init)
    grid_spec = pltpu.PrefetchScalarGridSpec(
        num_scalar_prefetch=4,
        grid=(heads, S // tq),
        in_specs=[
            pl.BlockSpec((V_DIM, tq), lambda h, i, *_: (h, i)),
            pl.BlockSpec((S, V_DIM), lambda h, i, *_: (0, h)),
            pl.BlockSpec((1, nk, V_DIM + ONES_ROWS, tk), lambda h, i, *_: (h, 0, 0, 0)),
            pl.BlockSpec((1, tq), lambda h, i, *_: (0, i)),
            pl.BlockSpec((nk, 1, tk), lambda h, i, *_: (0, 0, 0)),
            pl.BlockSpec((heads, LANES), lambda h, i, *_: (0, 0)),
            pl.BlockSpec((4, HEAD_DIM), lambda h, i, *_: (0, 0)),
            pl.BlockSpec((V_DIM, 1), lambda h, i, *_: (0, 0)),
        ],
        out_specs=pl.BlockSpec((tq, V_DIM), lambda h, i, *_: (i, h)),
        scratch_shapes=[
            pltpu.VMEM((V_DIM, 2 * tq), BF16),
            pltpu.VMEM((tk, 2 * tq), F32),
            pltpu.VMEM((tk, 2 * tq), F32),
            pltpu.VMEM((1, 2 * tq), F32),
            pltpu.VMEM((1, 2 * tq), F32),
            pltpu.VMEM((1, 2 * tq), F32),
            pltpu.VMEM((V_DIM + ONES_ROWS, 2 * tq), F32),
        ],
    )
    return pl.pallas_call(
        kernel,
        grid_spec=grid_spec,
        out_shape=jax.ShapeDtypeStruct((S, D), BF16),
        compiler_params=_params("parallel", "parallel"),
        name="diff_attention",
    )(bmin, bmax, table2.reshape(-1), order,
      qt, k, vt, positions.reshape(1, S), positions.reshape(nk, 1, tk), table_rows,
      lam_params, subln_g.reshape(V_DIM, 1))


def _proj_ln_kernel(a_ref, x_ref, w_ref, lg_ref, lb_ref, o_ref):
    z = DEEPNORM_ALPHA * x_ref[...] + _dot(a_ref[...], w_ref[...])
    o_ref[...] = _layer_norm(z, lg_ref[...], lb_ref[...])


def _proj_ln(a, x, w, ln_g, ln_b):
    S, D = x.shape
    K = a.shape[1]
    tm = TM_PROJ
    return pl.pallas_call(
        _proj_ln_kernel,
        grid=(S // tm,),
        in_specs=[
            pl.BlockSpec((tm, K), lambda i: (i, 0)),
            pl.BlockSpec((tm, D), lambda i: (i, 0)),
            _const_spec((K, D)),
            _const_spec((1, D)),
            _const_spec((1, D)),
        ],
        out_specs=pl.BlockSpec((tm, D), lambda i: (i, 0)),
        out_shape=jax.ShapeDtypeStruct((S, D), F32),
        compiler_params=_params("parallel"),
        name="attn_out_proj",
    )(a, x, w.astype(BF16), ln_g.reshape(1, D), ln_b.reshape(1, D))


def _attn_layer(x, positions, bmin, bmax, order, rel_table, w_qkv, lam_params, subln_g, w_out,
                lambda_init, ln_g, ln_b):
    qt, k, vt = _qkv(x, w_qkv)
    o = _attention(qt, k, vt, positions, bmin, bmax, order, rel_table, lam_params, subln_g,
                   lambda_init)
    return _proj_ln(o, x, w_out, ln_g, ln_b)


def _ffn_kernel(x_ref, xp_ref, xn_ref, wi_ref, cw_ref, cb_ref, wd_ref, lg_ref, lb_ref,
                o_ref, g_scr):
    i = pl.program_id(0)
    x = x_ref[...]
    tm = x.shape[0]
    halo = xp_ref.shape[0]
    F = wd_ref.shape[0]
    xp = jnp.where(i > 0, xp_ref[...], 0.0)
    xn = jnp.where(i < pl.num_programs(0) - 1, xn_ref[...], 0.0)
    xe = jnp.concatenate([xp, x, xn], axis=0).astype(BF16)
    g_scr[...] = _dot(xe, wi_ref[:, 0:F])
    cw = cw_ref[...]
    gate = (g_scr[halo - 1:halo - 1 + tm, :] * cw[0:1]
            + g_scr[halo:halo + tm, :] * cw[1:2]
            + g_scr[halo + 1:halo + 1 + tm, :] * cw[2:3]
            + cb_ref[...])
    up = _dot(x.astype(BF16), wi_ref[:, F:2 * F])
    hidden = (_gelu(gate) * up).astype(BF16)
    z = DEEPNORM_ALPHA * x + _dot(hidden, wd_ref[...])
    o_ref[...] = _layer_norm(z, lg_ref[...], lb_ref[...])


def _ffn_layer(x, w_in, conv_w, conv_b, w_down, ln_g, ln_b):
    S, D = x.shape
    F = w_down.shape[0]
    tm, halo = TM_FFN, SUBLANES
    blocks_per_tile = tm // halo
    last_halo_block = S // halo - 1
    return pl.pallas_call(
        _ffn_kernel,
        grid=(S // tm,),
        in_specs=[
            pl.BlockSpec((tm, D), lambda i: (i, 0)),
            pl.BlockSpec((halo, D), lambda i: (jnp.maximum(i * blocks_per_tile - 1, 0), 0)),
            pl.BlockSpec((halo, D),
                         lambda i: (jnp.minimum((i + 1) * blocks_per_tile, last_halo_block), 0)),
            _const_spec((D, 2 * F)),
            _const_spec((conv_w.shape[0], F)),
            _const_spec((1, F)),
            _const_spec((F, D)),
            _const_spec((1, D)),
            _const_spec((1, D)),
        ],
        out_specs=pl.BlockSpec((tm, D), lambda i: (i, 0)),
        out_shape=jax.ShapeDtypeStruct((S, D), F32),
        scratch_shapes=[pltpu.VMEM((tm + 2 * halo, F), F32)],
        compiler_params=_params("parallel"),
        name="conv_glu_ffn",
    )(x, x, x, w_in.astype(BF16), conv_w, conv_b.reshape(1, F), w_down.astype(BF16),
      ln_g.reshape(1, D), ln_b.reshape(1, D))


def kernel(x, positions, rel_bias_table, a_w_in, a_norm_g, a_norm_b, a_w_s, a_b_s, a_w_out,
           b_w_qkv, b_lambda, b_subln_g, b_w_out, f_w_in, f_conv_w, f_conv_b, f_w_down,
           ln_g, ln_b):
    B, S, D = x.shape
    outs = []
    for b in range(B):
        xs = x[b]
        pos = positions[b]
        bmin, bmax = _pos_stats(pos)
        order = _tile_order(bmin, bmax, S // TQ, S // TK)
        for i in range(DEPTH):
            j = i // 2
            if i % 2 == 0:
                xs = _gmlp_layer(xs, a_w_in[j], a_norm_g[j], a_norm_b[j], a_w_s[j], a_b_s[j],
                                 a_w_out[j], ln_g[i, 0], ln_b[i, 0])
            else:
                xs = _attn_layer(xs, pos, bmin, bmax, order, rel_bias_table, b_w_qkv[j], b_lambda[j],
                                 b_subln_g[j], b_w_out[j], _lambda_init(i), ln_g[i, 0], ln_b[i, 0])
            xs = _ffn_layer(xs, f_w_in[i], f_conv_w[i], f_conv_b[i], f_w_down[i],
                            ln_g[i, 1], ln_b[i, 1])
        outs.append(xs)
    return jnp.stack(outs)
```

```python
import functools
import math

import jax
import jax.numpy as jnp
from jax import lax
from jax.experimental import pallas as pl
from jax.experimental.pallas import tpu as pltpu

F32 = jnp.float32
BF16 = jnp.bfloat16

DEPTH = 4
A_CHUNK = 128
A_GROUPS = 8
HEAD_DIM = 64
V_DIM = 2 * HEAD_DIM
REL_BUCKETS = 32
REL_FAR = 128
LN_EPS = 1e-5
LOG2_E = math.log2(math.e)
DEEPNORM_ALPHA = (2 * DEPTH) ** 0.25
REL_CLIP = 2047
F32_MANTISSA_BITS = 23
F32_EXP_BIAS = 127

LANES = 128
SUBLANES = 8
BF16_ROWS = 16
VMEM_LIMIT = 56 * 1024 * 1024

TM_GMLP = 512
TQ = 512
TK = 512
TM_QKV = TK
TM_PROJ = 512
TM_FFN = 512
STAGE_GROUP = 8
MIN_STAGE_GROUP = 2
ONES_ROWS = BF16_ROWS


def _lambda_init(layer_idx):
    return 0.8 - 0.6 * math.exp(-0.3 * layer_idx)


def _gelu(x):
    return 0.5 * x * (1.0 + lax.erf(x * (1.0 / math.sqrt(2.0))))


def _layer_norm(z, g, b):
    mu = jnp.mean(z, axis=-1, keepdims=True)
    zc = z - mu
    var = jnp.mean(zc * zc, axis=-1, keepdims=True)
    return zc * lax.rsqrt(var + LN_EPS) * g + b


def _dot(a, b):
    return jnp.dot(a, b, preferred_element_type=F32)


def _dot_nt(a, b):
    return lax.dot_general(a, b, (((1,), (1,)), ((), ())), preferred_element_type=F32)


def _const_spec(shape):
    nd = len(shape)
    return pl.BlockSpec(shape, lambda *_: (0,) * nd, pipeline_mode=pl.Buffered(1))


def _params(*sem):
    return pltpu.CompilerParams(dimension_semantics=sem, vmem_limit_bytes=VMEM_LIMIT)


def _gmlp_kernel(x_ref, wi_ref, ng_ref, nb_ref, ws_ref, bs_ref, wo_ref, lg_ref, lb_ref, o_ref):
    x = x_ref[...]
    tm = x.shape[0]
    W = wo_ref.shape[0]
    gd = W // A_GROUPS
    hidden = _gelu(_dot(x.astype(BF16), wi_ref[...]))
    v = _layer_norm(hidden[:, W:], ng_ref[...], nb_ref[...]).astype(BF16)
    mixed = []
    for c in range(tm // A_CHUNK):
        row = []
        for g in range(A_GROUPS):
            vc = v[c * A_CHUNK:(c + 1) * A_CHUNK, g * gd:(g + 1) * gd]
            row.append(_dot(ws_ref[g], vc) + bs_ref[g])
        mixed.append(jnp.concatenate(row, axis=1))
    y = (hidden[:, :W] * jnp.concatenate(mixed, axis=0)).astype(BF16)
    z = DEEPNORM_ALPHA * x + _dot(y, wo_ref[...])
    o_ref[...] = _layer_norm(z, lg_ref[...], lb_ref[...])


def _gmlp_layer(x, w_in, norm_g, norm_b, w_s, b_s, w_out, ln_g, ln_b):
    S, D = x.shape
    W = w_out.shape[0]
    tm = TM_GMLP
    return pl.pallas_call(
        _gmlp_kernel,
        grid=(S // tm,),
        in_specs=[
            pl.BlockSpec((tm, D), lambda i: (i, 0)),
            _const_spec((D, 2 * W)),
            _const_spec((1, W)),
            _const_spec((1, W)),
            _const_spec((A_GROUPS, A_CHUNK, A_CHUNK)),
            _const_spec((A_GROUPS, A_CHUNK, 1)),
            _const_spec((W, D)),
            _const_spec((1, D)),
            _const_spec((1, D)),
        ],
        out_specs=pl.BlockSpec((tm, D), lambda i: (i, 0)),
        out_shape=jax.ShapeDtypeStruct((S, D), F32),
        compiler_params=_params("parallel"),
        name="gmlp_layer",
    )(x, w_in.astype(BF16), norm_g.reshape(1, W), norm_b.reshape(1, W), w_s.astype(BF16),
      b_s.reshape(A_GROUPS, A_CHUNK, 1), w_out.astype(BF16), ln_g.reshape(1, D), ln_b.reshape(1, D))


def _pos_stats_kernel(p_ref, mn_ref, mx_ref):
    p = p_ref[...]
    mn_ref[...] = jnp.min(p, axis=1, keepdims=True)
    mx_ref[...] = jnp.max(p, axis=1, keepdims=True)


def _pos_stats(positions):
    nb = positions.shape[0] // LANES
    mn, mx = pl.pallas_call(
        _pos_stats_kernel,
        out_shape=(jax.ShapeDtypeStruct((nb, 1), jnp.int32),) * 2,
        name="pos_stats",
    )(positions.reshape(nb, LANES))
    return mn.reshape(nb), mx.reshape(nb)


def _tile_order_kernel(bmin_ref, bmax_ref, order_ref, *, nq, nk, q_blocks, k_blocks):
    n_groups = nk // STAGE_GROUP

    def per_query_tile(qi, carry):
        qmin = bmin_ref[qi * q_blocks]
        qmax = bmax_ref[qi * q_blocks]
        for r in range(1, q_blocks):
            qmin = jnp.minimum(qmin, bmin_ref[qi * q_blocks + r])
            qmax = jnp.maximum(qmax, bmax_ref[qi * q_blocks + r])

        def near(kt):
            kmin = bmin_ref[kt * k_blocks]
            kmax = bmax_ref[kt * k_blocks]
            for r in range(1, k_blocks):
                kmin = jnp.minimum(kmin, bmin_ref[kt * k_blocks + r])
                kmax = jnp.maximum(kmax, bmax_ref[kt * k_blocks + r])
            far = jnp.logical_or(kmin - qmax >= REL_FAR, kmax - qmin <= -REL_FAR)
            return jnp.logical_not(far).astype(jnp.int32)

        n_near = lax.fori_loop(0, nk, lambda kt, c: c + near(kt), 0)
        reorder = n_near <= n_groups

        def place(kt, state):
            near_seen, far_slot = state
            is_near = near(kt)
            slot = jnp.where(is_near == 1, near_seen * STAGE_GROUP, far_slot)
            order_ref[qi * nk + jnp.where(reorder, slot, kt)] = kt
            nxt = far_slot + 1
            held = jnp.logical_and(nxt % STAGE_GROUP == 0, nxt // STAGE_GROUP < n_near)
            nxt = jnp.where(held, nxt + 1, nxt)
            return near_seen + is_near, jnp.where(is_near == 1, far_slot, nxt)

        lax.fori_loop(0, nk, place, (0, jnp.where(n_near > 0, 1, 0)))
        return carry

    lax.fori_loop(0, nq, per_query_tile, 0)


def _tile_order(bmin, bmax, nq, nk):
    smem = pl.BlockSpec(memory_space=pltpu.SMEM)
    kernel = functools.partial(_tile_order_kernel, nq=nq, nk=nk,
                               q_blocks=TQ // LANES, k_blocks=TK // LANES)
    return pl.pallas_call(
        kernel,
        in_specs=[smem, smem],
        out_specs=smem,
        out_shape=jax.ShapeDtypeStruct((nq * nk,), jnp.int32),
        name="tile_order",
    )(bmin, bmax)


def _qkv_kernel(x_ref, wqt_ref, wk_ref, wvt_ref, qt_ref, k_ref, vt_ref):
    xb = x_ref[...].astype(BF16)
    tm = xb.shape[0]
    qt_ref[...] = _dot_nt(wqt_ref[...], xb).astype(BF16)
    k_ref[...] = _dot(xb, wk_ref[...]).astype(BF16)
    vt = _dot_nt(wvt_ref[...], xb).astype(BF16)
    heads = vt.shape[0] // V_DIM
    vt_ref[:, 0, 0:V_DIM, :] = vt.reshape(heads, V_DIM, tm)
    vt_ref[:, 0, V_DIM:, :] = jnp.ones((heads, ONES_ROWS, tm), BF16)


def _qkv(x, w_qkv):
    S, D = x.shape
    tm = TM_QKV
    heads = D // V_DIM
    wqt = (w_qkv[:, :D] * (HEAD_DIM ** -0.5 * LOG2_E)).T.astype(BF16)
    wk = w_qkv[:, D:2 * D].astype(BF16)
    wvt = w_qkv[:, 2 * D:].T.astype(BF16)
    return pl.pallas_call(
        _qkv_kernel,
        grid=(S // tm,),
        in_specs=[
            pl.BlockSpec((tm, D), lambda i: (i, 0)),
            _const_spec((D, D)),
            _const_spec((D, D)),
            _const_spec((D, D)),
        ],
        out_specs=[
            pl.BlockSpec((D, tm), lambda i: (0, i)),
            pl.BlockSpec((tm, D), lambda i: (i, 0)),
            pl.BlockSpec((heads, 1, V_DIM + ONES_ROWS, tm), lambda i: (0, i, 0, 0)),
        ],
        out_shape=[
            jax.ShapeDtypeStruct((D, S), BF16),
            jax.ShapeDtypeStruct((S, D), BF16),
            jax.ShapeDtypeStruct((heads, S // tm, V_DIM + ONES_ROWS, tm), BF16),
        ],
        compiler_params=_params("parallel"),
        name="qkv_proj",
    )(x, wqt, wk, wvt)


def _bias_tile(posk_row, posq_row, table_row):
    tk = posk_row.shape[1]
    tq = posq_row.shape[1]
    table_sq = jnp.broadcast_to(table_row, (LANES, LANES))
    rows = []
    for c in range(tk // LANES):
        pk = posk_row[:, c * LANES:(c + 1) * LANES]
        pk_col = jnp.transpose(jnp.broadcast_to(pk, (LANES, LANES)))
        cols = []
        for d in range(tq // LANES):
            rel = pk_col - posq_row[:, d * LANES:(d + 1) * LANES]
            n = jnp.abs(rel)
            nsq = jnp.square(jnp.minimum(n, REL_CLIP)).astype(F32)
            log2_nsq = (lax.bitcast_convert_type(nsq, jnp.int32) >> F32_MANTISSA_BITS) - F32_EXP_BIAS
            large = jnp.minimum(log2_nsq + 2, REL_BUCKETS // 2 - 1)
            bucket = jnp.where(n < REL_BUCKETS // 4, n, large)
            bucket = bucket + jnp.where(rel > 0, REL_BUCKETS // 2, 0)
            cols.append(jnp.take_along_axis(table_sq, bucket, axis=1))
        rows.append(jnp.concatenate(cols, axis=1))
    return jnp.concatenate(rows, axis=0)


def _attn_kernel(bmin_ref, bmax_ref, tbl_ref, order_ref,
                 qt_ref, k_ref, vt_ref, posq_ref, posk_ref, tblv_ref, lam_ref, sg_ref,
                 o_ref, qcat, s_a, s_b, smax_a, smax_b, m_scr, acc_scr, *, nk, lambda_init):
    h = pl.program_id(0)
    qi = pl.program_id(1)
    tq = qt_ref.shape[1]
    tk = vt_ref.shape[3]
    heads = pl.num_programs(0)

    q = qt_ref[...]
    row = lax.broadcasted_iota(jnp.int32, q.shape, 0)
    zero = jnp.zeros_like(q)
    qcat[:, 0:tq] = jnp.where(row < HEAD_DIM, q, zero)
    qcat[:, tq:2 * tq] = jnp.where(row >= HEAD_DIM, q, zero)

    qmin = bmin_ref[qi * (tq // LANES)]
    qmax = bmax_ref[qi * (tq // LANES)]
    for r in range(1, tq // LANES):
        qmin = jnp.minimum(qmin, bmin_ref[qi * (tq // LANES) + r])
        qmax = jnp.maximum(qmax, bmax_ref[qi * (tq // LANES) + r])
    bias_before = tbl_ref[(REL_BUCKETS // 2 - 1) * heads + h]
    bias_after = tbl_ref[(REL_BUCKETS - 1) * heads + h]

    acc_scr[...] = jnp.zeros_like(acc_scr)
    m_scr[...] = jnp.full(m_scr.shape, -1e30, F32)

    def classify(kt):
        kmin = bmin_ref[kt * (tk // LANES)]
        kmax = bmax_ref[kt * (tk // LANES)]
        for r in range(1, tk // LANES):
            kmin = jnp.minimum(kmin, bmin_ref[kt * (tk // LANES) + r])
            kmax = jnp.maximum(kmax, bmax_ref[kt * (tk // LANES) + r])
        all_after = kmin - qmax >= REL_FAR
        all_before = kmax - qmin <= -REL_FAR
        near = jnp.logical_not(jnp.logical_or(all_after, all_before))
        const_bias = jnp.where(all_after, bias_after, jnp.where(all_before, bias_before, 0.0))
        return near, const_bias

    def logits(kt, s_ref, smax_ref):
        kk = k_ref[pl.ds(pl.multiple_of(kt * tk, tk), tk), :]
        s = _dot(kk, qcat[...])
        s_ref[...] = s
        smax_ref[...] = jnp.max(s, axis=0, keepdims=True)

    def add_near_bias(kt, s_ref, smax_ref):
        bias = _bias_tile(posk_ref[kt], posq_ref[...], tblv_ref[pl.ds(h, 1), :])
        s = s_ref[...] + jnp.concatenate([bias, bias], axis=1)
        s_ref[...] = s
        smax_ref[...] = jnp.max(s, axis=0, keepdims=True)

    def softmax_update(kt, s_ref, smax_ref, const_bias):
        m_old = m_scr[...]
        m_new = jnp.maximum(m_old, smax_ref[...] + const_bias)
        p = jnp.exp2(s_ref[...] - (m_new - const_bias)).astype(BF16)
        acc_scr[...] = jnp.exp2(m_old - m_new) * acc_scr[...] + _dot(vt_ref[0, kt], p)
        m_scr[...] = m_new

    def tile_at(step):
        return order_ref[qi * nk + step]

    def stage(step, cur, nxt, const_bias, fix_next):
        kt = tile_at(step)
        kn = tile_at(jnp.minimum(step + 1, nk - 1))
        logits(kn, *nxt)
        softmax_update(kt, *cur, const_bias)
        near_next, bias_next = classify(kn)
        if fix_next:
            @pl.when(near_next)
            def _():
                add_near_bias(kn, *nxt)
        return near_next, bias_next

    buf_a = (s_a, smax_a)
    buf_b = (s_b, smax_b)
    near0, bias0 = classify(tile_at(0))
    logits(tile_at(0), *buf_a)

    @pl.when(near0)
    def _():
        add_near_bias(tile_at(0), *buf_a)

    def pair(step, const_bias):
        _, const_bias = stage(step, buf_a, buf_b, const_bias, True)
        _, const_bias = stage(step + 1, buf_b, buf_a, const_bias, True)
        return const_bias

    def run_group(base, size, const_bias):
        inner_far = jnp.bool_(True)
        for i in range(1, size):
            inner_far = jnp.logical_and(inner_far, jnp.logical_not(classify(tile_at(base + i))[0]))

        def branch_free():
            c = const_bias
            for i in range(size):
                cur, nxt = (buf_a, buf_b) if i % 2 == 0 else (buf_b, buf_a)
                _, c = stage(base + i, cur, nxt, c, i == size - 1)
            return c

        def split():
            if size == MIN_STAGE_GROUP:
                return lax.fori_loop(0, size // 2, lambda t, c: pair(base + 2 * t, c), const_bias)
            half = size // 2
            return lax.fori_loop(0, 2, lambda t, c: run_group(base + t * half, half, c), const_bias)

        return lax.cond(inner_far, branch_free, split)

    lax.fori_loop(0, nk // STAGE_GROUP, lambda j, c: run_group(j * STAGE_GROUP, STAGE_GROUP, c), bias0)

    lp = lam_ref[...]
    lam = (jnp.exp(jnp.sum(lp[0:1] * lp[1:2], axis=1, keepdims=True))
           - jnp.exp(jnp.sum(lp[2:3] * lp[3:4], axis=1, keepdims=True)) + lambda_init)
    o1 = acc_scr[0:V_DIM, 0:tq] / acc_scr[V_DIM:V_DIM + 1, 0:tq]
    o2 = acc_scr[0:V_DIM, tq:2 * tq] / acc_scr[V_DIM:V_DIM + 1, tq:2 * tq]
    o = o1 - lam * o2
    o = o * lax.rsqrt(jnp.mean(o * o, axis=0, keepdims=True) + LN_EPS)
    o = o * sg_ref[...] * (1.0 - lambda_init)
    o_ref[...] = jnp.transpose(o).astype(BF16)


def _attention(qt, k, vt, positions, bmin, bmax, order, rel_table, lam_params, subln_g, lambda_init):
    D, S = qt.shape
    heads = D // V_DIM
    tq, tk = TQ, TK
    nk = S // tk
    assert S % tq == 0 and nk % STAGE_GROUP == 0, (S, tq, tk)
    table2 = rel_table.astype(F32) * LOG2_E
    table_rows = jnp.zeros((heads, LANES), F32).at[:, :REL_BUCKETS].set(table2.T)
    kernel = functools.partial(_attn_kernel, nk=nk, lambda_init=lambda_init)
    grid_spec = pltpu.PrefetchScalarGridSpec(
        num_scalar_prefetch=4,
        grid=(heads, S // tq),
        in_specs=[
            pl.BlockSpec((V_DIM, tq), lambda h, i, *_: (h, i)),
            pl.BlockSpec((S, V_DIM), lambda h, i, *_: (0, h)),
            pl.BlockSpec((1, nk, V_DIM + ONES_ROWS, tk), lambda h, i, *_: (h, 0, 0, 0)),
            pl.BlockSpec((1, tq), lambda h, i, *_: (0, i)),
            pl.BlockSpec((nk, 1, tk), lambda h, i, *_: (0, 0, 0)),
            pl.BlockSpec((heads, LANES), lambda h, i, *_: (0, 0)),
            pl.BlockSpec((4, HEAD_DIM), lambda h, i, *_: (0, 0)),
            pl.BlockSpec((V_DIM, 1), lambda h, i, *_: (0, 0)),
        ],
        out_specs=pl.BlockSpec((tq, V_DIM), lambda h, i, *_: (i, h)),
        scratch_shapes=[
            pltpu.VMEM((V_DIM, 2 * tq), BF16),
            pltpu.VMEM((tk, 2 * tq), F32),
            pltpu.VMEM((tk, 2 * tq), F32),
            pltpu.VMEM((1, 2 * tq), F32),
            pltpu.VMEM((1, 2 * tq), F32),
            pltpu.VMEM((1, 2 * tq), F32),
            pltpu.VMEM((V_DIM + ONES_ROWS, 2 * tq), F32),
        ],
    )
    return pl.pallas_call(
        kernel,
        grid_spec=grid_spec,
        out_shape=jax.ShapeDtypeStruct((S, D), BF16),
        compiler_params=_params("parallel", "parallel"),
        name="diff_attention",
    )(bmin, bmax, table2.reshape(-1), order,
      qt, k, vt, positions.reshape(1, S), positions.reshape(nk, 1, tk), table_rows,
      lam_params, subln_g.reshape(V_DIM, 1))


def _proj_ln_kernel(a_ref, x_ref, w_ref, lg_ref, lb_ref, o_ref):
    z = DEEPNORM_ALPHA * x_ref[...] + _dot(a_ref[...], w_ref[...])
    o_ref[...] = _layer_norm(z, lg_ref[...], lb_ref[...])


def _proj_ln(a, x, w, ln_g, ln_b):
    S, D = x.shape
    K = a.shape[1]
    tm = TM_PROJ
    return pl.pallas_call(
        _proj_ln_kernel,
        grid=(S // tm,),
        in_specs=[
            pl.BlockSpec((tm, K), lambda i: (i, 0)),
            pl.BlockSpec((tm, D), lambda i: (i, 0)),
            _const_spec((K, D)),
            _const_spec((1, D)),
            _const_spec((1, D)),
        ],
        out_specs=pl.BlockSpec((tm, D), lambda i: (i, 0)),
        out_shape=jax.ShapeDtypeStruct((S, D), F32),
        compiler_params=_params("parallel"),
        name="attn_out_proj",
    )(a, x, w.astype(BF16), ln_g.reshape(1, D), ln_b.reshape(1, D))


def _attn_layer(x, positions, bmin, bmax, order, rel_table, w_qkv, lam_params, subln_g, w_out,
                lambda_init, ln_g, ln_b):
    qt, k, vt = _qkv(x, w_qkv)
    o = _attention(qt, k, vt, positions, bmin, bmax, order, rel_table, lam_params, subln_g,
                   lambda_init)
    return _proj_ln(o, x, w_out, ln_g, ln_b)


def _ffn_kernel(x_ref, xp_ref, xn_ref, wi_ref, cw_ref, cb_ref, wd_ref, lg_ref, lb_ref,
                o_ref, g_scr):
    i = pl.program_id(0)
    x = x_ref[...]
    tm = x.shape[0]
    halo = xp_ref.shape[0]
    F = wd_ref.shape[0]
    xp = jnp.where(i > 0, xp_ref[...], 0.0)
    xn = jnp.where(i < pl.num_programs(0) - 1, xn_ref[...], 0.0)
    xe = jnp.concatenate([xp, x, xn], axis=0).astype(BF16)
    g_scr[...] = _dot(xe, wi_ref[:, 0:F])
    cw = cw_ref[...]
    gate = (g_scr[halo - 1:halo - 1 + tm, :] * cw[0:1]
            + g_scr[halo:halo + tm, :] * cw[1:2]
            + g_scr[halo + 1:halo + 1 + tm, :] * cw[2:3]
            + cb_ref[...])
    up = _dot(x.astype(BF16), wi_ref[:, F:2 * F])
    hidden = (_gelu(gate) * up).astype(BF16)
    z = DEEPNORM_ALPHA * x + _dot(hidden, wd_ref[...])
    o_ref[...] = _layer_norm(z, lg_ref[...], lb_ref[...])


def _ffn_layer(x, w_in, conv_w, conv_b, w_down, ln_g, ln_b):
    S, D = x.shape
    F = w_down.shape[0]
    tm, halo = TM_FFN, SUBLANES
    blocks_per_tile = tm // halo
    last_halo_block = S // halo - 1
    return pl.pallas_call(
        _ffn_kernel,
        grid=(S // tm,),
        in_specs=[
            pl.BlockSpec((tm, D), lambda i: (i, 0)),
            pl.BlockSpec((halo, D), lambda i: (jnp.maximum(i * blocks_per_tile - 1, 0), 0)),
            pl.BlockSpec((halo, D),
                         lambda i: (jnp.minimum((i + 1) * blocks_per_tile, last_halo_block), 0)),
            _const_spec((D, 2 * F)),
            _const_spec((conv_w.shape[0], F)),
            _const_spec((1, F)),
            _const_spec((F, D)),
            _const_spec((1, D)),
            _const_spec((1, D)),
        ],
        out_specs=pl.BlockSpec((tm, D), lambda i: (i, 0)),
        out_shape=jax.ShapeDtypeStruct((S, D), F32),
        scratch_shapes=[pltpu.VMEM((tm + 2 * halo, F), F32)],
        compiler_params=_params("parallel"),
        name="conv_glu_ffn",
    )(x, x, x, w_in.astype(BF16), conv_w, conv_b.reshape(1, F), w_down.astype(BF16),
      ln_g.reshape(1, D), ln_b.reshape(1, D))


def kernel(x, positions, rel_bias_table, a_w_in, a_norm_g, a_norm_b, a_w_s, a_b_s, a_w_out,
           b_w_qkv, b_lambda, b_subln_g, b_w_out, f_w_in, f_conv_w, f_conv_b, f_w_down,
           ln_g, ln_b):
    B, S, D = x.shape
    outs = []
    for b in range(B):
        xs = x[b]
        pos = positions[b]
        bmin, bmax = _pos_stats(pos)
        order = _tile_order(bmin, bmax, S // TQ, S // TK)
        for i in range(DEPTH):
            j = i // 2
            if i % 2 == 0:
                xs = _gmlp_layer(xs, a_w_in[j], a_norm_g[j], a_norm_b[j], a_w_s[j], a_b_s[j],
                                 a_w_out[j], ln_g[i, 0], ln_b[i, 0])
            else:
                xs = _attn_layer(xs, pos, bmin, bmax, order, rel_bias_table, b_w_qkv[j], b_lambda[j],
                                 b_subln_g[j], b_w_out[j], _lambda_init(i), ln_g[i, 0], ln_b[i, 0])
            xs = _ffn_layer(xs, f_w_in[i], f_conv_w[i], f_conv_b[i], f_w_down[i],
                            ln_g[i, 1], ln_b[i, 1])
        outs.append(xs)
    return jnp.stack(outs)
```

```python
import functools
import math

import jax
import jax.numpy as jnp
from jax import lax
from jax.experimental import pallas as pl
from jax.experimental.pallas import tpu as pltpu

F32 = jnp.float32
BF16 = jnp.bfloat16

DEPTH = 4
A_CHUNK = 128
A_GROUPS = 8
HEAD_DIM = 64
V_DIM = 2 * HEAD_DIM
REL_BUCKETS = 32
REL_FAR = 128
LN_EPS = 1e-5
LOG2_E = math.log2(math.e)
MASKED_LOGIT = -1e30
DEEPNORM_ALPHA = (2 * DEPTH) ** 0.25
REL_CLIP = 2047
F32_MANTISSA_BITS = 23
F32_EXP_BIAS = 127

LANES = 128
SUBLANES = 8
BF16_ROWS = 16
VMEM_BYTES = 64 * 1024 * 1024
VMEM_LIMIT = VMEM_BYTES * 3 // 4

TM_GMLP = 512
TQ = 512
TK = 512
TM_QKV = TK
TM_PROJ = 512
TM_FFN = 512
STAGE_GROUP = 8
MIN_STAGE_GROUP = 2
ONES_ROWS = BF16_ROWS


def _lambda_init(layer_idx):
    return 0.8 - 0.6 * math.exp(-0.3 * layer_idx)


def _gelu(x):
    return 0.5 * x * (1.0 + lax.erf(x * (1.0 / math.sqrt(2.0))))


def _layer_norm(z, g, b):
    mu = jnp.mean(z, axis=-1, keepdims=True)
    zc = z - mu
    var = jnp.mean(zc * zc, axis=-1, keepdims=True)
    return zc * lax.rsqrt(var + LN_EPS) * g + b


def _dot(a, b):
    return jnp.dot(a, b, preferred_element_type=F32)


def _dot_nt(a, b):
    return lax.dot_general(a, b, (((1,), (1,)), ((), ())), preferred_element_type=F32)


def _const_spec(shape):
    nd = len(shape)
    return pl.BlockSpec(shape, lambda *_: (0,) * nd, pipeline_mode=pl.Buffered(1))


def _params(*sem):
    return pltpu.CompilerParams(dimension_semantics=sem, vmem_limit_bytes=VMEM_LIMIT)


def _gmlp_kernel(x_ref, wi_ref, ng_ref, nb_ref, ws_ref, bs_ref, wo_ref, lg_ref, lb_ref, o_ref):
    x = x_ref[...]
    tm = x.shape[0]
    W = wo_ref.shape[0]
    gd = W // A_GROUPS
    hidden = _gelu(_dot(x.astype(BF16), wi_ref[...]))
    v = _layer_norm(hidden[:, W:], ng_ref[...], nb_ref[...]).astype(BF16)
    mixed = []
    for c in range(tm // A_CHUNK):
        row = []
        for g in range(A_GROUPS):
            vc = v[c * A_CHUNK:(c + 1) * A_CHUNK, g * gd:(g + 1) * gd]
            row.append(_dot(ws_ref[g], vc) + bs_ref[g])
        mixed.append(jnp.concatenate(row, axis=1))
    y = (hidden[:, :W] * jnp.concatenate(mixed, axis=0)).astype(BF16)
    z = DEEPNORM_ALPHA * x + _dot(y, wo_ref[...])
    o_ref[...] = _layer_norm(z, lg_ref[...], lb_ref[...])


def _gmlp_layer(x, w_in, norm_g, norm_b, w_s, b_s, w_out, ln_g, ln_b):
    S, D = x.shape
    W = w_out.shape[0]
    tm = TM_GMLP
    return pl.pallas_call(
        _gmlp_kernel,
        grid=(S // tm,),
        in_specs=[
            pl.BlockSpec((tm, D), lambda i: (i, 0)),
            _const_spec((D, 2 * W)),
            _const_spec((1, W)),
            _const_spec((1, W)),
            _const_spec((A_GROUPS, A_CHUNK, A_CHUNK)),
            _const_spec((A_GROUPS, A_CHUNK, 1)),
            _const_spec((W, D)),
            _const_spec((1, D)),
            _const_spec((1, D)),
        ],
        out_specs=pl.BlockSpec((tm, D), lambda i: (i, 0)),
        out_shape=jax.ShapeDtypeStruct((S, D), F32),
        compiler_params=_params("parallel"),
        name="gmlp_layer",
    )(x, w_in.astype(BF16), norm_g.reshape(1, W), norm_b.reshape(1, W), w_s.astype(BF16),
      b_s.reshape(A_GROUPS, A_CHUNK, 1), w_out.astype(BF16), ln_g.reshape(1, D), ln_b.reshape(1, D))


def _pos_stats_kernel(p_ref, mn_ref, mx_ref):
    p = p_ref[...]
    mn_ref[...] = jnp.min(p, axis=1, keepdims=True)
    mx_ref[...] = jnp.max(p, axis=1, keepdims=True)


def _pos_stats(positions):
    nb = positions.shape[0] // LANES
    mn, mx = pl.pallas_call(
        _pos_stats_kernel,
        out_shape=(jax.ShapeDtypeStruct((nb, 1), jnp.int32),) * 2,
        name="pos_stats",
    )(positions.reshape(nb, LANES))
    return mn.reshape(nb), mx.reshape(nb)


def _tile_order_kernel(bmin_ref, bmax_ref, order_ref, near_ref, *, nq, nk, q_blocks, k_blocks):
    n_groups = nk // STAGE_GROUP

    def per_query_tile(qi, carry):
        qmin = bmin_ref[qi * q_blocks]
        qmax = bmax_ref[qi * q_blocks]
        for r in range(1, q_blocks):
            qmin = jnp.minimum(qmin, bmin_ref[qi * q_blocks + r])
            qmax = jnp.maximum(qmax, bmax_ref[qi * q_blocks + r])

        def near(kt):
            kmin = bmin_ref[kt * k_blocks]
            kmax = bmax_ref[kt * k_blocks]
            for r in range(1, k_blocks):
                kmin = jnp.minimum(kmin, bmin_ref[kt * k_blocks + r])
                kmax = jnp.maximum(kmax, bmax_ref[kt * k_blocks + r])
            far = jnp.logical_or(kmin - qmax >= REL_FAR, kmax - qmin <= -REL_FAR)
            return jnp.logical_not(far).astype(jnp.int32)

        def count(kt, total):
            near_ref[kt] = near(kt)
            return total + near_ref[kt]

        n_near = lax.fori_loop(0, nk, count, 0)
        reorder = n_near <= n_groups

        def place(kt, state):
            near_seen, far_slot = state
            is_near = near_ref[kt]
            slot = jnp.where(is_near == 1, near_seen * STAGE_GROUP, far_slot)
            order_ref[qi * nk + jnp.where(reorder, slot, kt)] = kt
            nxt = far_slot + 1
            held = jnp.logical_and(nxt % STAGE_GROUP == 0, nxt // STAGE_GROUP < n_near)
            nxt = jnp.where(held, nxt + 1, nxt)
            return near_seen + is_near, jnp.where(is_near == 1, far_slot, nxt)

        lax.fori_loop(0, nk, place, (0, jnp.where(n_near > 0, 1, 0)))
        return carry

    lax.fori_loop(0, nq, per_query_tile, 0)


def _tile_order(bmin, bmax, nq, nk):
    smem = pl.BlockSpec(memory_space=pltpu.SMEM)
    kernel = functools.partial(_tile_order_kernel, nq=nq, nk=nk,
                               q_blocks=TQ // LANES, k_blocks=TK // LANES)
    return pl.pallas_call(
        kernel,
        in_specs=[smem, smem],
        out_specs=smem,
        out_shape=jax.ShapeDtypeStruct((nq * nk,), jnp.int32),
        scratch_shapes=[pltpu.SMEM((nk,), jnp.int32)],
        name="tile_order",
    )(bmin, bmax)


def _qkv_kernel(x_ref, wqt_ref, wk_ref, wvt_ref, qt_ref, k_ref, vt_ref):
    xb = x_ref[...].astype(BF16)
    tm = xb.shape[0]
    qt_ref[...] = _dot_nt(wqt_ref[...], xb).astype(BF16)
    k_ref[...] = _dot(xb, wk_ref[...]).astype(BF16)
    vt = _dot_nt(wvt_ref[...], xb).astype(BF16)
    heads = vt.shape[0] // V_DIM
    vt_ref[:, 0, 0:V_DIM, :] = vt.reshape(heads, V_DIM, tm)
    vt_ref[:, 0, V_DIM:, :] = jnp.ones((heads, ONES_ROWS, tm), BF16)


def _qkv(x, w_qkv):
    S, D = x.shape
    tm = TM_QKV
    heads = D // V_DIM
    wqt = (w_qkv[:, :D] * (HEAD_DIM ** -0.5 * LOG2_E)).T.astype(BF16)
    wk = w_qkv[:, D:2 * D].astype(BF16)
    wvt = w_qkv[:, 2 * D:].T.astype(BF16)
    return pl.pallas_call(
        _qkv_kernel,
        grid=(S // tm,),
        in_specs=[
            pl.BlockSpec((tm, D), lambda i: (i, 0)),
            _const_spec((D, D)),
            _const_spec((D, D)),
            _const_spec((D, D)),
        ],
        out_specs=[
            pl.BlockSpec((D, tm), lambda i: (0, i)),
            pl.BlockSpec((tm, D), lambda i: (i, 0)),
            pl.BlockSpec((heads, 1, V_DIM + ONES_ROWS, tm), lambda i: (0, i, 0, 0)),
        ],
        out_shape=[
            jax.ShapeDtypeStruct((D, S), BF16),
            jax.ShapeDtypeStruct((S, D), BF16),
            jax.ShapeDtypeStruct((heads, S // tm, V_DIM + ONES_ROWS, tm), BF16),
        ],
        compiler_params=_params("parallel"),
        name="qkv_proj",
    )(x, wqt, wk, wvt)


def _bias_tile(posk_row, posq_row, table_row):
    tk = posk_row.shape[1]
    tq = posq_row.shape[1]
    table_sq = jnp.broadcast_to(table_row, (LANES, LANES))
    rows = []
    for c in range(tk // LANES):
        pk = posk_row[:, c * LANES:(c + 1) * LANES]
        pk_col = jnp.transpose(jnp.broadcast_to(pk, (LANES, LANES)))
        cols = []
        for d in range(tq // LANES):
            rel = pk_col - posq_row[:, d * LANES:(d + 1) * LANES]
            n = jnp.abs(rel)
            nsq = jnp.square(jnp.minimum(n, REL_CLIP)).astype(F32)
            log2_nsq = (lax.bitcast_convert_type(nsq, jnp.int32) >> F32_MANTISSA_BITS) - F32_EXP_BIAS
            large = jnp.minimum(log2_nsq + 2, REL_BUCKETS // 2 - 1)
            bucket = jnp.where(n < REL_BUCKETS // 4, n, large)
            bucket = bucket + jnp.where(rel > 0, REL_BUCKETS // 2, 0)
            cols.append(jnp.take_along_axis(table_sq, bucket, axis=1))
        rows.append(jnp.concatenate(cols, axis=1))
    return jnp.concatenate(rows, axis=0)


def _attn_kernel(bmin_ref, bmax_ref, tbl_ref, order_ref,
                 qt_ref, k_ref, vt_ref, posq_ref, posk_ref, tblv_ref, lam_ref, sg_ref,
                 o_ref, qcat, s_a, s_b, smax_a, smax_b, m_scr, acc_scr, *, nk, lambda_init):
    h = pl.program_id(0)
    qi = pl.program_id(1)
    tq = qt_ref.shape[1]
    tk = vt_ref.shape[3]
    heads = pl.num_programs(0)

    q = qt_ref[...]
    row = lax.broadcasted_iota(jnp.int32, q.shape, 0)
    zero = jnp.zeros_like(q)
    qcat[:, 0:tq] = jnp.where(row < HEAD_DIM, q, zero)
    qcat[:, tq:2 * tq] = jnp.where(row >= HEAD_DIM, q, zero)

    qmin = bmin_ref[qi * (tq // LANES)]
    qmax = bmax_ref[qi * (tq // LANES)]
    for r in range(1, tq // LANES):
        qmin = jnp.minimum(qmin, bmin_ref[qi * (tq // LANES) + r])
        qmax = jnp.maximum(qmax, bmax_ref[qi * (tq // LANES) + r])
    bias_before = tbl_ref[(REL_BUCKETS // 2 - 1) * heads + h]
    bias_after = tbl_ref[(REL_BUCKETS - 1) * heads + h]

    acc_scr[...] = jnp.zeros_like(acc_scr)
    m_scr[...] = jnp.full(m_scr.shape, MASKED_LOGIT, F32)

    def classify(kt):
        kmin = bmin_ref[kt * (tk // LANES)]
        kmax = bmax_ref[kt * (tk // LANES)]
        for r in range(1, tk // LANES):
            kmin = jnp.minimum(kmin, bmin_ref[kt * (tk // LANES) + r])
            kmax = jnp.maximum(kmax, bmax_ref[kt * (tk // LANES) + r])
        all_after = kmin - qmax >= REL_FAR
        all_before = kmax - qmin <= -REL_FAR
        near = jnp.logical_not(jnp.logical_or(all_after, all_before))
        const_bias = jnp.where(all_after, bias_after, jnp.where(all_before, bias_before, 0.0))
        return near, const_bias

    def logits(kt, s_ref, smax_ref):
        kk = k_ref[pl.ds(pl.multiple_of(kt * tk, tk), tk), :]
        s = _dot(kk, qcat[...])
        s_ref[...] = s
        smax_ref[...] = jnp.max(s, axis=0, keepdims=True)

    def add_near_bias(kt, s_ref, smax_ref):
        bias = _bias_tile(posk_ref[kt], posq_ref[...], tblv_ref[pl.ds(h, 1), :])
        s = s_ref[...] + jnp.concatenate([bias, bias], axis=1)
        s_ref[...] = s
        smax_ref[...] = jnp.max(s, axis=0, keepdims=True)

    def softmax_update(kt, s_ref, smax_ref, const_bias):
        m_old = m_scr[...]
        m_new = jnp.maximum(m_old, smax_ref[...] + const_bias)
        p = jnp.exp2(s_ref[...] - (m_new - const_bias)).astype(BF16)
        acc_scr[...] = jnp.exp2(m_old - m_new) * acc_scr[...] + _dot(vt_ref[0, kt], p)
        m_scr[...] = m_new

    def tile_at(step):
        return order_ref[qi * nk + step]

    def stage(step, cur, nxt, const_bias, fix_next):
        kt = tile_at(step)
        kn = tile_at(jnp.minimum(step + 1, nk - 1))
        logits(kn, *nxt)
        softmax_update(kt, *cur, const_bias)
        near_next, bias_next = classify(kn)
        if fix_next:
            @pl.when(near_next)
            def _():
                add_near_bias(kn, *nxt)
        return near_next, bias_next

    buf_a = (s_a, smax_a)
    buf_b = (s_b, smax_b)
    near0, bias0 = classify(tile_at(0))
    logits(tile_at(0), *buf_a)

    @pl.when(near0)
    def _():
        add_near_bias(tile_at(0), *buf_a)

    def pair(step, const_bias):
        _, const_bias = stage(step, buf_a, buf_b, const_bias, True)
        _, const_bias = stage(step + 1, buf_b, buf_a, const_bias, True)
        return const_bias

    def run_group(base, size, const_bias):
        inner_far = jnp.bool_(True)
        for i in range(1, size):
            inner_far = jnp.logical_and(inner_far, jnp.logical_not(classify(tile_at(base + i))[0]))

        def branch_free():
            c = const_bias
            for i in range(size):
                cur, nxt = (buf_a, buf_b) if i % 2 == 0 else (buf_b, buf_a)
                _, c = stage(base + i, cur, nxt, c, i == size - 1)
            return c

        def split():
            if size == MIN_STAGE_GROUP:
                return lax.fori_loop(0, size // 2, lambda t, c: pair(base + 2 * t, c), const_bias)
            half = size // 2
            return lax.fori_loop(0, 2, lambda t, c: run_group(base + t * half, half, c), const_bias)

        return lax.cond(inner_far, branch_free, split)

    lax.fori_loop(0, nk // STAGE_GROUP, lambda j, c: run_group(j * STAGE_GROUP, STAGE_GROUP, c), bias0)

    lp = lam_ref[...]
    lam = (jnp.exp(jnp.sum(lp[0:1] * lp[1:2], axis=1, keepdims=True))
           - jnp.exp(jnp.sum(lp[2:3] * lp[3:4], axis=1, keepdims=True)) + lambda_init)
    o1 = acc_scr[0:V_DIM, 0:tq] / acc_scr[V_DIM:V_DIM + 1, 0:tq]
    o2 = acc_scr[0:V_DIM, tq:2 * tq] / acc_scr[V_DIM:V_DIM + 1, tq:2 * tq]
    o = o1 - lam * o2
    o = o * lax.rsqrt(jnp.mean(o * o, axis=0, keepdims=True) + LN_EPS)
    o = o * sg_ref[...] * (1.0 - lambda_init)
    o_ref[...] = jnp.transpose(o).astype(BF16)


def _attention(qt, k, vt, positions, bmin, bmax, order, rel_table, lam_params, subln_g, lambda_init):
    D, S = qt.shape
    heads = D // V_DIM
    tq, tk = TQ, TK
    nk = S // tk
    assert S % tq == 0 and nk % STAGE_GROUP == 0, (S, tq, tk)
    table2 = rel_table.astype(F32) * LOG2_E
    table_rows = jnp.zeros((heads, LANES), F32).at[:, :REL_BUCKETS].set(table2.T)
    kernel = functools.partial(_attn_kernel, nk=nk, lambda_init=lambda_init)
    grid_spec = pltpu.PrefetchScalarGridSpec(
        num_scalar_prefetch=4,
        grid=(heads, S // tq),
        in_specs=[
            pl.BlockSpec((V_DIM, tq), lambda h, i, *_: (h, i)),
            pl.BlockSpec((S, V_DIM), lambda h, i, *_: (0, h)),
            pl.BlockSpec((1, nk, V_DIM + ONES_ROWS, tk), lambda h, i, *_: (h, 0, 0, 0)),
            pl.BlockSpec((1, tq), lambda h, i, *_: (0, i)),
            pl.BlockSpec((nk, 1, tk), lambda h, i, *_: (0, 0, 0)),
            pl.BlockSpec((heads, LANES), lambda h, i, *_: (0, 0)),
            pl.BlockSpec((4, HEAD_DIM), lambda h, i, *_: (0, 0)),
            pl.BlockSpec((V_DIM, 1), lambda h, i, *_: (0, 0)),
        ],
        out_specs=pl.BlockSpec((tq, V_DIM), lambda h, i, *_: (i, h)),
        scratch_shapes=[
            pltpu.VMEM((V_DIM, 2 * tq), BF16),
            pltpu.VMEM((tk, 2 * tq), F32),
            pltpu.VMEM((tk, 2 * tq), F32),
            pltpu.VMEM((1, 2 * tq), F32),
            pltpu.VMEM((1, 2 * tq), F32),
            pltpu.VMEM((1, 2 * tq), F32),
            pltpu.VMEM((V_DIM + ONES_ROWS, 2 * tq), F32),
        ],
    )
    return pl.pallas_call(
        kernel,
        grid_spec=grid_spec,
        out_shape=jax.ShapeDtypeStruct((S, D), BF16),
        compiler_params=_params("parallel", "parallel"),
        name="diff_attention",
    )(bmin, bmax, table2.reshape(-1), order,
      qt, k, vt, positions.reshape(1, S), positions.reshape(nk, 1, tk), table_rows,
      lam_params, subln_g.reshape(V_DIM, 1))


def _proj_ln_kernel(a_ref, x_ref, w_ref, lg_ref, lb_ref, o_ref):
    z = DEEPNORM_ALPHA * x_ref[...] + _dot(a_ref[...], w_ref[...])
    o_ref[...] = _layer_norm(z, lg_ref[...], lb_ref[...])


def _proj_ln(a, x, w, ln_g, ln_b):
    S, D = x.shape
    K = a.shape[1]
    tm = TM_PROJ
    return pl.pallas_call(
        _proj_ln_kernel,
        grid=(S // tm,),
        in_specs=[
            pl.BlockSpec((tm, K), lambda i: (i, 0)),
            pl.BlockSpec((tm, D), lambda i: (i, 0)),
            _const_spec((K, D)),
            _const_spec((1, D)),
            _const_spec((1, D)),
        ],
        out_specs=pl.BlockSpec((tm, D), lambda i: (i, 0)),
        out_shape=jax.ShapeDtypeStruct((S, D), F32),
        compiler_params=_params("parallel"),
        name="attn_out_proj",
    )(a, x, w.astype(BF16), ln_g.reshape(1, D), ln_b.reshape(1, D))


def _attn_layer(x, positions, bmin, bmax, order, rel_table, w_qkv, lam_params, subln_g, w_out,
                lambda_init, ln_g, ln_b):
    qt, k, vt = _qkv(x, w_qkv)
    o = _attention(qt, k, vt, positions, bmin, bmax, order, rel_table, lam_params, subln_g,
                   lambda_init)
    return _proj_ln(o, x, w_out, ln_g, ln_b)


def _ffn_kernel(x_ref, xp_ref, xn_ref, wi_ref, cw_ref, cb_ref, wd_ref, lg_ref, lb_ref,
                o_ref, g_scr):
    i = pl.program_id(0)
    x = x_ref[...]
    tm = x.shape[0]
    halo = xp_ref.shape[0]
    F = wd_ref.shape[0]
    xp = jnp.where(i > 0, xp_ref[...], 0.0)
    xn = jnp.where(i < pl.num_programs(0) - 1, xn_ref[...], 0.0)
    xe = jnp.concatenate([xp, x, xn], axis=0).astype(BF16)
    g_scr[...] = _dot(xe, wi_ref[:, 0:F])
    cw = cw_ref[...]
    gate = (g_scr[halo - 1:halo - 1 + tm, :] * cw[0:1]
            + g_scr[halo:halo + tm, :] * cw[1:2]
            + g_scr[halo + 1:halo + 1 + tm, :] * cw[2:3]
            + cb_ref[...])
    up = _dot(x.astype(BF16), wi_ref[:, F:2 * F])
    hidden = (_gelu(gate) * up).astype(BF16)
    z = DEEPNORM_ALPHA * x + _dot(hidden, wd_ref[...])
    o_ref[...] = _layer_norm(z, lg_ref[...], lb_ref[...])


def _ffn_layer(x, w_in, conv_w, conv_b, w_down, ln_g, ln_b):
    S, D = x.shape
    F = w_down.shape[0]
    tm, halo = TM_FFN, SUBLANES
    blocks_per_tile = tm // halo
    last_halo_block = S // halo - 1
    return pl.pallas_call(
        _ffn_kernel,
        grid=(S // tm,),
        in_specs=[
            pl.BlockSpec((tm, D), lambda i: (i, 0)),
            pl.BlockSpec((halo, D), lambda i: (jnp.maximum(i * blocks_per_tile - 1, 0), 0)),
            pl.BlockSpec((halo, D),
                         lambda i: (jnp.minimum((i + 1) * blocks_per_tile, last_halo_block), 0)),
            _const_spec((D, 2 * F)),
            _const_spec((conv_w.shape[0], F)),
            _const_spec((1, F)),
            _const_spec((F, D)),
            _const_spec((1, D)),
            _const_spec((1, D)),
        ],
        out_specs=pl.BlockSpec((tm, D), lambda i: (i, 0)),
        out_shape=jax.ShapeDtypeStruct((S, D), F32),
        scratch_shapes=[pltpu.VMEM((tm + 2 * halo, F), F32)],
        compiler_params=_params("parallel"),
        name="conv_glu_ffn",
    )(x, x, x, w_in.astype(BF16), conv_w, conv_b.reshape(1, F), w_down.astype(BF16),
      ln_g.reshape(1, D), ln_b.reshape(1, D))


def kernel(x, positions, rel_bias_table, a_w_in, a_norm_g, a_norm_b, a_w_s, a_b_s, a_w_out,
           b_w_qkv, b_lambda, b_subln_g, b_w_out, f_w_in, f_conv_w, f_conv_b, f_w_down,
           ln_g, ln_b):
    B, S, D = x.shape
    outs = []
    for b in range(B):
        xs = x[b]
        pos = positions[b]
        bmin, bmax = _pos_stats(pos)
        order = _tile_order(bmin, bmax, S // TQ, S // TK)
        for i in range(DEPTH):
            j = i // 2
            if i % 2 == 0:
                xs = _gmlp_layer(xs, a_w_in[j], a_norm_g[j], a_norm_b[j], a_w_s[j], a_b_s[j],
                                 a_w_out[j], ln_g[i, 0], ln_b[i, 0])
            else:
                xs = _attn_layer(xs, pos, bmin, bmax, order, rel_bias_table, b_w_qkv[j], b_lambda[j],
                                 b_subln_g[j], b_w_out[j], _lambda_init(i), ln_g[i, 0], ln_b[i, 0])
            xs = _ffn_layer(xs, f_w_in[i], f_conv_w[i], f_conv_b[i], f_w_down[i],
                            ln_g[i, 1], ln_b[i, 1])
        outs.append(xs)
    return jnp.stack(outs)
```

```python
import functools
import math

import jax
import jax.numpy as jnp
from jax import lax
from jax.experimental import pallas as pl
from jax.experimental.pallas import tpu as pltpu

F32 = jnp.float32
BF16 = jnp.bfloat16

DEPTH = 4
A_CHUNK = 128
A_GROUPS = 8
HEAD_DIM = 64
V_DIM = 2 * HEAD_DIM
REL_BUCKETS = 32
REL_FAR = 128
LN_EPS = 1e-5
LOG2_E = math.log2(math.e)
MASKED_LOGIT = -1e30
DEEPNORM_ALPHA = (2 * DEPTH) ** 0.25
REL_CLIP = 2047
F32_MANTISSA_BITS = 23
F32_EXP_BIAS = 127

LANES = 128
SUBLANES = 8
BF16_ROWS = 16
VMEM_BYTES = 64 * 1024 * 1024
VMEM_LIMIT = VMEM_BYTES * 3 // 4

TM_GMLP = 512
TQ = 512
TK = 256
TM_QKV = TK
TM_PROJ = 512
TM_FFN = 512
STAGE_GROUP = 16
MIN_STAGE_GROUP = 2
ONES_ROWS = BF16_ROWS


def _lambda_init(layer_idx):
    return 0.8 - 0.6 * math.exp(-0.3 * layer_idx)


def _gelu(x):
    return 0.5 * x * (1.0 + lax.erf(x * (1.0 / math.sqrt(2.0))))


def _layer_norm(z, g, b):
    mu = jnp.mean(z, axis=-1, keepdims=True)
    zc = z - mu
    var = jnp.mean(zc * zc, axis=-1, keepdims=True)
    return zc * lax.rsqrt(var + LN_EPS) * g + b


def _dot(a, b):
    return jnp.dot(a, b, preferred_element_type=F32)


def _dot_nt(a, b):
    return lax.dot_general(a, b, (((1,), (1,)), ((), ())), preferred_element_type=F32)


def _const_spec(shape):
    nd = len(shape)
    return pl.BlockSpec(shape, lambda *_: (0,) * nd, pipeline_mode=pl.Buffered(1))


def _params(*sem):
    return pltpu.CompilerParams(dimension_semantics=sem, vmem_limit_bytes=VMEM_LIMIT)


def _gmlp_kernel(x_ref, wi_ref, ng_ref, nb_ref, ws_ref, bs_ref, wo_ref, lg_ref, lb_ref, o_ref):
    x = x_ref[...]
    tm = x.shape[0]
    W = wo_ref.shape[0]
    gd = W // A_GROUPS
    hidden = _gelu(_dot(x.astype(BF16), wi_ref[...]))
    v = _layer_norm(hidden[:, W:], ng_ref[...], nb_ref[...]).astype(BF16)
    mixed = []
    for c in range(tm // A_CHUNK):
        row = []
        for g in range(A_GROUPS):
            vc = v[c * A_CHUNK:(c + 1) * A_CHUNK, g * gd:(g + 1) * gd]
            row.append(_dot(ws_ref[g], vc) + bs_ref[g])
        mixed.append(jnp.concatenate(row, axis=1))
    y = (hidden[:, :W] * jnp.concatenate(mixed, axis=0)).astype(BF16)
    z = DEEPNORM_ALPHA * x + _dot(y, wo_ref[...])
    o_ref[...] = _layer_norm(z, lg_ref[...], lb_ref[...])


def _gmlp_layer(x, w_in, norm_g, norm_b, w_s, b_s, w_out, ln_g, ln_b):
    S, D = x.shape
    W = w_out.shape[0]
    tm = TM_GMLP
    return pl.pallas_call(
        _gmlp_kernel,
        grid=(S // tm,),
        in_specs=[
            pl.BlockSpec((tm, D), lambda i: (i, 0)),
            _const_spec((D, 2 * W)),
            _const_spec((1, W)),
            _const_spec((1, W)),
            _const_spec((A_GROUPS, A_CHUNK, A_CHUNK)),
            _const_spec((A_GROUPS, A_CHUNK, 1)),
            _const_spec((W, D)),
            _const_spec((1, D)),
            _const_spec((1, D)),
        ],
        out_specs=pl.BlockSpec((tm, D), lambda i: (i, 0)),
        out_shape=jax.ShapeDtypeStruct((S, D), F32),
        compiler_params=_params("parallel"),
        name="gmlp_layer",
    )(x, w_in.astype(BF16), norm_g.reshape(1, W), norm_b.reshape(1, W), w_s.astype(BF16),
      b_s.reshape(A_GROUPS, A_CHUNK, 1), w_out.astype(BF16), ln_g.reshape(1, D), ln_b.reshape(1, D))


def _pos_stats_kernel(p_ref, mn_ref, mx_ref):
    p = p_ref[...]
    mn_ref[...] = jnp.min(p, axis=1, keepdims=True)
    mx_ref[...] = jnp.max(p, axis=1, keepdims=True)


def _pos_stats(positions):
    nb = positions.shape[0] // LANES
    mn, mx = pl.pallas_call(
        _pos_stats_kernel,
        out_shape=(jax.ShapeDtypeStruct((nb, 1), jnp.int32),) * 2,
        name="pos_stats",
    )(positions.reshape(nb, LANES))
    return mn.reshape(nb), mx.reshape(nb)


def _tile_order_kernel(bmin_ref, bmax_ref, order_ref, near_ref, *, nq, nk, q_blocks, k_blocks):
    n_groups = nk // STAGE_GROUP

    def per_query_tile(qi, carry):
        qmin = bmin_ref[qi * q_blocks]
        qmax = bmax_ref[qi * q_blocks]
        for r in range(1, q_blocks):
            qmin = jnp.minimum(qmin, bmin_ref[qi * q_blocks + r])
            qmax = jnp.maximum(qmax, bmax_ref[qi * q_blocks + r])

        def near(kt):
            kmin = bmin_ref[kt * k_blocks]
            kmax = bmax_ref[kt * k_blocks]
            for r in range(1, k_blocks):
                kmin = jnp.minimum(kmin, bmin_ref[kt * k_blocks + r])
                kmax = jnp.maximum(kmax, bmax_ref[kt * k_blocks + r])
            far = jnp.logical_or(kmin - qmax >= REL_FAR, kmax - qmin <= -REL_FAR)
            return jnp.logical_not(far).astype(jnp.int32)

        def count(kt, total):
            near_ref[kt] = near(kt)
            return total + near_ref[kt]

        n_near = lax.fori_loop(0, nk, count, 0)
        reorder = n_near <= n_groups

        def place(kt, state):
            near_seen, far_slot = state
            is_near = near_ref[kt]
            slot = jnp.where(is_near == 1, near_seen * STAGE_GROUP, far_slot)
            order_ref[qi * nk + jnp.where(reorder, slot, kt)] = kt
            nxt = far_slot + 1
            held = jnp.logical_and(nxt % STAGE_GROUP == 0, nxt // STAGE_GROUP < n_near)
            nxt = jnp.where(held, nxt + 1, nxt)
            return near_seen + is_near, jnp.where(is_near == 1, far_slot, nxt)

        lax.fori_loop(0, nk, place, (0, jnp.where(n_near > 0, 1, 0)))
        return carry

    lax.fori_loop(0, nq, per_query_tile, 0)


def _tile_order(bmin, bmax, nq, nk):
    smem = pl.BlockSpec(memory_space=pltpu.SMEM)
    kernel = functools.partial(_tile_order_kernel, nq=nq, nk=nk,
                               q_blocks=TQ // LANES, k_blocks=TK // LANES)
    return pl.pallas_call(
        kernel,
        in_specs=[smem, smem],
        out_specs=smem,
        out_shape=jax.ShapeDtypeStruct((nq * nk,), jnp.int32),
        scratch_shapes=[pltpu.SMEM((nk,), jnp.int32)],
        name="tile_order",
    )(bmin, bmax)


def _qkv_kernel(x_ref, wqt_ref, wk_ref, wvt_ref, qt_ref, k_ref, vt_ref):
    xb = x_ref[...].astype(BF16)
    tm = xb.shape[0]
    qt_ref[...] = _dot_nt(wqt_ref[...], xb).astype(BF16)
    k_ref[...] = _dot(xb, wk_ref[...]).astype(BF16)
    vt = _dot_nt(wvt_ref[...], xb).astype(BF16)
    heads = vt.shape[0] // V_DIM
    vt_ref[:, 0, 0:V_DIM, :] = vt.reshape(heads, V_DIM, tm)
    vt_ref[:, 0, V_DIM:, :] = jnp.ones((heads, ONES_ROWS, tm), BF16)


def _qkv(x, w_qkv):
    S, D = x.shape
    tm = TM_QKV
    heads = D // V_DIM
    wqt = (w_qkv[:, :D] * (HEAD_DIM ** -0.5 * LOG2_E)).T.astype(BF16)
    wk = w_qkv[:, D:2 * D].astype(BF16)
    wvt = w_qkv[:, 2 * D:].T.astype(BF16)
    return pl.pallas_call(
        _qkv_kernel,
        grid=(S // tm,),
        in_specs=[
            pl.BlockSpec((tm, D), lambda i: (i, 0)),
            _const_spec((D, D)),
            _const_spec((D, D)),
            _const_spec((D, D)),
        ],
        out_specs=[
            pl.BlockSpec((D, tm), lambda i: (0, i)),
            pl.BlockSpec((tm, D), lambda i: (i, 0)),
            pl.BlockSpec((heads, 1, V_DIM + ONES_ROWS, tm), lambda i: (0, i, 0, 0)),
        ],
        out_shape=[
            jax.ShapeDtypeStruct((D, S), BF16),
            jax.ShapeDtypeStruct((S, D), BF16),
            jax.ShapeDtypeStruct((heads, S // tm, V_DIM + ONES_ROWS, tm), BF16),
        ],
        compiler_params=_params("parallel"),
        name="qkv_proj",
    )(x, wqt, wk, wvt)


def _bias_tile(posk_row, posq_row, table_row):
    tk = posk_row.shape[1]
    tq = posq_row.shape[1]
    table_sq = jnp.broadcast_to(table_row, (LANES, LANES))
    rows = []
    for c in range(tk // LANES):
        pk = posk_row[:, c * LANES:(c + 1) * LANES]
        pk_col = jnp.transpose(jnp.broadcast_to(pk, (LANES, LANES)))
        cols = []
        for d in range(tq // LANES):
            rel = pk_col - posq_row[:, d * LANES:(d + 1) * LANES]
            n = jnp.abs(rel)
            nsq = jnp.square(jnp.minimum(n, REL_CLIP)).astype(F32)
            log2_nsq = (lax.bitcast_convert_type(nsq, jnp.int32) >> F32_MANTISSA_BITS) - F32_EXP_BIAS
            large = jnp.minimum(log2_nsq + 2, REL_BUCKETS // 2 - 1)
            bucket = jnp.where(n < REL_BUCKETS // 4, n, large)
            bucket = bucket + jnp.where(rel > 0, REL_BUCKETS // 2, 0)
            cols.append(jnp.take_along_axis(table_sq, bucket, axis=1))
        rows.append(jnp.concatenate(cols, axis=1))
    return jnp.concatenate(rows, axis=0)


def _attn_kernel(bmin_ref, bmax_ref, tbl_ref, order_ref,
                 qt_ref, k_ref, vt_ref, posq_ref, posk_ref, tblv_ref, lam_ref, sg_ref,
                 o_ref, qcat, s_a, s_b, smax_a, smax_b, m_scr, acc_scr, *, nk, lambda_init):
    h = pl.program_id(0)
    qi = pl.program_id(1)
    tq = qt_ref.shape[1]
    tk = vt_ref.shape[3]
    heads = pl.num_programs(0)

    q = qt_ref[...]
    row = lax.broadcasted_iota(jnp.int32, q.shape, 0)
    zero = jnp.zeros_like(q)
    qcat[:, 0:tq] = jnp.where(row < HEAD_DIM, q, zero)
    qcat[:, tq:2 * tq] = jnp.where(row >= HEAD_DIM, q, zero)

    qmin = bmin_ref[qi * (tq // LANES)]
    qmax = bmax_ref[qi * (tq // LANES)]
    for r in range(1, tq // LANES):
        qmin = jnp.minimum(qmin, bmin_ref[qi * (tq // LANES) + r])
        qmax = jnp.maximum(qmax, bmax_ref[qi * (tq // LANES) + r])
    bias_before = tbl_ref[(REL_BUCKETS // 2 - 1) * heads + h]
    bias_after = tbl_ref[(REL_BUCKETS - 1) * heads + h]

    acc_scr[...] = jnp.zeros_like(acc_scr)
    m_scr[...] = jnp.full(m_scr.shape, MASKED_LOGIT, F32)

    def classify(kt):
        kmin = bmin_ref[kt * (tk // LANES)]
        kmax = bmax_ref[kt * (tk // LANES)]
        for r in range(1, tk // LANES):
            kmin = jnp.minimum(kmin, bmin_ref[kt * (tk // LANES) + r])
            kmax = jnp.maximum(kmax, bmax_ref[kt * (tk // LANES) + r])
        all_after = kmin - qmax >= REL_FAR
        all_before = kmax - qmin <= -REL_FAR
        near = jnp.logical_not(jnp.logical_or(all_after, all_before))
        const_bias = jnp.where(all_after, bias_after, jnp.where(all_before, bias_before, 0.0))
        return near, const_bias

    def logits(kt, s_ref, smax_ref):
        kk = k_ref[pl.ds(pl.multiple_of(kt * tk, tk), tk), :]
        s = _dot(kk, qcat[...])
        s_ref[...] = s
        smax_ref[...] = jnp.max(s, axis=0, keepdims=True)

    def add_near_bias(kt, s_ref, smax_ref):
        bias = _bias_tile(posk_ref[kt], posq_ref[...], tblv_ref[pl.ds(h, 1), :])
        s = s_ref[...] + jnp.concatenate([bias, bias], axis=1)
        s_ref[...] = s
        smax_ref[...] = jnp.max(s, axis=0, keepdims=True)

    def softmax_update(kt, s_ref, smax_ref, const_bias):
        m_old = m_scr[...]
        m_new = jnp.maximum(m_old, smax_ref[...] + const_bias)
        p = jnp.exp2(s_ref[...] - (m_new - const_bias)).astype(BF16)
        acc_scr[...] = jnp.exp2(m_old - m_new) * acc_scr[...] + _dot(vt_ref[0, kt], p)
        m_scr[...] = m_new

    def tile_at(step):
        return order_ref[qi * nk + step]

    def stage(step, cur, nxt, const_bias, fix_next):
        kt = tile_at(step)
        kn = tile_at(jnp.minimum(step + 1, nk - 1))
        logits(kn, *nxt)
        softmax_update(kt, *cur, const_bias)
        near_next, bias_next = classify(kn)
        if fix_next:
            @pl.when(near_next)
            def _():
                add_near_bias(kn, *nxt)
        return near_next, bias_next

    buf_a = (s_a, smax_a)
    buf_b = (s_b, smax_b)
    near0, bias0 = classify(tile_at(0))
    logits(tile_at(0), *buf_a)

    @pl.when(near0)
    def _():
        add_near_bias(tile_at(0), *buf_a)

    def pair(step, const_bias):
        _, const_bias = stage(step, buf_a, buf_b, const_bias, True)
        _, const_bias = stage(step + 1, buf_b, buf_a, const_bias, True)
        return const_bias

    def run_group(base, size, const_bias):
        inner_far = jnp.bool_(True)
        for i in range(1, size):
            inner_far = jnp.logical_and(inner_far, jnp.logical_not(classify(tile_at(base + i))[0]))

        def branch_free():
            c = const_bias
            for i in range(size):
                cur, nxt = (buf_a, buf_b) if i % 2 == 0 else (buf_b, buf_a)
                _, c = stage(base + i, cur, nxt, c, i == size - 1)
            return c

        def split():
            if size == MIN_STAGE_GROUP:
                return lax.fori_loop(0, size // 2, lambda t, c: pair(base + 2 * t, c), const_bias)
            half = size // 2
            return lax.fori_loop(0, 2, lambda t, c: run_group(base + t * half, half, c), const_bias)

        return lax.cond(inner_far, branch_free, split)

    lax.fori_loop(0, nk // STAGE_GROUP, lambda j, c: run_group(j * STAGE_GROUP, STAGE_GROUP, c), bias0)

    lp = lam_ref[...]
    lam = (jnp.exp(jnp.sum(lp[0:1] * lp[1:2], axis=1, keepdims=True))
           - jnp.exp(jnp.sum(lp[2:3] * lp[3:4], axis=1, keepdims=True)) + lambda_init)
    o1 = acc_scr[0:V_DIM, 0:tq] / acc_scr[V_DIM:V_DIM + 1, 0:tq]
    o2 = acc_scr[0:V_DIM, tq:2 * tq] / acc_scr[V_DIM:V_DIM + 1, tq:2 * tq]
    o = o1 - lam * o2
    o = o * lax.rsqrt(jnp.mean(o * o, axis=0, keepdims=True) + LN_EPS)
    o = o * sg_ref[...] * (1.0 - lambda_init)
    o_ref[...] = jnp.transpose(o).astype(BF16)


def _attention(qt, k, vt, positions, bmin, bmax, order, rel_table, lam_params, subln_g, lambda_init):
    D, S = qt.shape
    heads = D // V_DIM
    tq, tk = TQ, TK
    nk = S // tk
    assert S % tq == 0 and nk % STAGE_GROUP == 0, (S, tq, tk)
    table2 = rel_table.astype(F32) * LOG2_E
    table_rows = jnp.zeros((heads, LANES), F32).at[:, :REL_BUCKETS].set(table2.T)
    kernel = functools.partial(_attn_kernel, nk=nk, lambda_init=lambda_init)
    grid_spec = pltpu.PrefetchScalarGridSpec(
        num_scalar_prefetch=4,
        grid=(heads, S // tq),
        in_specs=[
            pl.BlockSpec((V_DIM, tq), lambda h, i, *_: (h, i)),
            pl.BlockSpec((S, V_DIM), lambda h, i, *_: (0, h)),
            pl.BlockSpec((1, nk, V_DIM + ONES_ROWS, tk), lambda h, i, *_: (h, 0, 0, 0)),
            pl.BlockSpec((1, tq), lambda h, i, *_: (0, i)),
            pl.BlockSpec((nk, 1, tk), lambda h, i, *_: (0, 0, 0)),
            pl.BlockSpec((heads, LANES), lambda h, i, *_: (0, 0)),
            pl.BlockSpec((4, HEAD_DIM), lambda h, i, *_: (0, 0)),
            pl.BlockSpec((V_DIM, 1), lambda h, i, *_: (0, 0)),
        ],
        out_specs=pl.BlockSpec((tq, V_DIM), lambda h, i, *_: (i, h)),
        scratch_shapes=[
            pltpu.VMEM((V_DIM, 2 * tq), BF16),
            pltpu.VMEM((tk, 2 * tq), F32),
            pltpu.VMEM((tk, 2 * tq), F32),
            pltpu.VMEM((1, 2 * tq), F32),
            pltpu.VMEM((1, 2 * tq), F32),
            pltpu.VMEM((1, 2 * tq), F32),
            pltpu.VMEM((V_DIM + ONES_ROWS, 2 * tq), F32),
        ],
    )
    return pl.pallas_call(
        kernel,
        grid_spec=grid_spec,
        out_shape=jax.ShapeDtypeStruct((S, D), BF16),
        compiler_params=_params("parallel", "parallel"),
        name="diff_attention",
    )(bmin, bmax, table2.reshape(-1), order,
      qt, k, vt, positions.reshape(1, S), positions.reshape(nk, 1, tk), table_rows,
      lam_params, subln_g.reshape(V_DIM, 1))


def _proj_ln_kernel(a_ref, x_ref, w_ref, lg_ref, lb_ref, o_ref):
    z = DEEPNORM_ALPHA * x_ref[...] + _dot(a_ref[...], w_ref[...])
    o_ref[...] = _layer_norm(z, lg_ref[...], lb_ref[...])


def _proj_ln(a, x, w, ln_g, ln_b):
    S, D = x.shape
    K = a.shape[1]
    tm = TM_PROJ
    return pl.pallas_call(
        _proj_ln_kernel,
        grid=(S // tm,),
        in_specs=[
            pl.BlockSpec((tm, K), lambda i: (i, 0)),
            pl.BlockSpec((tm, D), lambda i: (i, 0)),
            _const_spec((K, D)),
            _const_spec((1, D)),
            _const_spec((1, D)),
        ],
        out_specs=pl.BlockSpec((tm, D), lambda i: (i, 0)),
        out_shape=jax.ShapeDtypeStruct((S, D), F32),
        compiler_params=_params("parallel"),
        name="attn_out_proj",
    )(a, x, w.astype(BF16), ln_g.reshape(1, D), ln_b.reshape(1, D))


def _attn_layer(x, positions, bmin, bmax, order, rel_table, w_qkv, lam_params, subln_g, w_out,
                lambda_init, ln_g, ln_b):
    qt, k, vt = _qkv(x, w_qkv)
    o = _attention(qt, k, vt, positions, bmin, bmax, order, rel_table, lam_params, subln_g,
                   lambda_init)
    return _proj_ln(o, x, w_out, ln_g, ln_b)


def _ffn_kernel(x_ref, xp_ref, xn_ref, wi_ref, cw_ref, cb_ref, wd_ref, lg_ref, lb_ref,
                o_ref, g_scr):
    i = pl.program_id(0)
    x = x_ref[...]
    tm = x.shape[0]
    halo = xp_ref.shape[0]
    F = wd_ref.shape[0]
    xp = jnp.where(i > 0, xp_ref[...], 0.0)
    xn = jnp.where(i < pl.num_programs(0) - 1, xn_ref[...], 0.0)
    xe = jnp.concatenate([xp, x, xn], axis=0).astype(BF16)
    g_scr[...] = _dot(xe, wi_ref[:, 0:F])
    cw = cw_ref[...]
    gate = (g_scr[halo - 1:halo - 1 + tm, :] * cw[0:1]
            + g_scr[halo:halo + tm, :] * cw[1:2]
            + g_scr[halo + 1:halo + 1 + tm, :] * cw[2:3]
            + cb_ref[...])
    up = _dot(x.astype(BF16), wi_ref[:, F:2 * F])
    hidden = (_gelu(gate) * up).astype(BF16)
    z = DEEPNORM_ALPHA * x + _dot(hidden, wd_ref[...])
    o_ref[...] = _layer_norm(z, lg_ref[...], lb_ref[...])


def _ffn_layer(x, w_in, conv_w, conv_b, w_down, ln_g, ln_b):
    S, D = x.shape
    F = w_down.shape[0]
    tm, halo = TM_FFN, SUBLANES
    blocks_per_tile = tm // halo
    last_halo_block = S // halo - 1
    return pl.pallas_call(
        _ffn_kernel,
        grid=(S // tm,),
        in_specs=[
            pl.BlockSpec((tm, D), lambda i: (i, 0)),
            pl.BlockSpec((halo, D), lambda i: (jnp.maximum(i * blocks_per_tile - 1, 0), 0)),
            pl.BlockSpec((halo, D),
                         lambda i: (jnp.minimum((i + 1) * blocks_per_tile, last_halo_block), 0)),
            _const_spec((D, 2 * F)),
            _const_spec((conv_w.shape[0], F)),
            _const_spec((1, F)),
            _const_spec((F, D)),
            _const_spec((1, D)),
            _const_spec((1, D)),
        ],
        out_specs=pl.BlockSpec((tm, D), lambda i: (i, 0)),
        out_shape=jax.ShapeDtypeStruct((S, D), F32),
        scratch_shapes=[pltpu.VMEM((tm + 2 * halo, F), F32)],
        compiler_params=_params("parallel"),
        name="conv_glu_ffn",
    )(x, x, x, w_in.astype(BF16), conv_w, conv_b.reshape(1, F), w_down.astype(BF16),
      ln_g.reshape(1, D), ln_b.reshape(1, D))


def kernel(x, positions, rel_bias_table, a_w_in, a_norm_g, a_norm_b, a_w_s, a_b_s, a_w_out,
           b_w_qkv, b_lambda, b_subln_g, b_w_out, f_w_in, f_conv_w, f_conv_b, f_w_down,
           ln_g, ln_b):
    B, S, D = x.shape
    outs = []
    for b in range(B):
        xs = x[b]
        pos = positions[b]
        bmin, bmax = _pos_stats(pos)
        order = _tile_order(bmin, bmax, S // TQ, S // TK)
        for i in range(DEPTH):
            j = i // 2
            if i % 2 == 0:
                xs = _gmlp_layer(xs, a_w_in[j], a_norm_g[j], a_norm_b[j], a_w_s[j], a_b_s[j],
                                 a_w_out[j], ln_g[i, 0], ln_b[i, 0])
            else:
                xs = _attn_layer(xs, pos, bmin, bmax, order, rel_bias_table, b_w_qkv[j], b_lambda[j],
                                 b_subln_g[j], b_w_out[j], _lambda_init(i), ln_g[i, 0], ln_b[i, 0])
            xs = _ffn_layer(xs, f_w_in[i], f_conv_w[i], f_conv_b[i], f_w_down[i],
                            ln_g[i, 1], ln_b[i, 1])
        outs.append(xs)
    return jnp.stack(outs)
```

```python
import functools
import math

import jax
import jax.numpy as jnp
from jax import lax
from jax.experimental import pallas as pl
from jax.experimental.pallas import tpu as pltpu

F32 = jnp.float32
BF16 = jnp.bfloat16

DEPTH = 4
A_CHUNK = 128
A_GROUPS = 8
HEAD_DIM = 64
V_DIM = 2 * HEAD_DIM
REL_BUCKETS = 32
REL_FAR = 128
LN_EPS = 1e-5
LOG2_E = math.log2(math.e)
MASKED_LOGIT = -1e30
DEEPNORM_ALPHA = (2 * DEPTH) ** 0.25
REL_CLIP = 2047
F32_MANTISSA_BITS = 23
F32_EXP_BIAS = 127

LANES = 128
SUBLANES = 8
BF16_ROWS = 16
VMEM_BYTES = 64 * 1024 * 1024
VMEM_LIMIT = VMEM_BYTES * 3 // 4

TM_GMLP = 512
TQ = 512
TK = 256
TM_QKV = TK
TM_PROJ = 512
TM_FFN = 512
STAGE_GROUP = 16
MIN_STAGE_GROUP = 2
ONES_ROWS = BF16_ROWS


def _lambda_init(layer_idx):
    return 0.8 - 0.6 * math.exp(-0.3 * layer_idx)


def _gelu(x):
    return 0.5 * x * (1.0 + lax.erf(x * (1.0 / math.sqrt(2.0))))


def _layer_norm(z, g, b):
    mu = jnp.mean(z, axis=-1, keepdims=True)
    zc = z - mu
    var = jnp.mean(zc * zc, axis=-1, keepdims=True)
    return zc * lax.rsqrt(var + LN_EPS) * g + b


def _dot(a, b):
    return jnp.dot(a, b, preferred_element_type=F32)


def _dot_nt(a, b):
    return lax.dot_general(a, b, (((1,), (1,)), ((), ())), preferred_element_type=F32)


def _const_spec(shape):
    nd = len(shape)
    return pl.BlockSpec(shape, lambda *_: (0,) * nd, pipeline_mode=pl.Buffered(1))


def _params(*sem):
    return pltpu.CompilerParams(dimension_semantics=sem, vmem_limit_bytes=VMEM_LIMIT)


def _gmlp_kernel(x_ref, wi_ref, ng_ref, nb_ref, ws_ref, bs_ref, wo_ref, lg_ref, lb_ref, o_ref):
    x = x_ref[...]
    tm = x.shape[0]
    W = wo_ref.shape[0]
    gd = W // A_GROUPS
    hidden = _gelu(_dot(x.astype(BF16), wi_ref[...]))
    v = _layer_norm(hidden[:, W:], ng_ref[...], nb_ref[...]).astype(BF16)
    mixed = []
    for c in range(tm // A_CHUNK):
        row = []
        for g in range(A_GROUPS):
            vc = v[c * A_CHUNK:(c + 1) * A_CHUNK, g * gd:(g + 1) * gd]
            row.append(_dot(ws_ref[g], vc) + bs_ref[g])
        mixed.append(jnp.concatenate(row, axis=1))
    y = (hidden[:, :W] * jnp.concatenate(mixed, axis=0)).astype(BF16)
    z = DEEPNORM_ALPHA * x + _dot(y, wo_ref[...])
    o_ref[...] = _layer_norm(z, lg_ref[...], lb_ref[...])


def _gmlp_layer(x, w_in, norm_g, norm_b, w_s, b_s, w_out, ln_g, ln_b):
    S, D = x.shape
    W = w_out.shape[0]
    tm = TM_GMLP
    return pl.pallas_call(
        _gmlp_kernel,
        grid=(S // tm,),
        in_specs=[
            pl.BlockSpec((tm, D), lambda i: (i, 0)),
            _const_spec((D, 2 * W)),
            _const_spec((1, W)),
            _const_spec((1, W)),
            _const_spec((A_GROUPS, A_CHUNK, A_CHUNK)),
            _const_spec((A_GROUPS, A_CHUNK, 1)),
            _const_spec((W, D)),
            _const_spec((1, D)),
            _const_spec((1, D)),
        ],
        out_specs=pl.BlockSpec((tm, D), lambda i: (i, 0)),
        out_shape=jax.ShapeDtypeStruct((S, D), F32),
        compiler_params=_params("parallel"),
        name="gmlp_layer",
    )(x, w_in.astype(BF16), norm_g.reshape(1, W), norm_b.reshape(1, W), w_s.astype(BF16),
      b_s.reshape(A_GROUPS, A_CHUNK, 1), w_out.astype(BF16), ln_g.reshape(1, D), ln_b.reshape(1, D))


def _pos_stats_kernel(p_ref, mn_ref, mx_ref):
    p = p_ref[...]
    mn_ref[...] = jnp.min(p, axis=1, keepdims=True)
    mx_ref[...] = jnp.max(p, axis=1, keepdims=True)


def _pos_stats(positions):
    nb = positions.shape[0] // LANES
    mn, mx = pl.pallas_call(
        _pos_stats_kernel,
        out_shape=(jax.ShapeDtypeStruct((nb, 1), jnp.int32),) * 2,
        name="pos_stats",
    )(positions.reshape(nb, LANES))
    return mn.reshape(nb), mx.reshape(nb)


def _tile_order_kernel(bmin_ref, bmax_ref, order_ref, near_ref, *, nq, nk, q_blocks, k_blocks):
    n_groups = nk // STAGE_GROUP

    def per_query_tile(qi, carry):
        qmin = bmin_ref[qi * q_blocks]
        qmax = bmax_ref[qi * q_blocks]
        for r in range(1, q_blocks):
            qmin = jnp.minimum(qmin, bmin_ref[qi * q_blocks + r])
            qmax = jnp.maximum(qmax, bmax_ref[qi * q_blocks + r])

        def near(kt):
            kmin = bmin_ref[kt * k_blocks]
            kmax = bmax_ref[kt * k_blocks]
            for r in range(1, k_blocks):
                kmin = jnp.minimum(kmin, bmin_ref[kt * k_blocks + r])
                kmax = jnp.maximum(kmax, bmax_ref[kt * k_blocks + r])
            far = jnp.logical_or(kmin - qmax >= REL_FAR, kmax - qmin <= -REL_FAR)
            return jnp.logical_not(far).astype(jnp.int32)

        def count(kt, total):
            near_ref[kt] = near(kt)
            return total + near_ref[kt]

        n_near = lax.fori_loop(0, nk, count, 0)
        reorder = n_near <= n_groups

        def place(kt, state):
            near_seen, far_slot = state
            is_near = near_ref[kt]
            slot = jnp.where(is_near == 1, near_seen * STAGE_GROUP, far_slot)
            order_ref[qi * nk + jnp.where(reorder, slot, kt)] = kt
            nxt = far_slot + 1
            held = jnp.logical_and(nxt % STAGE_GROUP == 0, nxt // STAGE_GROUP < n_near)
            nxt = jnp.where(held, nxt + 1, nxt)
            return near_seen + is_near, jnp.where(is_near == 1, far_slot, nxt)

        lax.fori_loop(0, nk, place, (0, jnp.where(n_near > 0, 1, 0)))
        return carry

    lax.fori_loop(0, nq, per_query_tile, 0)


def _tile_order(bmin, bmax, nq, nk):
    smem = pl.BlockSpec(memory_space=pltpu.SMEM)
    kernel = functools.partial(_tile_order_kernel, nq=nq, nk=nk,
                               q_blocks=TQ // LANES, k_blocks=TK // LANES)
    return pl.pallas_call(
        kernel,
        in_specs=[smem, smem],
        out_specs=smem,
        out_shape=jax.ShapeDtypeStruct((nq * nk,), jnp.int32),
        scratch_shapes=[pltpu.SMEM((nk,), jnp.int32)],
        name="tile_order",
    )(bmin, bmax)


def _qkv_kernel(x_ref, wqt_ref, wk_ref, wvt_ref, qt_ref, k_ref, vt_ref):
    xb = x_ref[...].astype(BF16)
    tm = xb.shape[0]
    qt_ref[...] = _dot_nt(wqt_ref[...], xb).astype(BF16)
    k_ref[...] = _dot(xb, wk_ref[...]).astype(BF16)
    vt = _dot_nt(wvt_ref[...], xb).astype(BF16)
    heads = vt.shape[0] // V_DIM
    vt_ref[:, 0, 0:V_DIM, :] = vt.reshape(heads, V_DIM, tm)
    vt_ref[:, 0, V_DIM:, :] = jnp.ones((heads, ONES_ROWS, tm), BF16)


def _qkv(x, w_qkv):
    S, D = x.shape
    tm = TM_QKV
    heads = D // V_DIM
    wqt = (w_qkv[:, :D] * (HEAD_DIM ** -0.5 * LOG2_E)).T.astype(BF16)
    wk = w_qkv[:, D:2 * D].astype(BF16)
    wvt = w_qkv[:, 2 * D:].T.astype(BF16)
    return pl.pallas_call(
        _qkv_kernel,
        grid=(S // tm,),
        in_specs=[
            pl.BlockSpec((tm, D), lambda i: (i, 0)),
            _const_spec((D, D)),
            _const_spec((D, D)),
            _const_spec((D, D)),
        ],
        out_specs=[
            pl.BlockSpec((D, tm), lambda i: (0, i)),
            pl.BlockSpec((tm, D), lambda i: (i, 0)),
            pl.BlockSpec((heads, 1, V_DIM + ONES_ROWS, tm), lambda i: (0, i, 0, 0)),
        ],
        out_shape=[
            jax.ShapeDtypeStruct((D, S), BF16),
            jax.ShapeDtypeStruct((S, D), BF16),
            jax.ShapeDtypeStruct((heads, S // tm, V_DIM + ONES_ROWS, tm), BF16),
        ],
        compiler_params=_params("parallel"),
        name="qkv_proj",
    )(x, wqt, wk, wvt)


def _bias_tile(posk_row, posq_row, table_row):
    tk = posk_row.shape[1]
    tq = posq_row.shape[1]
    table_sq = jnp.broadcast_to(table_row, (LANES, LANES))
    rows = []
    for c in range(tk // LANES):
        pk = posk_row[:, c * LANES:(c + 1) * LANES]
        pk_col = jnp.transpose(jnp.broadcast_to(pk, (LANES, LANES)))
        cols = []
        for d in range(tq // LANES):
            rel = pk_col - posq_row[:, d * LANES:(d + 1) * LANES]
            n = jnp.abs(rel)
            nsq = jnp.square(jnp.minimum(n, REL_CLIP)).astype(F32)
            log2_nsq = (lax.bitcast_convert_type(nsq, jnp.int32) >> F32_MANTISSA_BITS) - F32_EXP_BIAS
            large = jnp.minimum(log2_nsq + 2, REL_BUCKETS // 2 - 1)
            bucket = jnp.where(n < REL_BUCKETS // 4, n, large)
            bucket = bucket + jnp.where(rel > 0, REL_BUCKETS // 2, 0)
            cols.append(jnp.take_along_axis(table_sq, bucket, axis=1))
        rows.append(jnp.concatenate(cols, axis=1))
    return jnp.concatenate(rows, axis=0)


def _attn_kernel(bmin_ref, bmax_ref, tbl_ref, order_ref,
                 qt_ref, k_ref, vt_ref, posq_ref, posk_ref, tblv_ref, lam_ref, sg_ref,
                 o_ref, qcat, s_a, s_b, smax_a, smax_b, m_scr, acc_scr, bias_scr, *, nk, lambda_init):
    h = pl.program_id(0)
    qi = pl.program_id(1)
    tq = qt_ref.shape[1]
    tk = vt_ref.shape[3]
    heads = pl.num_programs(0)

    q = qt_ref[...]
    row = lax.broadcasted_iota(jnp.int32, q.shape, 0)
    zero = jnp.zeros_like(q)
    qcat[:, 0:tq] = jnp.where(row < HEAD_DIM, q, zero)
    qcat[:, tq:2 * tq] = jnp.where(row >= HEAD_DIM, q, zero)

    qmin = bmin_ref[qi * (tq // LANES)]
    qmax = bmax_ref[qi * (tq // LANES)]
    for r in range(1, tq // LANES):
        qmin = jnp.minimum(qmin, bmin_ref[qi * (tq // LANES) + r])
        qmax = jnp.maximum(qmax, bmax_ref[qi * (tq // LANES) + r])
    bias_before = tbl_ref[(REL_BUCKETS // 2 - 1) * heads + h]
    bias_after = tbl_ref[(REL_BUCKETS - 1) * heads + h]

    acc_scr[...] = jnp.zeros_like(acc_scr)
    m_scr[...] = jnp.full(m_scr.shape, MASKED_LOGIT, F32)

    def classify(kt):
        kmin = bmin_ref[kt * (tk // LANES)]
        kmax = bmax_ref[kt * (tk // LANES)]
        for r in range(1, tk // LANES):
            kmin = jnp.minimum(kmin, bmin_ref[kt * (tk // LANES) + r])
            kmax = jnp.maximum(kmax, bmax_ref[kt * (tk // LANES) + r])
        all_after = kmin - qmax >= REL_FAR
        all_before = kmax - qmin <= -REL_FAR
        near = jnp.logical_not(jnp.logical_or(all_after, all_before))
        const_bias = jnp.where(all_after, bias_after, jnp.where(all_before, bias_before, 0.0))
        return near, const_bias

    def logits(kt, s_ref, smax_ref):
        kk = k_ref[pl.ds(pl.multiple_of(kt * tk, tk), tk), :]
        s = _dot(kk, qcat[...])
        s_ref[...] = s
        smax_ref[...] = jnp.max(s, axis=0, keepdims=True)

    def near_bias(kt):
        return _bias_tile(posk_ref[kt], posq_ref[...], tblv_ref[pl.ds(h, 1), :])

    def add_bias(bias, s_ref, smax_ref):
        s = s_ref[...] + jnp.concatenate([bias, bias], axis=1)
        s_ref[...] = s
        smax_ref[...] = jnp.max(s, axis=0, keepdims=True)

    def add_near_bias(kt, s_ref, smax_ref):
        add_bias(near_bias(kt), s_ref, smax_ref)

    def softmax_update(kt, s_ref, smax_ref, const_bias):
        m_old = m_scr[...]
        m_new = jnp.maximum(m_old, smax_ref[...] + const_bias)
        p = jnp.exp2(s_ref[...] - (m_new - const_bias)).astype(BF16)
        acc_scr[...] = jnp.exp2(m_old - m_new) * acc_scr[...] + _dot(vt_ref[0, kt], p)
        m_scr[...] = m_new

    def tile_at(step):
        return order_ref[qi * nk + step]

    def stage(step, cur, nxt, const_bias, fix_next, eager_bias=False):
        kt = tile_at(step)
        kn = tile_at(jnp.minimum(step + 1, nk - 1))
        if eager_bias:
            bias_scr[...] = near_bias(kn)
        logits(kn, *nxt)
        softmax_update(kt, *cur, const_bias)
        near_next, bias_next = classify(kn)
        if fix_next:
            @pl.when(near_next)
            def _():
                if eager_bias:
                    add_bias(bias_scr[...], *nxt)
                else:
                    add_near_bias(kn, *nxt)
        return near_next, bias_next

    buf_a = (s_a, smax_a)
    buf_b = (s_b, smax_b)
    near0, bias0 = classify(tile_at(0))
    logits(tile_at(0), *buf_a)

    @pl.when(near0)
    def _():
        add_near_bias(tile_at(0), *buf_a)

    def pair(step, const_bias):
        _, const_bias = stage(step, buf_a, buf_b, const_bias, True)
        _, const_bias = stage(step + 1, buf_b, buf_a, const_bias, True)
        return const_bias

    def run_group(base, size, const_bias):
        inner_far = jnp.bool_(True)
        for i in range(1, size):
            inner_far = jnp.logical_and(inner_far, jnp.logical_not(classify(tile_at(base + i))[0]))

        def branch_free():
            c = const_bias
            for i in range(size):
                cur, nxt = (buf_a, buf_b) if i % 2 == 0 else (buf_b, buf_a)
                last = i == size - 1
                _, c = stage(base + i, cur, nxt, c, last, eager_bias=last)
            return c

        def split():
            if size == MIN_STAGE_GROUP:
                return lax.fori_loop(0, size // 2, lambda t, c: pair(base + 2 * t, c), const_bias)
            half = size // 2
            return lax.fori_loop(0, 2, lambda t, c: run_group(base + t * half, half, c), const_bias)

        return lax.cond(inner_far, branch_free, split)

    lax.fori_loop(0, nk // STAGE_GROUP, lambda j, c: run_group(j * STAGE_GROUP, STAGE_GROUP, c), bias0)

    lp = lam_ref[...]
    lam = (jnp.exp(jnp.sum(lp[0:1] * lp[1:2], axis=1, keepdims=True))
           - jnp.exp(jnp.sum(lp[2:3] * lp[3:4], axis=1, keepdims=True)) + lambda_init)
    o1 = acc_scr[0:V_DIM, 0:tq] / acc_scr[V_DIM:V_DIM + 1, 0:tq]
    o2 = acc_scr[0:V_DIM, tq:2 * tq] / acc_scr[V_DIM:V_DIM + 1, tq:2 * tq]
    o = o1 - lam * o2
    o = o * lax.rsqrt(jnp.mean(o * o, axis=0, keepdims=True) + LN_EPS)
    o = o * sg_ref[...] * (1.0 - lambda_init)
    o_ref[...] = jnp.transpose(o).astype(BF16)


def _attention(qt, k, vt, positions, bmin, bmax, order, rel_table, lam_params, subln_g, lambda_init):
    D, S = qt.shape
    heads = D // V_DIM
    tq, tk = TQ, TK
    nk = S // tk
    assert S % tq == 0 and nk % STAGE_GROUP == 0, (S, tq, tk)
    table2 = rel_table.astype(F32) * LOG2_E
    table_rows = jnp.zeros((heads, LANES), F32).at[:, :REL_BUCKETS].set(table2.T)
    kernel = functools.partial(_attn_kernel, nk=nk, lambda_init=lambda_init)
    grid_spec = pltpu.PrefetchScalarGridSpec(
        num_scalar_prefetch=4,
        grid=(heads, S // tq),
        in_specs=[
            pl.BlockSpec((V_DIM, tq), lambda h, i, *_: (h, i)),
            pl.BlockSpec((S, V_DIM), lambda h, i, *_: (0, h)),
            pl.BlockSpec((1, nk, V_DIM + ONES_ROWS, tk), lambda h, i, *_: (h, 0, 0, 0)),
            pl.BlockSpec((1, tq), lambda h, i, *_: (0, i)),
            pl.BlockSpec((nk, 1, tk), lambda h, i, *_: (0, 0, 0)),
            pl.BlockSpec((heads, LANES), lambda h, i, *_: (0, 0)),
            pl.BlockSpec((4, HEAD_DIM), lambda h, i, *_: (0, 0)),
            pl.BlockSpec((V_DIM, 1), lambda h, i, *_: (0, 0)),
        ],
        out_specs=pl.BlockSpec((tq, V_DIM), lambda h, i, *_: (i, h)),
        scratch_shapes=[
            pltpu.VMEM((V_DIM, 2 * tq), BF16),
            pltpu.VMEM((tk, 2 * tq), F32),
            pltpu.VMEM((tk, 2 * tq), F32),
            pltpu.VMEM((1, 2 * tq), F32),
            pltpu.VMEM((1, 2 * tq), F32),
            pltpu.VMEM((1, 2 * tq), F32),
            pltpu.VMEM((V_DIM + ONES_ROWS, 2 * tq), F32),
            pltpu.VMEM((tk, tq), F32),
        ],
    )
    return pl.pallas_call(
        kernel,
        grid_spec=grid_spec,
        out_shape=jax.ShapeDtypeStruct((S, D), BF16),
        compiler_params=_params("parallel", "parallel"),
        name="diff_attention",
    )(bmin, bmax, table2.reshape(-1), order,
      qt, k, vt, positions.reshape(1, S), positions.reshape(nk, 1, tk), table_rows,
      lam_params, subln_g.reshape(V_DIM, 1))


def _proj_ln_kernel(a_ref, x_ref, w_ref, lg_ref, lb_ref, o_ref):
    z = DEEPNORM_ALPHA * x_ref[...] + _dot(a_ref[...], w_ref[...])
    o_ref[...] = _layer_norm(z, lg_ref[...], lb_ref[...])


def _proj_ln(a, x, w, ln_g, ln_b):
    S, D = x.shape
    K = a.shape[1]
    tm = TM_PROJ
    return pl.pallas_call(
        _proj_ln_kernel,
        grid=(S // tm,),
        in_specs=[
            pl.BlockSpec((tm, K), lambda i: (i, 0)),
            pl.BlockSpec((tm, D), lambda i: (i, 0)),
            _const_spec((K, D)),
            _const_spec((1, D)),
            _const_spec((1, D)),
        ],
        out_specs=pl.BlockSpec((tm, D), lambda i: (i, 0)),
        out_shape=jax.ShapeDtypeStruct((S, D), F32),
        compiler_params=_params("parallel"),
        name="attn_out_proj",
    )(a, x, w.astype(BF16), ln_g.reshape(1, D), ln_b.reshape(1, D))


def _attn_layer(x, positions, bmin, bmax, order, rel_table, w_qkv, lam_params, subln_g, w_out,
                lambda_init, ln_g, ln_b):
    qt, k, vt = _qkv(x, w_qkv)
    o = _attention(qt, k, vt, positions, bmin, bmax, order, rel_table, lam_params, subln_g,
                   lambda_init)
    return _proj_ln(o, x, w_out, ln_g, ln_b)


def _ffn_kernel(x_ref, xp_ref, xn_ref, wi_ref, cw_ref, cb_ref, wd_ref, lg_ref, lb_ref,
                o_ref, g_scr):
    i = pl.program_id(0)
    x = x_ref[...]
    tm = x.shape[0]
    halo = xp_ref.shape[0]
    F = wd_ref.shape[0]
    xp = jnp.where(i > 0, xp_ref[...], 0.0)
    xn = jnp.where(i < pl.num_programs(0) - 1, xn_ref[...], 0.0)
    xe = jnp.concatenate([xp, x, xn], axis=0).astype(BF16)
    g_scr[...] = _dot(xe, wi_ref[:, 0:F])
    cw = cw_ref[...]
    gate = (g_scr[halo - 1:halo - 1 + tm, :] * cw[0:1]
            + g_scr[halo:halo + tm, :] * cw[1:2]
            + g_scr[halo + 1:halo + 1 + tm, :] * cw[2:3]
            + cb_ref[...])
    up = _dot(x.astype(BF16), wi_ref[:, F:2 * F])
    hidden = (_gelu(gate) * up).astype(BF16)
    z = DEEPNORM_ALPHA * x + _dot(hidden, wd_ref[...])
    o_ref[...] = _layer_norm(z, lg_ref[...], lb_ref[...])


def _ffn_layer(x, w_in, conv_w, conv_b, w_down, ln_g, ln_b):
    S, D = x.shape
    F = w_down.shape[0]
    tm, halo = TM_FFN, SUBLANES
    blocks_per_tile = tm // halo
    last_halo_block = S // halo - 1
    return pl.pallas_call(
        _ffn_kernel,
        grid=(S // tm,),
        in_specs=[
            pl.BlockSpec((tm, D), lambda i: (i, 0)),
            pl.BlockSpec((halo, D), lambda i: (jnp.maximum(i * blocks_per_tile - 1, 0), 0)),
            pl.BlockSpec((halo, D),
                         lambda i: (jnp.minimum((i + 1) * blocks_per_tile, last_halo_block), 0)),
            _const_spec((D, 2 * F)),
            _const_spec((conv_w.shape[0], F)),
            _const_spec((1, F)),
            _const_spec((F, D)),
            _const_spec((1, D)),
            _const_spec((1, D)),
        ],
        out_specs=pl.BlockSpec((tm, D), lambda i: (i, 0)),
        out_shape=jax.ShapeDtypeStruct((S, D), F32),
        scratch_shapes=[pltpu.VMEM((tm + 2 * halo, F), F32)],
        compiler_params=_params("parallel"),
        name="conv_glu_ffn",
    )(x, x, x, w_in.astype(BF16), conv_w, conv_b.reshape(1, F), w_down.astype(BF16),
      ln_g.reshape(1, D), ln_b.reshape(1, D))


def kernel(x, positions, rel_bias_table, a_w_in, a_norm_g, a_norm_b, a_w_s, a_b_s, a_w_out,
           b_w_qkv, b_lambda, b_subln_g, b_w_out, f_w_in, f_conv_w, f_conv_b, f_w_down,
           ln_g, ln_b):
    B, S, D = x.shape
    outs = []
    for b in range(B):
        xs = x[b]
        pos = positions[b]
        bmin, bmax = _pos_stats(pos)
        order = _tile_order(bmin, bmax, S // TQ, S // TK)
        for i in range(DEPTH):
            j = i // 2
            if i % 2 == 0:
                xs = _gmlp_layer(xs, a_w_in[j], a_norm_g[j], a_norm_b[j], a_w_s[j], a_b_s[j],
                                 a_w_out[j], ln_g[i, 0], ln_b[i, 0])
            else:
                xs = _attn_layer(xs, pos, bmin, bmax, order, rel_bias_table, b_w_qkv[j], b_lambda[j],
                                 b_subln_g[j], b_w_out[j], _lambda_init(i), ln_g[i, 0], ln_b[i, 0])
            xs = _ffn_layer(xs, f_w_in[i], f_conv_w[i], f_conv_b[i], f_w_down[i],
                            ln_g[i, 1], ln_b[i, 1])
        outs.append(xs)
    return jnp.stack(outs)
```

```python
import functools
import math

import jax
import jax.numpy as jnp
from jax import lax
from jax.experimental import pallas as pl
from jax.experimental.pallas import tpu as pltpu

F32 = jnp.float32
BF16 = jnp.bfloat16

DEPTH = 4
A_CHUNK = 128
A_GROUPS = 8
HEAD_DIM = 64
V_DIM = 2 * HEAD_DIM
REL_BUCKETS = 32
REL_FAR = 128
LN_EPS = 1e-5
LOG2_E = math.log2(math.e)
MASKED_LOGIT = -1e30
DEEPNORM_ALPHA = (2 * DEPTH) ** 0.25
REL_CLIP = 2047
F32_MANTISSA_BITS = 23
F32_EXP_BIAS = 127

LANES = 128
SUBLANES = 8
BF16_ROWS = 16
VMEM_BYTES = 64 * 1024 * 1024
VMEM_LIMIT = VMEM_BYTES * 3 // 4

TM_GMLP = 512
TQ = 512
TK = 256
TM_QKV = TK
TM_PROJ = 512
TM_FFN = 512
STAGE_GROUP = 16
MIN_STAGE_GROUP = 16
ONES_ROWS = BF16_ROWS


def _lambda_init(layer_idx):
    return 0.8 - 0.6 * math.exp(-0.3 * layer_idx)


def _gelu(x):
    return 0.5 * x * (1.0 + lax.erf(x * (1.0 / math.sqrt(2.0))))


def _layer_norm(z, g, b):
    mu = jnp.mean(z, axis=-1, keepdims=True)
    zc = z - mu
    var = jnp.mean(zc * zc, axis=-1, keepdims=True)
    return zc * lax.rsqrt(var + LN_EPS) * g + b


def _dot(a, b):
    return jnp.dot(a, b, preferred_element_type=F32)


def _dot_nt(a, b):
    return lax.dot_general(a, b, (((1,), (1,)), ((), ())), preferred_element_type=F32)


def _const_spec(shape):
    nd = len(shape)
    return pl.BlockSpec(shape, lambda *_: (0,) * nd, pipeline_mode=pl.Buffered(1))


def _params(*sem):
    return pltpu.CompilerParams(dimension_semantics=sem, vmem_limit_bytes=VMEM_LIMIT)


def _gmlp_kernel(x_ref, wi_ref, ng_ref, nb_ref, ws_ref, bs_ref, wo_ref, lg_ref, lb_ref, o_ref):
    x = x_ref[...]
    tm = x.shape[0]
    W = wo_ref.shape[0]
    gd = W // A_GROUPS
    hidden = _gelu(_dot(x.astype(BF16), wi_ref[...]))
    v = _layer_norm(hidden[:, W:], ng_ref[...], nb_ref[...]).astype(BF16)
    mixed = []
    for c in range(tm // A_CHUNK):
        row = []
        for g in range(A_GROUPS):
            vc = v[c * A_CHUNK:(c + 1) * A_CHUNK, g * gd:(g + 1) * gd]
            row.append(_dot(ws_ref[g], vc) + bs_ref[g])
        mixed.append(jnp.concatenate(row, axis=1))
    y = (hidden[:, :W] * jnp.concatenate(mixed, axis=0)).astype(BF16)
    z = DEEPNORM_ALPHA * x + _dot(y, wo_ref[...])
    o_ref[...] = _layer_norm(z, lg_ref[...], lb_ref[...])


def _gmlp_layer(x, w_in, norm_g, norm_b, w_s, b_s, w_out, ln_g, ln_b):
    S, D = x.shape
    W = w_out.shape[0]
    tm = TM_GMLP
    return pl.pallas_call(
        _gmlp_kernel,
        grid=(S // tm,),
        in_specs=[
            pl.BlockSpec((tm, D), lambda i: (i, 0)),
            _const_spec((D, 2 * W)),
            _const_spec((1, W)),
            _const_spec((1, W)),
            _const_spec((A_GROUPS, A_CHUNK, A_CHUNK)),
            _const_spec((A_GROUPS, A_CHUNK, 1)),
            _const_spec((W, D)),
            _const_spec((1, D)),
            _const_spec((1, D)),
        ],
        out_specs=pl.BlockSpec((tm, D), lambda i: (i, 0)),
        out_shape=jax.ShapeDtypeStruct((S, D), F32),
        compiler_params=_params("parallel"),
        name="gmlp_layer",
    )(x, w_in.astype(BF16), norm_g.reshape(1, W), norm_b.reshape(1, W), w_s.astype(BF16),
      b_s.reshape(A_GROUPS, A_CHUNK, 1), w_out.astype(BF16), ln_g.reshape(1, D), ln_b.reshape(1, D))


def _pos_stats_kernel(p_ref, mn_ref, mx_ref):
    p = p_ref[...]
    mn_ref[...] = jnp.min(p, axis=1, keepdims=True)
    mx_ref[...] = jnp.max(p, axis=1, keepdims=True)


def _pos_stats(positions):
    nb = positions.shape[0] // LANES
    mn, mx = pl.pallas_call(
        _pos_stats_kernel,
        out_shape=(jax.ShapeDtypeStruct((nb, 1), jnp.int32),) * 2,
        name="pos_stats",
    )(positions.reshape(nb, LANES))
    return mn.reshape(nb), mx.reshape(nb)


def _tile_order_kernel(bmin_ref, bmax_ref, order_ref, near_ref, *, nq, nk, q_blocks, k_blocks):
    n_groups = nk // STAGE_GROUP

    def per_query_tile(qi, carry):
        qmin = bmin_ref[qi * q_blocks]
        qmax = bmax_ref[qi * q_blocks]
        for r in range(1, q_blocks):
            qmin = jnp.minimum(qmin, bmin_ref[qi * q_blocks + r])
            qmax = jnp.maximum(qmax, bmax_ref[qi * q_blocks + r])

        def near(kt):
            kmin = bmin_ref[kt * k_blocks]
            kmax = bmax_ref[kt * k_blocks]
            for r in range(1, k_blocks):
                kmin = jnp.minimum(kmin, bmin_ref[kt * k_blocks + r])
                kmax = jnp.maximum(kmax, bmax_ref[kt * k_blocks + r])
            far = jnp.logical_or(kmin - qmax >= REL_FAR, kmax - qmin <= -REL_FAR)
            return jnp.logical_not(far).astype(jnp.int32)

        def count(kt, total):
            near_ref[kt] = near(kt)
            return total + near_ref[kt]

        n_near = lax.fori_loop(0, nk, count, 0)
        reorder = n_near <= n_groups

        def place(kt, state):
            near_seen, far_slot = state
            is_near = near_ref[kt]
            slot = jnp.where(is_near == 1, near_seen * STAGE_GROUP, far_slot)
            order_ref[qi * nk + jnp.where(reorder, slot, kt)] = kt
            nxt = far_slot + 1
            held = jnp.logical_and(nxt % STAGE_GROUP == 0, nxt // STAGE_GROUP < n_near)
            nxt = jnp.where(held, nxt + 1, nxt)
            return near_seen + is_near, jnp.where(is_near == 1, far_slot, nxt)

        lax.fori_loop(0, nk, place, (0, jnp.where(n_near > 0, 1, 0)))
        return carry

    lax.fori_loop(0, nq, per_query_tile, 0)


def _tile_order(bmin, bmax, nq, nk):
    smem = pl.BlockSpec(memory_space=pltpu.SMEM)
    kernel = functools.partial(_tile_order_kernel, nq=nq, nk=nk,
                               q_blocks=TQ // LANES, k_blocks=TK // LANES)
    return pl.pallas_call(
        kernel,
        in_specs=[smem, smem],
        out_specs=smem,
        out_shape=jax.ShapeDtypeStruct((nq * nk,), jnp.int32),
        scratch_shapes=[pltpu.SMEM((nk,), jnp.int32)],
        name="tile_order",
    )(bmin, bmax)


def _qkv_kernel(x_ref, wqt_ref, wk_ref, wvt_ref, qt_ref, k_ref, vt_ref):
    xb = x_ref[...].astype(BF16)
    tm = xb.shape[0]
    qt_ref[...] = _dot_nt(wqt_ref[...], xb).astype(BF16)
    k_ref[...] = _dot(xb, wk_ref[...]).astype(BF16)
    vt = _dot_nt(wvt_ref[...], xb).astype(BF16)
    heads = vt.shape[0] // V_DIM
    vt_ref[:, 0, 0:V_DIM, :] = vt.reshape(heads, V_DIM, tm)
    vt_ref[:, 0, V_DIM:, :] = jnp.ones((heads, ONES_ROWS, tm), BF16)


def _qkv(x, w_qkv):
    S, D = x.shape
    tm = TM_QKV
    heads = D // V_DIM
    wqt = (w_qkv[:, :D] * (HEAD_DIM ** -0.5 * LOG2_E)).T.astype(BF16)
    wk = w_qkv[:, D:2 * D].astype(BF16)
    wvt = w_qkv[:, 2 * D:].T.astype(BF16)
    return pl.pallas_call(
        _qkv_kernel,
        grid=(S // tm,),
        in_specs=[
            pl.BlockSpec((tm, D), lambda i: (i, 0)),
            _const_spec((D, D)),
            _const_spec((D, D)),
            _const_spec((D, D)),
        ],
        out_specs=[
            pl.BlockSpec((D, tm), lambda i: (0, i)),
            pl.BlockSpec((tm, D), lambda i: (i, 0)),
            pl.BlockSpec((heads, 1, V_DIM + ONES_ROWS, tm), lambda i: (0, i, 0, 0)),
        ],
        out_shape=[
            jax.ShapeDtypeStruct((D, S), BF16),
            jax.ShapeDtypeStruct((S, D), BF16),
            jax.ShapeDtypeStruct((heads, S // tm, V_DIM + ONES_ROWS, tm), BF16),
        ],
        compiler_params=_params("parallel"),
        name="qkv_proj",
    )(x, wqt, wk, wvt)


def _bias_tile(posk_row, posq_row, table_row):
    tk = posk_row.shape[1]
    tq = posq_row.shape[1]
    table_sq = jnp.broadcast_to(table_row, (LANES, LANES))
    rows = []
    for c in range(tk // LANES):
        pk = posk_row[:, c * LANES:(c + 1) * LANES]
        pk_col = jnp.transpose(jnp.broadcast_to(pk, (LANES, LANES)))
        cols = []
        for d in range(tq // LANES):
            rel = pk_col - posq_row[:, d * LANES:(d + 1) * LANES]
            n = jnp.abs(rel)
            nsq = jnp.square(jnp.minimum(n, REL_CLIP)).astype(F32)
            log2_nsq = (lax.bitcast_convert_type(nsq, jnp.int32) >> F32_MANTISSA_BITS) - F32_EXP_BIAS
            large = jnp.minimum(log2_nsq + 2, REL_BUCKETS // 2 - 1)
            bucket = jnp.where(n < REL_BUCKETS // 4, n, large)
            bucket = bucket + jnp.where(rel > 0, REL_BUCKETS // 2, 0)
            cols.append(jnp.take_along_axis(table_sq, bucket, axis=1))
        rows.append(jnp.concatenate(cols, axis=1))
    return jnp.concatenate(rows, axis=0)


def _attn_kernel(bmin_ref, bmax_ref, tbl_ref, order_ref,
                 qt_ref, k_ref, vt_ref, posq_ref, posk_ref, tblv_ref, lam_ref, sg_ref,
                 o_ref, qcat, s_a, s_b, smax_a, smax_b, m_scr, acc_scr, *, nk, lambda_init):
    h = pl.program_id(0)
    qi = pl.program_id(1)
    tq = qt_ref.shape[1]
    tk = vt_ref.shape[3]
    heads = pl.num_programs(0)

    q = qt_ref[...]
    row = lax.broadcasted_iota(jnp.int32, q.shape, 0)
    zero = jnp.zeros_like(q)
    qcat[:, 0:tq] = jnp.where(row < HEAD_DIM, q, zero)
    qcat[:, tq:2 * tq] = jnp.where(row >= HEAD_DIM, q, zero)

    qmin = bmin_ref[qi * (tq // LANES)]
    qmax = bmax_ref[qi * (tq // LANES)]
    for r in range(1, tq // LANES):
        qmin = jnp.minimum(qmin, bmin_ref[qi * (tq // LANES) + r])
        qmax = jnp.maximum(qmax, bmax_ref[qi * (tq // LANES) + r])
    bias_before = tbl_ref[(REL_BUCKETS // 2 - 1) * heads + h]
    bias_after = tbl_ref[(REL_BUCKETS - 1) * heads + h]

    acc_scr[...] = jnp.zeros_like(acc_scr)
    m_scr[...] = jnp.full(m_scr.shape, MASKED_LOGIT, F32)

    def classify(kt):
        kmin = bmin_ref[kt * (tk // LANES)]
        kmax = bmax_ref[kt * (tk // LANES)]
        for r in range(1, tk // LANES):
            kmin = jnp.minimum(kmin, bmin_ref[kt * (tk // LANES) + r])
            kmax = jnp.maximum(kmax, bmax_ref[kt * (tk // LANES) + r])
        all_after = kmin - qmax >= REL_FAR
        all_before = kmax - qmin <= -REL_FAR
        near = jnp.logical_not(jnp.logical_or(all_after, all_before))
        const_bias = jnp.where(all_after, bias_after, jnp.where(all_before, bias_before, 0.0))
        return near, const_bias

    def logits(kt, s_ref, smax_ref):
        kk = k_ref[pl.ds(pl.multiple_of(kt * tk, tk), tk), :]
        s = _dot(kk, qcat[...])
        s_ref[...] = s
        smax_ref[...] = jnp.max(s, axis=0, keepdims=True)

    def add_near_bias(kt, s_ref, smax_ref):
        bias = _bias_tile(posk_ref[kt], posq_ref[...], tblv_ref[pl.ds(h, 1), :])
        s = s_ref[...] + jnp.concatenate([bias, bias], axis=1)
        s_ref[...] = s
        smax_ref[...] = jnp.max(s, axis=0, keepdims=True)

    def softmax_update(kt, s_ref, smax_ref, const_bias):
        m_old = m_scr[...]
        m_new = jnp.maximum(m_old, smax_ref[...] + const_bias)
        p = jnp.exp2(s_ref[...] - (m_new - const_bias)).astype(BF16)
        acc_scr[...] = jnp.exp2(m_old - m_new) * acc_scr[...] + _dot(vt_ref[0, kt], p)
        m_scr[...] = m_new

    def tile_at(step):
        return order_ref[qi * nk + step]

    def stage(step, cur, nxt, const_bias, fix_next):
        kt = tile_at(step)
        kn = tile_at(jnp.minimum(step + 1, nk - 1))
        logits(kn, *nxt)
        softmax_update(kt, *cur, const_bias)
        near_next, bias_next = classify(kn)
        if fix_next:
            @pl.when(near_next)
            def _():
                add_near_bias(kn, *nxt)
        return near_next, bias_next

    buf_a = (s_a, smax_a)
    buf_b = (s_b, smax_b)
    near0, bias0 = classify(tile_at(0))
    logits(tile_at(0), *buf_a)

    @pl.when(near0)
    def _():
        add_near_bias(tile_at(0), *buf_a)

    def pair(step, const_bias):
        _, const_bias = stage(step, buf_a, buf_b, const_bias, True)
        _, const_bias = stage(step + 1, buf_b, buf_a, const_bias, True)
        return const_bias

    def run_group(base, size, const_bias):
        inner_far = jnp.bool_(True)
        for i in range(1, size):
            inner_far = jnp.logical_and(inner_far, jnp.logical_not(classify(tile_at(base + i))[0]))

        def branch_free():
            c = const_bias
            for i in range(size):
                cur, nxt = (buf_a, buf_b) if i % 2 == 0 else (buf_b, buf_a)
                _, c = stage(base + i, cur, nxt, c, i == size - 1)
            return c

        def split():
            if size == MIN_STAGE_GROUP:
                return lax.fori_loop(0, size // 2, lambda t, c: pair(base + 2 * t, c), const_bias)
            half = size // 2
            return lax.fori_loop(0, 2, lambda t, c: run_group(base + t * half, half, c), const_bias)

        return lax.cond(inner_far, branch_free, split)

    lax.fori_loop(0, nk // STAGE_GROUP, lambda j, c: run_group(j * STAGE_GROUP, STAGE_GROUP, c), bias0)

    lp = lam_ref[...]
    lam = (jnp.exp(jnp.sum(lp[0:1] * lp[1:2], axis=1, keepdims=True))
           - jnp.exp(jnp.sum(lp[2:3] * lp[3:4], axis=1, keepdims=True)) + lambda_init)
    o1 = acc_scr[0:V_DIM, 0:tq] / acc_scr[V_DIM:V_DIM + 1, 0:tq]
    o2 = acc_scr[0:V_DIM, tq:2 * tq] / acc_scr[V_DIM:V_DIM + 1, tq:2 * tq]
    o = o1 - lam * o2
    o = o * lax.rsqrt(jnp.mean(o * o, axis=0, keepdims=True) + LN_EPS)
    o = o * sg_ref[...] * (1.0 - lambda_init)
    o_ref[...] = jnp.transpose(o).astype(BF16)


def _attention(qt, k, vt, positions, bmin, bmax, order, rel_table, lam_params, subln_g, lambda_init):
    D, S = qt.shape
    heads = D // V_DIM
    tq, tk = TQ, TK
    nk = S // tk
    assert S % tq == 0 and nk % STAGE_GROUP == 0, (S, tq, tk)
    table2 = rel_table.astype(F32) * LOG2_E
    table_rows = jnp.zeros((heads, LANES), F32).at[:, :REL_BUCKETS].set(table2.T)
    kernel = functools.partial(_attn_kernel, nk=nk, lambda_init=lambda_init)
    grid_spec = pltpu.PrefetchScalarGridSpec(
        num_scalar_prefetch=4,
        grid=(heads, S // tq),
        in_specs=[
            pl.BlockSpec((V_DIM, tq), lambda h, i, *_: (h, i)),
            pl.BlockSpec((S, V_DIM), lambda h, i, *_: (0, h)),
            pl.BlockSpec((1, nk, V_DIM + ONES_ROWS, tk), lambda h, i, *_: (h, 0, 0, 0)),
            pl.BlockSpec((1, tq), lambda h, i, *_: (0, i)),
            pl.BlockSpec((nk, 1, tk), lambda h, i, *_: (0, 0, 0)),
            pl.BlockSpec((heads, LANES), lambda h, i, *_: (0, 0)),
            pl.BlockSpec((4, HEAD_DIM), lambda h, i, *_: (0, 0)),
            pl.BlockSpec((V_DIM, 1), lambda h, i, *_: (0, 0)),
        ],
        out_specs=pl.BlockSpec((tq, V_DIM), lambda h, i, *_: (i, h)),
        scratch_shapes=[
            pltpu.VMEM((V_DIM, 2 * tq), BF16),
            pltpu.VMEM((tk, 2 * tq), F32),
            pltpu.VMEM((tk, 2 * tq), F32),
            pltpu.VMEM((1, 2 * tq), F32),
            pltpu.VMEM((1, 2 * tq), F32),
            pltpu.VMEM((1, 2 * tq), F32),
            pltpu.VMEM((V_DIM + ONES_ROWS, 2 * tq), F32),
        ],
    )
    return pl.pallas_call(
        kernel,
        grid_spec=grid_spec,
        out_shape=jax.ShapeDtypeStruct((S, D), BF16),
        compiler_params=_params("parallel", "parallel"),
        name="diff_attention",
    )(bmin, bmax, table2.reshape(-1), order,
      qt, k, vt, positions.reshape(1, S), positions.reshape(nk, 1, tk), table_rows,
      lam_params, subln_g.reshape(V_DIM, 1))


def _proj_ln_kernel(a_ref, x_ref, w_ref, lg_ref, lb_ref, o_ref):
    z = DEEPNORM_ALPHA * x_ref[...] + _dot(a_ref[...], w_ref[...])
    o_ref[...] = _layer_norm(z, lg_ref[...], lb_ref[...])


def _proj_ln(a, x, w, ln_g, ln_b):
    S, D = x.shape
    K = a.shape[1]
    tm = TM_PROJ
    return pl.pallas_call(
        _proj_ln_kernel,
        grid=(S // tm,),
        in_specs=[
            pl.BlockSpec((tm, K), lambda i: (i, 0)),
            pl.BlockSpec((tm, D), lambda i: (i, 0)),
            _const_spec((K, D)),
            _const_spec((1, D)),
            _const_spec((1, D)),
        ],
        out_specs=pl.BlockSpec((tm, D), lambda i: (i, 0)),
        out_shape=jax.ShapeDtypeStruct((S, D), F32),
        compiler_params=_params("parallel"),
        name="attn_out_proj",
    )(a, x, w.astype(BF16), ln_g.reshape(1, D), ln_b.reshape(1, D))


def _attn_layer(x, positions, bmin, bmax, order, rel_table, w_qkv, lam_params, subln_g, w_out,
                lambda_init, ln_g, ln_b):
    qt, k, vt = _qkv(x, w_qkv)
    o = _attention(qt, k, vt, positions, bmin, bmax, order, rel_table, lam_params, subln_g,
                   lambda_init)
    return _proj_ln(o, x, w_out, ln_g, ln_b)


def _ffn_kernel(x_ref, xp_ref, xn_ref, wi_ref, cw_ref, cb_ref, wd_ref, lg_ref, lb_ref,
                o_ref, g_scr):
    i = pl.program_id(0)
    x = x_ref[...]
    tm = x.shape[0]
    halo = xp_ref.shape[0]
    F = wd_ref.shape[0]
    xp = jnp.where(i > 0, xp_ref[...], 0.0)
    xn = jnp.where(i < pl.num_programs(0) - 1, xn_ref[...], 0.0)
    xe = jnp.concatenate([xp, x, xn], axis=0).astype(BF16)
    g_scr[...] = _dot(xe, wi_ref[:, 0:F])
    cw = cw_ref[...]
    gate = (g_scr[halo - 1:halo - 1 + tm, :] * cw[0:1]
            + g_scr[halo:halo + tm, :] * cw[1:2]
            + g_scr[halo + 1:halo + 1 + tm, :] * cw[2:3]
            + cb_ref[...])
    up = _dot(x.astype(BF16), wi_ref[:, F:2 * F])
    hidden = (_gelu(gate) * up).astype(BF16)
    z = DEEPNORM_ALPHA * x + _dot(hidden, wd_ref[...])
    o_ref[...] = _layer_norm(z, lg_ref[...], lb_ref[...])


def _ffn_layer(x, w_in, conv_w, conv_b, w_down, ln_g, ln_b):
    S, D = x.shape
    F = w_down.shape[0]
    tm, halo = TM_FFN, SUBLANES
    blocks_per_tile = tm // halo
    last_halo_block = S // halo - 1
    return pl.pallas_call(
        _ffn_kernel,
        grid=(S // tm,),
        in_specs=[
            pl.BlockSpec((tm, D), lambda i: (i, 0)),
            pl.BlockSpec((halo, D), lambda i: (jnp.maximum(i * blocks_per_tile - 1, 0), 0)),
            pl.BlockSpec((halo, D),
                         lambda i: (jnp.minimum((i + 1) * blocks_per_tile, last_halo_block), 0)),
            _const_spec((D, 2 * F)),
            _const_spec((conv_w.shape[0], F)),
            _const_spec((1, F)),
            _const_spec((F, D)),
            _const_spec((1, D)),
            _const_spec((1, D)),
        ],
        out_specs=pl.BlockSpec((tm, D), lambda i: (i, 0)),
        out_shape=jax.ShapeDtypeStruct((S, D), F32),
        scratch_shapes=[pltpu.VMEM((tm + 2 * halo, F), F32)],
        compiler_params=_params("parallel"),
        name="conv_glu_ffn",
    )(x, x, x, w_in.astype(BF16), conv_w, conv_b.reshape(1, F), w_down.astype(BF16),
      ln_g.reshape(1, D), ln_b.reshape(1, D))


def kernel(x, positions, rel_bias_table, a_w_in, a_norm_g, a_norm_b, a_w_s, a_b_s, a_w_out,
           b_w_qkv, b_lambda, b_subln_g, b_w_out, f_w_in, f_conv_w, f_conv_b, f_w_down,
           ln_g, ln_b):
    B, S, D = x.shape
    outs = []
    for b in range(B):
        xs = x[b]
        pos = positions[b]
        bmin, bmax = _pos_stats(pos)
        order = _tile_order(bmin, bmax, S // TQ, S // TK)
        for i in range(DEPTH):
            j = i // 2
            if i % 2 == 0:
                xs = _gmlp_layer(xs, a_w_in[j], a_norm_g[j], a_norm_b[j], a_w_s[j], a_b_s[j],
                                 a_w_out[j], ln_g[i, 0], ln_b[i, 0])
            else:
                xs = _attn_layer(xs, pos, bmin, bmax, order, rel_bias_table, b_w_qkv[j], b_lambda[j],
                                 b_subln_g[j], b_w_out[j], _lambda_init(i), ln_g[i, 0], ln_b[i, 0])
            xs = _ffn_layer(xs, f_w_in[i], f_conv_w[i], f_conv_b[i], f_w_down[i],
                            ln_g[i, 1], ln_b[i, 1])
        outs.append(xs)
    return jnp.stack(outs)
```

```python
import functools
import math

import jax
import jax.numpy as jnp
from jax import lax
from jax.experimental import pallas as pl
from jax.experimental.pallas import tpu as pltpu

F32 = jnp.float32
BF16 = jnp.bfloat16

DEPTH = 4
A_CHUNK = 128
A_GROUPS = 8
HEAD_DIM = 64
V_DIM = 2 * HEAD_DIM
REL_BUCKETS = 32
REL_FAR = 128
LN_EPS = 1e-5
LOG2_E = math.log2(math.e)
MASKED_LOGIT = -1e30
DEEPNORM_ALPHA = (2 * DEPTH) ** 0.25
REL_CLIP = 2047
F32_MANTISSA_BITS = 23
F32_EXP_BIAS = 127

LANES = 128
SUBLANES = 8
BF16_ROWS = 16
VMEM_BYTES = 64 * 1024 * 1024
VMEM_LIMIT = VMEM_BYTES * 3 // 4

TM_GMLP = 512
TQ = 512
TK = 256
TM_QKV = TK
TM_PROJ = 512
TM_FFN = 512
STAGE_GROUP = 16
ONES_ROWS = BF16_ROWS


def _lambda_init(layer_idx):
    return 0.8 - 0.6 * math.exp(-0.3 * layer_idx)


def _gelu(x):
    return 0.5 * x * (1.0 + lax.erf(x * (1.0 / math.sqrt(2.0))))


def _layer_norm(z, g, b):
    mu = jnp.mean(z, axis=-1, keepdims=True)
    zc = z - mu
    var = jnp.mean(zc * zc, axis=-1, keepdims=True)
    return zc * lax.rsqrt(var + LN_EPS) * g + b


def _dot(a, b):
    return jnp.dot(a, b, preferred_element_type=F32)


def _dot_nt(a, b):
    return lax.dot_general(a, b, (((1,), (1,)), ((), ())), preferred_element_type=F32)


def _const_spec(shape):
    nd = len(shape)
    return pl.BlockSpec(shape, lambda *_: (0,) * nd, pipeline_mode=pl.Buffered(1))


def _params(*sem):
    return pltpu.CompilerParams(dimension_semantics=sem, vmem_limit_bytes=VMEM_LIMIT)


def _gmlp_kernel(x_ref, wi_ref, ng_ref, nb_ref, ws_ref, bs_ref, wo_ref, lg_ref, lb_ref, o_ref):
    x = x_ref[...]
    tm = x.shape[0]
    W = wo_ref.shape[0]
    gd = W // A_GROUPS
    hidden = _gelu(_dot(x.astype(BF16), wi_ref[...]))
    v = _layer_norm(hidden[:, W:], ng_ref[...], nb_ref[...]).astype(BF16)
    mixed = []
    for c in range(tm // A_CHUNK):
        row = []
        for g in range(A_GROUPS):
            vc = v[c * A_CHUNK:(c + 1) * A_CHUNK, g * gd:(g + 1) * gd]
            row.append(_dot(ws_ref[g], vc) + bs_ref[g])
        mixed.append(jnp.concatenate(row, axis=1))
    y = (hidden[:, :W] * jnp.concatenate(mixed, axis=0)).astype(BF16)
    z = DEEPNORM_ALPHA * x + _dot(y, wo_ref[...])
    o_ref[...] = _layer_norm(z, lg_ref[...], lb_ref[...])


def _gmlp_layer(x, w_in, norm_g, norm_b, w_s, b_s, w_out, ln_g, ln_b):
    S, D = x.shape
    W = w_out.shape[0]
    tm = TM_GMLP
    return pl.pallas_call(
        _gmlp_kernel,
        grid=(S // tm,),
        in_specs=[
            pl.BlockSpec((tm, D), lambda i: (i, 0)),
            _const_spec((D, 2 * W)),
            _const_spec((1, W)),
            _const_spec((1, W)),
            _const_spec((A_GROUPS, A_CHUNK, A_CHUNK)),
            _const_spec((A_GROUPS, A_CHUNK, 1)),
            _const_spec((W, D)),
            _const_spec((1, D)),
            _const_spec((1, D)),
        ],
        out_specs=pl.BlockSpec((tm, D), lambda i: (i, 0)),
        out_shape=jax.ShapeDtypeStruct((S, D), F32),
        compiler_params=_params("parallel"),
        name="gmlp_layer",
    )(x, w_in.astype(BF16), norm_g.reshape(1, W), norm_b.reshape(1, W), w_s.astype(BF16),
      b_s.reshape(A_GROUPS, A_CHUNK, 1), w_out.astype(BF16), ln_g.reshape(1, D), ln_b.reshape(1, D))


def _pos_stats_kernel(p_ref, mn_ref, mx_ref):
    p = p_ref[...]
    mn_ref[...] = jnp.min(p, axis=1, keepdims=True)
    mx_ref[...] = jnp.max(p, axis=1, keepdims=True)


def _pos_stats(positions):
    nb = positions.shape[0] // LANES
    mn, mx = pl.pallas_call(
        _pos_stats_kernel,
        out_shape=(jax.ShapeDtypeStruct((nb, 1), jnp.int32),) * 2,
        name="pos_stats",
    )(positions.reshape(nb, LANES))
    return mn.reshape(nb), mx.reshape(nb)


def _tile_order_kernel(bmin_ref, bmax_ref, order_ref, near_ref, *, nq, nk, q_blocks, k_blocks):
    n_groups = nk // STAGE_GROUP

    def per_query_tile(qi, carry):
        qmin = bmin_ref[qi * q_blocks]
        qmax = bmax_ref[qi * q_blocks]
        for r in range(1, q_blocks):
            qmin = jnp.minimum(qmin, bmin_ref[qi * q_blocks + r])
            qmax = jnp.maximum(qmax, bmax_ref[qi * q_blocks + r])

        def near(kt):
            kmin = bmin_ref[kt * k_blocks]
            kmax = bmax_ref[kt * k_blocks]
            for r in range(1, k_blocks):
                kmin = jnp.minimum(kmin, bmin_ref[kt * k_blocks + r])
                kmax = jnp.maximum(kmax, bmax_ref[kt * k_blocks + r])
            far = jnp.logical_or(kmin - qmax >= REL_FAR, kmax - qmin <= -REL_FAR)
            return jnp.logical_not(far).astype(jnp.int32)

        def count(kt, total):
            near_ref[kt] = near(kt)
            return total + near_ref[kt]

        n_near = lax.fori_loop(0, nk, count, 0)
        reorder = n_near <= n_groups

        def place(kt, state):
            near_seen, far_slot = state
            is_near = near_ref[kt]
            slot = jnp.where(is_near == 1, near_seen * STAGE_GROUP, far_slot)
            order_ref[qi * nk + jnp.where(reorder, slot, kt)] = kt
            nxt = far_slot + 1
            held = jnp.logical_and(nxt % STAGE_GROUP == 0, nxt // STAGE_GROUP < n_near)
            nxt = jnp.where(held, nxt + 1, nxt)
            return near_seen + is_near, jnp.where(is_near == 1, far_slot, nxt)

        lax.fori_loop(0, nk, place, (0, jnp.where(n_near > 0, 1, 0)))
        return carry

    lax.fori_loop(0, nq, per_query_tile, 0)


def _tile_order(bmin, bmax, nq, nk):
    smem = pl.BlockSpec(memory_space=pltpu.SMEM)
    kernel = functools.partial(_tile_order_kernel, nq=nq, nk=nk,
                               q_blocks=TQ // LANES, k_blocks=TK // LANES)
    return pl.pallas_call(
        kernel,
        in_specs=[smem, smem],
        out_specs=smem,
        out_shape=jax.ShapeDtypeStruct((nq * nk,), jnp.int32),
        scratch_shapes=[pltpu.SMEM((nk,), jnp.int32)],
        name="tile_order",
    )(bmin, bmax)


def _qkv_kernel(x_ref, wqt_ref, wk_ref, wvt_ref, qt_ref, k_ref, vt_ref):
    xb = x_ref[...].astype(BF16)
    tm = xb.shape[0]
    qt_ref[...] = _dot_nt(wqt_ref[...], xb).astype(BF16)
    k_ref[...] = _dot(xb, wk_ref[...]).astype(BF16)
    vt = _dot_nt(wvt_ref[...], xb).astype(BF16)
    heads = vt.shape[0] // V_DIM
    vt_ref[:, 0, 0:V_DIM, :] = vt.reshape(heads, V_DIM, tm)
    vt_ref[:, 0, V_DIM:, :] = jnp.ones((heads, ONES_ROWS, tm), BF16)


def _qkv(x, w_qkv):
    S, D = x.shape
    tm = TM_QKV
    heads = D // V_DIM
    wqt = (w_qkv[:, :D] * (HEAD_DIM ** -0.5 * LOG2_E)).T.astype(BF16)
    wk = w_qkv[:, D:2 * D].astype(BF16)
    wvt = w_qkv[:, 2 * D:].T.astype(BF16)
    return pl.pallas_call(
        _qkv_kernel,
        grid=(S // tm,),
        in_specs=[
            pl.BlockSpec((tm, D), lambda i: (i, 0)),
            _const_spec((D, D)),
            _const_spec((D, D)),
            _const_spec((D, D)),
        ],
        out_specs=[
            pl.BlockSpec((D, tm), lambda i: (0, i)),
            pl.BlockSpec((tm, D), lambda i: (i, 0)),
            pl.BlockSpec((heads, 1, V_DIM + ONES_ROWS, tm), lambda i: (0, i, 0, 0)),
        ],
        out_shape=[
            jax.ShapeDtypeStruct((D, S), BF16),
            jax.ShapeDtypeStruct((S, D), BF16),
            jax.ShapeDtypeStruct((heads, S // tm, V_DIM + ONES_ROWS, tm), BF16),
        ],
        compiler_params=_params("parallel"),
        name="qkv_proj",
    )(x, wqt, wk, wvt)


def _bias_tile(posk_row, posq_row, table_row):
    tk = posk_row.shape[1]
    tq = posq_row.shape[1]
    table_sq = jnp.broadcast_to(table_row, (LANES, LANES))
    rows = []
    for c in range(tk // LANES):
        pk = posk_row[:, c * LANES:(c + 1) * LANES]
        pk_col = jnp.transpose(jnp.broadcast_to(pk, (LANES, LANES)))
        cols = []
        for d in range(tq // LANES):
            rel = pk_col - posq_row[:, d * LANES:(d + 1) * LANES]
            n = jnp.abs(rel)
            nsq = jnp.square(jnp.minimum(n, REL_CLIP)).astype(F32)
            log2_nsq = (lax.bitcast_convert_type(nsq, jnp.int32) >> F32_MANTISSA_BITS) - F32_EXP_BIAS
            large = jnp.minimum(log2_nsq + 2, REL_BUCKETS // 2 - 1)
            bucket = jnp.where(n < REL_BUCKETS // 4, n, large)
            bucket = bucket + jnp.where(rel > 0, REL_BUCKETS // 2, 0)
            cols.append(jnp.take_along_axis(table_sq, bucket, axis=1))
        rows.append(jnp.concatenate(cols, axis=1))
    return jnp.concatenate(rows, axis=0)


def _attn_kernel(bmin_ref, bmax_ref, tbl_ref, order_ref,
                 qt_ref, k_ref, vt_ref, posq_ref, posk_ref, tblv_ref, lam_ref, sg_ref,
                 o_ref, qcat, s_a, s_b, smax_a, smax_b, m_scr, acc_scr, *, nk, lambda_init):
    h = pl.program_id(0)
    qi = pl.program_id(1)
    tq = qt_ref.shape[1]
    tk = vt_ref.shape[3]
    heads = pl.num_programs(0)

    q = qt_ref[...]
    row = lax.broadcasted_iota(jnp.int32, q.shape, 0)
    zero = jnp.zeros_like(q)
    qcat[:, 0:tq] = jnp.where(row < HEAD_DIM, q, zero)
    qcat[:, tq:2 * tq] = jnp.where(row >= HEAD_DIM, q, zero)

    qmin = bmin_ref[qi * (tq // LANES)]
    qmax = bmax_ref[qi * (tq // LANES)]
    for r in range(1, tq // LANES):
        qmin = jnp.minimum(qmin, bmin_ref[qi * (tq // LANES) + r])
        qmax = jnp.maximum(qmax, bmax_ref[qi * (tq // LANES) + r])
    bias_before = tbl_ref[(REL_BUCKETS // 2 - 1) * heads + h]
    bias_after = tbl_ref[(REL_BUCKETS - 1) * heads + h]

    acc_scr[...] = jnp.zeros_like(acc_scr)
    m_scr[...] = jnp.full(m_scr.shape, MASKED_LOGIT, F32)

    def classify(kt):
        kmin = bmin_ref[kt * (tk // LANES)]
        kmax = bmax_ref[kt * (tk // LANES)]
        for r in range(1, tk // LANES):
            kmin = jnp.minimum(kmin, bmin_ref[kt * (tk // LANES) + r])
            kmax = jnp.maximum(kmax, bmax_ref[kt * (tk // LANES) + r])
        all_after = kmin - qmax >= REL_FAR
        all_before = kmax - qmin <= -REL_FAR
        near = jnp.logical_not(jnp.logical_or(all_after, all_before))
        const_bias = jnp.where(all_after, bias_after, jnp.where(all_before, bias_before, 0.0))
        return near, const_bias

    def logits(kt, s_ref, smax_ref):
        kk = k_ref[pl.ds(pl.multiple_of(kt * tk, tk), tk), :]
        s = _dot(kk, qcat[...])
        s_ref[...] = s
        smax_ref[...] = jnp.max(s, axis=0, keepdims=True)

    def add_near_bias(kt, s_ref, smax_ref):
        bias = _bias_tile(posk_ref[kt], posq_ref[...], tblv_ref[pl.ds(h, 1), :])
        s = s_ref[...] + jnp.concatenate([bias, bias], axis=1)
        s_ref[...] = s
        smax_ref[...] = jnp.max(s, axis=0, keepdims=True)

    def softmax_update(kt, s_ref, smax_ref, const_bias):
        m_old = m_scr[...]
        m_new = jnp.maximum(m_old, smax_ref[...] + const_bias)
        p = jnp.exp2(s_ref[...] - (m_new - const_bias)).astype(BF16)
        acc_scr[...] = jnp.exp2(m_old - m_new) * acc_scr[...] + _dot(vt_ref[0, kt], p)
        m_scr[...] = m_new

    def tile_at(step):
        return order_ref[qi * nk + step]

    def stage(step, cur, nxt, const_bias, fix_next):
        kt = tile_at(step)
        kn = tile_at(jnp.minimum(step + 1, nk - 1))
        logits(kn, *nxt)
        softmax_update(kt, *cur, const_bias)
        near_next, bias_next = classify(kn)
        if fix_next:
            @pl.when(near_next)
            def _():
                add_near_bias(kn, *nxt)
        return near_next, bias_next

    buf_a = (s_a, smax_a)
    buf_b = (s_b, smax_b)
    near0, bias0 = classify(tile_at(0))
    logits(tile_at(0), *buf_a)

    @pl.when(near0)
    def _():
        add_near_bias(tile_at(0), *buf_a)

    def pair(step, const_bias):
        _, const_bias = stage(step, buf_a, buf_b, const_bias, True)
        _, const_bias = stage(step + 1, buf_b, buf_a, const_bias, True)
        return const_bias

    def group(j, const_bias):
        base = j * STAGE_GROUP
        inner_far = jnp.bool_(True)
        for i in range(1, STAGE_GROUP):
            inner_far = jnp.logical_and(inner_far, jnp.logical_not(classify(tile_at(base + i))[0]))

        def branch_free():
            c = const_bias
            for i in range(STAGE_GROUP):
                cur, nxt = (buf_a, buf_b) if i % 2 == 0 else (buf_b, buf_a)
                _, c = stage(base + i, cur, nxt, c, i == STAGE_GROUP - 1)
            return c

        def checked():
            return lax.fori_loop(0, STAGE_GROUP // 2, lambda t, c: pair(base + 2 * t, c), const_bias)

        return lax.cond(inner_far, branch_free, checked)

    lax.fori_loop(0, nk // STAGE_GROUP, group, bias0)

    lp = lam_ref[...]
    lam = (jnp.exp(jnp.sum(lp[0:1] * lp[1:2], axis=1, keepdims=True))
           - jnp.exp(jnp.sum(lp[2:3] * lp[3:4], axis=1, keepdims=True)) + lambda_init)
    o1 = acc_scr[0:V_DIM, 0:tq] / acc_scr[V_DIM:V_DIM + 1, 0:tq]
    o2 = acc_scr[0:V_DIM, tq:2 * tq] / acc_scr[V_DIM:V_DIM + 1, tq:2 * tq]
    o = o1 - lam * o2
    o = o * lax.rsqrt(jnp.mean(o * o, axis=0, keepdims=True) + LN_EPS)
    o = o * sg_ref[...] * (1.0 - lambda_init)
    o_ref[...] = jnp.transpose(o).astype(BF16)


def _attention(qt, k, vt, positions, bmin, bmax, order, rel_table, lam_params, subln_g, lambda_init):
    D, S = qt.shape
    heads = D // V_DIM
    tq, tk = TQ, TK
    nk = S // tk
    assert S % tq == 0 and nk % STAGE_GROUP == 0, (S, tq, tk)
    table2 = rel_table.astype(F32) * LOG2_E
    table_rows = jnp.zeros((heads, LANES), F32).at[:, :REL_BUCKETS].set(table2.T)
    kernel = functools.partial(_attn_kernel, nk=nk, lambda_init=lambda_init)
    grid_spec = pltpu.PrefetchScalarGridSpec(
        num_scalar_prefetch=4,
        grid=(heads, S // tq),
        in_specs=[
            pl.BlockSpec((V_DIM, tq), lambda h, i, *_: (h, i)),
            pl.BlockSpec((S, V_DIM), lambda h, i, *_: (0, h)),
            pl.BlockSpec((1, nk, V_DIM + ONES_ROWS, tk), lambda h, i, *_: (h, 0, 0, 0)),
            pl.BlockSpec((1, tq), lambda h, i, *_: (0, i)),
            pl.BlockSpec((nk, 1, tk), lambda h, i, *_: (0, 0, 0)),
            pl.BlockSpec((heads, LANES), lambda h, i, *_: (0, 0)),
            pl.BlockSpec((4, HEAD_DIM), lambda h, i, *_: (0, 0)),
            pl.BlockSpec((V_DIM, 1), lambda h, i, *_: (0, 0)),
        ],
        out_specs=pl.BlockSpec((tq, V_DIM), lambda h, i, *_: (i, h)),
        scratch_shapes=[
            pltpu.VMEM((V_DIM, 2 * tq), BF16),
            pltpu.VMEM((tk, 2 * tq), F32),
            pltpu.VMEM((tk, 2 * tq), F32),
            pltpu.VMEM((1, 2 * tq), F32),
            pltpu.VMEM((1, 2 * tq), F32),
            pltpu.VMEM((1, 2 * tq), F32),
            pltpu.VMEM((V_DIM + ONES_ROWS, 2 * tq), F32),
        ],
    )
    return pl.pallas_call(
        kernel,
        grid_spec=grid_spec,
        out_shape=jax.ShapeDtypeStruct((S, D), BF16),
        compiler_params=_params("parallel", "parallel"),
        name="diff_attention",
    )(bmin, bmax, table2.reshape(-1), order,
      qt, k, vt, positions.reshape(1, S), positions.reshape(nk, 1, tk), table_rows,
      lam_params, subln_g.reshape(V_DIM, 1))


def _proj_ln_kernel(a_ref, x_ref, w_ref, lg_ref, lb_ref, o_ref):
    z = DEEPNORM_ALPHA * x_ref[...] + _dot(a_ref[...], w_ref[...])
    o_ref[...] = _layer_norm(z, lg_ref[...], lb_ref[...])


def _proj_ln(a, x, w, ln_g, ln_b):
    S, D = x.shape
    K = a.shape[1]
    tm = TM_PROJ
    return pl.pallas_call(
        _proj_ln_kernel,
        grid=(S // tm,),
        in_specs=[
            pl.BlockSpec((tm, K), lambda i: (i, 0)),
            pl.BlockSpec((tm, D), lambda i: (i, 0)),
            _const_spec((K, D)),
            _const_spec((1, D)),
            _const_spec((1, D)),
        ],
        out_specs=pl.BlockSpec((tm, D), lambda i: (i, 0)),
        out_shape=jax.ShapeDtypeStruct((S, D), F32),
        compiler_params=_params("parallel"),
        name="attn_out_proj",
    )(a, x, w.astype(BF16), ln_g.reshape(1, D), ln_b.reshape(1, D))


def _attn_layer(x, positions, bmin, bmax, order, rel_table, w_qkv, lam_params, subln_g, w_out,
                lambda_init, ln_g, ln_b):
    qt, k, vt = _qkv(x, w_qkv)
    o = _attention(qt, k, vt, positions, bmin, bmax, order, rel_table, lam_params, subln_g,
                   lambda_init)
    return _proj_ln(o, x, w_out, ln_g, ln_b)


def _ffn_kernel(x_ref, xp_ref, xn_ref, wi_ref, cw_ref, cb_ref, wd_ref, lg_ref, lb_ref,
                o_ref, g_scr):
    i = pl.program_id(0)
    x = x_ref[...]
    tm = x.shape[0]
    halo = xp_ref.shape[0]
    F = wd_ref.shape[0]
    xp = jnp.where(i > 0, xp_ref[...], 0.0)
    xn = jnp.where(i < pl.num_programs(0) - 1, xn_ref[...], 0.0)
    xe = jnp.concatenate([xp, x, xn], axis=0).astype(BF16)
    g_scr[...] = _dot(xe, wi_ref[:, 0:F])
    cw = cw_ref[...]
    gate = (g_scr[halo - 1:halo - 1 + tm, :] * cw[0:1]
            + g_scr[halo:halo + tm, :] * cw[1:2]
            + g_scr[halo + 1:halo + 1 + tm, :] * cw[2:3]
            + cb_ref[...])
    up = _dot(x.astype(BF16), wi_ref[:, F:2 * F])
    hidden = (_gelu(gate) * up).astype(BF16)
    z = DEEPNORM_ALPHA * x + _dot(hidden, wd_ref[...])
    o_ref[...] = _layer_norm(z, lg_ref[...], lb_ref[...])


def _ffn_layer(x, w_in, conv_w, conv_b, w_down, ln_g, ln_b):
    S, D = x.shape
    F = w_down.shape[0]
    tm, halo = TM_FFN, SUBLANES
    blocks_per_tile = tm // halo
    last_halo_block = S // halo - 1
    return pl.pallas_call(
        _ffn_kernel,
        grid=(S // tm,),
        in_specs=[
            pl.BlockSpec((tm, D), lambda i: (i, 0)),
            pl.BlockSpec((halo, D), lambda i: (jnp.maximum(i * blocks_per_tile - 1, 0), 0)),
            pl.BlockSpec((halo, D),
                         lambda i: (jnp.minimum((i + 1) * blocks_per_tile, last_halo_block), 0)),
            _const_spec((D, 2 * F)),
            _const_spec((conv_w.shape[0], F)),
            _const_spec((1, F)),
            _const_spec((F, D)),
            _const_spec((1, D)),
            _const_spec((1, D)),
        ],
        out_specs=pl.BlockSpec((tm, D), lambda i: (i, 0)),
        out_shape=jax.ShapeDtypeStruct((S, D), F32),
        scratch_shapes=[pltpu.VMEM((tm + 2 * halo, F), F32)],
        compiler_params=_params("parallel"),
        name="conv_glu_ffn",
    )(x, x, x, w_in.astype(BF16), conv_w, conv_b.reshape(1, F), w_down.astype(BF16),
      ln_g.reshape(1, D), ln_b.reshape(1, D))


def kernel(x, positions, rel_bias_table, a_w_in, a_norm_g, a_norm_b, a_w_s, a_b_s, a_w_out,
           b_w_qkv, b_lambda, b_subln_g, b_w_out, f_w_in, f_conv_w, f_conv_b, f_w_down,
           ln_g, ln_b):
    B, S, D = x.shape
    outs = []
    for b in range(B):
        xs = x[b]
        pos = positions[b]
        bmin, bmax = _pos_stats(pos)
        order = _tile_order(bmin, bmax, S // TQ, S // TK)
        for i in range(DEPTH):
            j = i // 2
            if i % 2 == 0:
                xs = _gmlp_layer(xs, a_w_in[j], a_norm_g[j], a_norm_b[j], a_w_s[j], a_b_s[j],
                                 a_w_out[j], ln_g[i, 0], ln_b[i, 0])
            else:
                xs = _attn_layer(xs, pos, bmin, bmax, order, rel_bias_table, b_w_qkv[j], b_lambda[j],
                                 b_subln_g[j], b_w_out[j], _lambda_init(i), ln_g[i, 0], ln_b[i, 0])
            xs = _ffn_layer(xs, f_w_in[i], f_conv_w[i], f_conv_b[i], f_w_down[i],
                            ln_g[i, 1], ln_b[i, 1])
        outs.append(xs)
    return jnp.stack(outs)
```

```python
import functools
import math

import jax
import jax.numpy as jnp
from jax import lax
from jax.experimental import pallas as pl
from jax.experimental.pallas import tpu as pltpu

F32 = jnp.float32
BF16 = jnp.bfloat16

DEPTH = 4
A_CHUNK = 128
A_GROUPS = 8
HEAD_DIM = 64
V_DIM = 2 * HEAD_DIM
REL_BUCKETS = 32
REL_FAR = 128
LN_EPS = 1e-5
LOG2_E = math.log2(math.e)
MASKED_LOGIT = -1e30
DEEPNORM_ALPHA = (2 * DEPTH) ** 0.25
REL_CLIP = 2047
F32_MANTISSA_BITS = 23
F32_EXP_BIAS = 127

LANES = 128
SUBLANES = 8
BF16_ROWS = 16
VMEM_BYTES = 64 * 1024 * 1024
VMEM_LIMIT = VMEM_BYTES * 3 // 4

TM_GMLP = 512
TQ = 512
TK = 256
TM_QKV = TK
TM_PROJ = 512
TM_FFN = 512
STAGE_GROUP = 16
ONES_ROWS = BF16_ROWS


def _lambda_init(layer_idx):
    return 0.8 - 0.6 * math.exp(-0.3 * layer_idx)


def _gelu(x):
    return 0.5 * x * (1.0 + lax.erf(x * (1.0 / math.sqrt(2.0))))


def _layer_norm(z, g, b):
    mu = jnp.mean(z, axis=-1, keepdims=True)
    zc = z - mu
    var = jnp.mean(zc * zc, axis=-1, keepdims=True)
    return zc * lax.rsqrt(var + LN_EPS) * g + b


def _dot(a, b):
    return jnp.dot(a, b, preferred_element_type=F32)


def _dot_nt(a, b):
    return lax.dot_general(a, b, (((1,), (1,)), ((), ())), preferred_element_type=F32)


def _const_spec(shape):
    nd = len(shape)
    return pl.BlockSpec(shape, lambda *_: (0,) * nd, pipeline_mode=pl.Buffered(1))


def _params(*sem):
    return pltpu.CompilerParams(dimension_semantics=sem, vmem_limit_bytes=VMEM_LIMIT)


def _gmlp_kernel(x_ref, wi_ref, ng_ref, nb_ref, ws_ref, bs_ref, wo_ref, lg_ref, lb_ref, o_ref):
    x = x_ref[...]
    tm = x.shape[0]
    W = wo_ref.shape[0]
    gd = W // A_GROUPS
    hidden = _gelu(_dot(x.astype(BF16), wi_ref[...]))
    v = _layer_norm(hidden[:, W:], ng_ref[...], nb_ref[...]).astype(BF16)
    mixed = []
    for c in range(tm // A_CHUNK):
        row = []
        for g in range(A_GROUPS):
            vc = v[c * A_CHUNK:(c + 1) * A_CHUNK, g * gd:(g + 1) * gd]
            row.append(_dot(ws_ref[g], vc) + bs_ref[g])
        mixed.append(jnp.concatenate(row, axis=1))
    y = (hidden[:, :W] * jnp.concatenate(mixed, axis=0)).astype(BF16)
    z = DEEPNORM_ALPHA * x + _dot(y, wo_ref[...])
    o_ref[...] = _layer_norm(z, lg_ref[...], lb_ref[...])


def _gmlp_layer(x, w_in, norm_g, norm_b, w_s, b_s, w_out, ln_g, ln_b):
    S, D = x.shape
    W = w_out.shape[0]
    tm = TM_GMLP
    return pl.pallas_call(
        _gmlp_kernel,
        grid=(S // tm,),
        in_specs=[
            pl.BlockSpec((tm, D), lambda i: (i, 0)),
            _const_spec((D, 2 * W)),
            _const_spec((1, W)),
            _const_spec((1, W)),
            _const_spec((A_GROUPS, A_CHUNK, A_CHUNK)),
            _const_spec((A_GROUPS, A_CHUNK, 1)),
            _const_spec((W, D)),
            _const_spec((1, D)),
            _const_spec((1, D)),
        ],
        out_specs=pl.BlockSpec((tm, D), lambda i: (i, 0)),
        out_shape=jax.ShapeDtypeStruct((S, D), F32),
        compiler_params=_params("parallel"),
        name="gmlp_layer",
    )(x, w_in.astype(BF16), norm_g.reshape(1, W), norm_b.reshape(1, W), w_s.astype(BF16),
      b_s.reshape(A_GROUPS, A_CHUNK, 1), w_out.astype(BF16), ln_g.reshape(1, D), ln_b.reshape(1, D))


def _pos_stats_kernel(p_ref, mn_ref, mx_ref):
    p = p_ref[...]
    mn_ref[...] = jnp.min(p, axis=1, keepdims=True)
    mx_ref[...] = jnp.max(p, axis=1, keepdims=True)


def _pos_stats(positions):
    nb = positions.shape[0] // LANES
    mn, mx = pl.pallas_call(
        _pos_stats_kernel,
        out_shape=(jax.ShapeDtypeStruct((nb, 1), jnp.int32),) * 2,
        name="pos_stats",
    )(positions.reshape(nb, LANES))
    return mn.reshape(nb), mx.reshape(nb)


def _tile_order_kernel(bmin_ref, bmax_ref, order_ref, near_ref, *, nq, nk, q_blocks, k_blocks):
    n_groups = nk // STAGE_GROUP

    def per_query_tile(qi, carry):
        qmin = bmin_ref[qi * q_blocks]
        qmax = bmax_ref[qi * q_blocks]
        for r in range(1, q_blocks):
            qmin = jnp.minimum(qmin, bmin_ref[qi * q_blocks + r])
            qmax = jnp.maximum(qmax, bmax_ref[qi * q_blocks + r])

        def near(kt):
            kmin = bmin_ref[kt * k_blocks]
            kmax = bmax_ref[kt * k_blocks]
            for r in range(1, k_blocks):
                kmin = jnp.minimum(kmin, bmin_ref[kt * k_blocks + r])
                kmax = jnp.maximum(kmax, bmax_ref[kt * k_blocks + r])
            far = jnp.logical_or(kmin - qmax >= REL_FAR, kmax - qmin <= -REL_FAR)
            return jnp.logical_not(far).astype(jnp.int32)

        def count(kt, total):
            near_ref[kt] = near(kt)
            return total + near_ref[kt]

        n_near = lax.fori_loop(0, nk, count, 0)
        reorder = n_near <= n_groups

        def place(kt, state):
            near_seen, far_slot = state
            is_near = near_ref[kt]
            slot = jnp.where(is_near == 1, near_seen * STAGE_GROUP, far_slot)
            order_ref[qi * nk + jnp.where(reorder, slot, kt)] = kt
            nxt = far_slot + 1
            held = jnp.logical_and(nxt % STAGE_GROUP == 0, nxt // STAGE_GROUP < n_near)
            nxt = jnp.where(held, nxt + 1, nxt)
            return near_seen + is_near, jnp.where(is_near == 1, far_slot, nxt)

        lax.fori_loop(0, nk, place, (0, jnp.where(n_near > 0, 1, 0)))
        return carry

    lax.fori_loop(0, nq, per_query_tile, 0)


def _tile_order(bmin, bmax, nq, nk):
    smem = pl.BlockSpec(memory_space=pltpu.SMEM)
    kernel = functools.partial(_tile_order_kernel, nq=nq, nk=nk,
                               q_blocks=TQ // LANES, k_blocks=TK // LANES)
    return pl.pallas_call(
        kernel,
        in_specs=[smem, smem],
        out_specs=smem,
        out_shape=jax.ShapeDtypeStruct((nq * nk,), jnp.int32),
        scratch_shapes=[pltpu.SMEM((nk,), jnp.int32)],
        name="tile_order",
    )(bmin, bmax)


def _qkv_kernel(x_ref, wqt_ref, wk_ref, wvt_ref, qt_ref, k_ref, vt_ref):
    xb = x_ref[...].astype(BF16)
    tm = xb.shape[0]
    qt_ref[...] = _dot_nt(wqt_ref[...], xb).astype(BF16)
    k_ref[...] = _dot(xb, wk_ref[...]).astype(BF16)
    vt = _dot_nt(wvt_ref[...], xb).astype(BF16)
    heads = vt.shape[0] // V_DIM
    vt_ref[:, 0, 0:V_DIM, :] = vt.reshape(heads, V_DIM, tm)
    vt_ref[:, 0, V_DIM:, :] = jnp.ones((heads, ONES_ROWS, tm), BF16)


def _qkv(x, w_qkv):
    S, D = x.shape
    tm = TM_QKV
    heads = D // V_DIM
    wqt = (w_qkv[:, :D] * (HEAD_DIM ** -0.5 * LOG2_E)).T.astype(BF16)
    wk = w_qkv[:, D:2 * D].astype(BF16)
    wvt = w_qkv[:, 2 * D:].T.astype(BF16)
    return pl.pallas_call(
        _qkv_kernel,
        grid=(S // tm,),
        in_specs=[
            pl.BlockSpec((tm, D), lambda i: (i, 0)),
            _const_spec((D, D)),
            _const_spec((D, D)),
            _const_spec((D, D)),
        ],
        out_specs=[
            pl.BlockSpec((D, tm), lambda i: (0, i)),
            pl.BlockSpec((tm, D), lambda i: (i, 0)),
            pl.BlockSpec((heads, 1, V_DIM + ONES_ROWS, tm), lambda i: (0, i, 0, 0)),
        ],
        out_shape=[
            jax.ShapeDtypeStruct((D, S), BF16),
            jax.ShapeDtypeStruct((S, D), BF16),
            jax.ShapeDtypeStruct((heads, S // tm, V_DIM + ONES_ROWS, tm), BF16),
        ],
        compiler_params=_params("parallel"),
        name="qkv_proj",
    )(x, wqt, wk, wvt)


def _bias_tile(posk_row, posq_row, table_row):
    tk = posk_row.shape[1]
    tq = posq_row.shape[1]
    table_sq = jnp.broadcast_to(table_row, (LANES, LANES))
    rows = []
    for c in range(tk // LANES):
        pk = posk_row[:, c * LANES:(c + 1) * LANES]
        pk_col = jnp.transpose(jnp.broadcast_to(pk, (LANES, LANES)))
        cols = []
        for d in range(tq // LANES):
            rel = pk_col - posq_row[:, d * LANES:(d + 1) * LANES]
            n = jnp.abs(rel)
            nsq = jnp.square(jnp.minimum(n, REL_CLIP)).astype(F32)
            log2_nsq = (lax.bitcast_convert_type(nsq, jnp.int32) >> F32_MANTISSA_BITS) - F32_EXP_BIAS
            large = jnp.minimum(log2_nsq + 2, REL_BUCKETS // 2 - 1)
            bucket = jnp.where(n < REL_BUCKETS // 4, n, large)
            bucket = bucket + jnp.where(rel > 0, REL_BUCKETS // 2, 0)
            cols.append(jnp.take_along_axis(table_sq, bucket, axis=1))
        rows.append(jnp.concatenate(cols, axis=1))
    return jnp.concatenate(rows, axis=0)


def _attn_kernel(bmin_ref, bmax_ref, tbl_ref, order_ref,
                 qt_ref, k_ref, vt_ref, posq_ref, posk_ref, tblv_ref, lam_ref, sg_ref,
                 o_ref, qcat, s_a, s_b, smax_a, smax_b, m_scr, acc_scr, *, nk, lambda_init):
    h = pl.program_id(0)
    qi = pl.program_id(1)
    tq = qt_ref.shape[1]
    tk = vt_ref.shape[3]
    heads = pl.num_programs(0)

    q = qt_ref[...]
    row = lax.broadcasted_iota(jnp.int32, q.shape, 0)
    zero = jnp.zeros_like(q)
    qcat[:, 0:tq] = jnp.where(row < HEAD_DIM, q, zero)
    qcat[:, tq:2 * tq] = jnp.where(row >= HEAD_DIM, q, zero)

    qmin = bmin_ref[qi * (tq // LANES)]
    qmax = bmax_ref[qi * (tq // LANES)]
    for r in range(1, tq // LANES):
        qmin = jnp.minimum(qmin, bmin_ref[qi * (tq // LANES) + r])
        qmax = jnp.maximum(qmax, bmax_ref[qi * (tq // LANES) + r])
    bias_before = tbl_ref[(REL_BUCKETS // 2 - 1) * heads + h]
    bias_after = tbl_ref[(REL_BUCKETS - 1) * heads + h]

    acc_scr[...] = jnp.zeros_like(acc_scr)
    m_scr[...] = jnp.full(m_scr.shape, MASKED_LOGIT, F32)

    def classify(kt):
        kmin = bmin_ref[kt * (tk // LANES)]
        kmax = bmax_ref[kt * (tk // LANES)]
        for r in range(1, tk // LANES):
            kmin = jnp.minimum(kmin, bmin_ref[kt * (tk // LANES) + r])
            kmax = jnp.maximum(kmax, bmax_ref[kt * (tk // LANES) + r])
        all_after = kmin - qmax >= REL_FAR
        all_before = kmax - qmin <= -REL_FAR
        near = jnp.logical_not(jnp.logical_or(all_after, all_before))
        const_bias = jnp.where(all_after, bias_after, jnp.where(all_before, bias_before, 0.0))
        return near, const_bias

    def logits(kt, s_ref, smax_ref):
        kk = k_ref[pl.ds(pl.multiple_of(kt * tk, tk), tk), :]
        s = _dot(kk, qcat[...])
        s_ref[...] = s
        smax_ref[...] = jnp.max(s, axis=0, keepdims=True)

    def add_near_bias(kt, s_ref, smax_ref):
        bias = _bias_tile(posk_ref[kt], posq_ref[...], tblv_ref[pl.ds(h, 1), :])
        s = s_ref[...] + jnp.concatenate([bias, bias], axis=1)
        s_ref[...] = s
        smax_ref[...] = jnp.max(s, axis=0, keepdims=True)

    def softmax_update(kt, s_ref, smax_ref, const_bias):
        m_old = m_scr[...]
        m_new = jnp.maximum(m_old, smax_ref[...] + const_bias)
        p = jnp.exp2(s_ref[...] - (m_new - const_bias)).astype(BF16)
        acc_scr[...] = jnp.exp2(m_old - m_new) * acc_scr[...] + _dot(vt_ref[0, kt], p)
        m_scr[...] = m_new

    def tile_at(step):
        return order_ref[qi * nk + step]

    def stage(step, cur, nxt, const_bias, fix_next):
        kt = tile_at(step)
        kn = tile_at(jnp.minimum(step + 1, nk - 1))
        logits(kn, *nxt)
        softmax_update(kt, *cur, const_bias)
        near_next, bias_next = classify(kn)
        if fix_next:
            @pl.when(near_next)
            def _():
                add_near_bias(kn, *nxt)
        return near_next, bias_next

    buf_a = (s_a, smax_a)
    buf_b = (s_b, smax_b)
    near0, bias0 = classify(tile_at(0))
    logits(tile_at(0), *buf_a)

    @pl.when(near0)
    def _():
        add_near_bias(tile_at(0), *buf_a)

    def pair(step, const_bias):
        _, const_bias = stage(step, buf_a, buf_b, const_bias, True)
        _, const_bias = stage(step + 1, buf_b, buf_a, const_bias, True)
        return const_bias

    def group(j, const_bias):
        base = j * STAGE_GROUP
        inner_far = jnp.bool_(True)
        for i in range(1, STAGE_GROUP):
            inner_far = jnp.logical_and(inner_far, jnp.logical_not(classify(tile_at(base + i))[0]))

        def branch_free():
            c = const_bias
            for i in range(STAGE_GROUP):
                cur, nxt = (buf_a, buf_b) if i % 2 == 0 else (buf_b, buf_a)
                _, c = stage(base + i, cur, nxt, c, i == STAGE_GROUP - 1)
            return c

        def checked():
            return lax.fori_loop(0, STAGE_GROUP // 2, lambda t, c: pair(base + 2 * t, c), const_bias)

        return lax.cond(inner_far, branch_free, checked)

    lax.fori_loop(0, nk // STAGE_GROUP, group, bias0)

    lp = lam_ref[...]
    lam = (jnp.exp(jnp.sum(lp[0:1] * lp[1:2], axis=1, keepdims=True))
           - jnp.exp(jnp.sum(lp[2:3] * lp[3:4], axis=1, keepdims=True)) + lambda_init)
    inv_sum = 1.0 / acc_scr[V_DIM:V_DIM + 1, :]
    o1 = acc_scr[0:V_DIM, 0:tq] * inv_sum[:, 0:tq]
    o2 = acc_scr[0:V_DIM, tq:2 * tq] * inv_sum[:, tq:2 * tq]
    o = o1 - lam * o2
    o = o * lax.rsqrt(jnp.mean(o * o, axis=0, keepdims=True) + LN_EPS)
    o = o * sg_ref[...] * (1.0 - lambda_init)
    o_ref[...] = jnp.transpose(o).astype(BF16)


def _attention(qt, k, vt, positions, bmin, bmax, order, rel_table, lam_params, subln_g, lambda_init):
    D, S = qt.shape
    heads = D // V_DIM
    tq, tk = TQ, TK
    nk = S // tk
    assert S % tq == 0 and nk % STAGE_GROUP == 0, (S, tq, tk)
    table2 = rel_table.astype(F32) * LOG2_E
    table_rows = jnp.zeros((heads, LANES), F32).at[:, :REL_BUCKETS].set(table2.T)
    kernel = functools.partial(_attn_kernel, nk=nk, lambda_init=lambda_init)
    grid_spec = pltpu.PrefetchScalarGridSpec(
        num_scalar_prefetch=4,
        grid=(heads, S // tq),
        in_specs=[
            pl.BlockSpec((V_DIM, tq), lambda h, i, *_: (h, i)),
            pl.BlockSpec((S, V_DIM), lambda h, i, *_: (0, h)),
            pl.BlockSpec((1, nk, V_DIM + ONES_ROWS, tk), lambda h, i, *_: (h, 0, 0, 0)),
            pl.BlockSpec((1, tq), lambda h, i, *_: (0, i)),
            pl.BlockSpec((nk, 1, tk), lambda h, i, *_: (0, 0, 0)),
            pl.BlockSpec((heads, LANES), lambda h, i, *_: (0, 0)),
            pl.BlockSpec((4, HEAD_DIM), lambda h, i, *_: (0, 0)),
            pl.BlockSpec((V_DIM, 1), lambda h, i, *_: (0, 0)),
        ],
        out_specs=pl.BlockSpec((tq, V_DIM), lambda h, i, *_: (i, h)),
        scratch_shapes=[
            pltpu.VMEM((V_DIM, 2 * tq), BF16),
            pltpu.VMEM((tk, 2 * tq), F32),
            pltpu.VMEM((tk, 2 * tq), F32),
            pltpu.VMEM((1, 2 * tq), F32),
            pltpu.VMEM((1, 2 * tq), F32),
            pltpu.VMEM((1, 2 * tq), F32),
            pltpu.VMEM((V_DIM + ONES_ROWS, 2 * tq), F32),
        ],
    )
    return pl.pallas_call(
        kernel,
        grid_spec=grid_spec,
        out_shape=jax.ShapeDtypeStruct((S, D), BF16),
        compiler_params=_params("parallel", "parallel"),
        name="diff_attention",
    )(bmin, bmax, table2.reshape(-1), order,
      qt, k, vt, positions.reshape(1, S), positions.reshape(nk, 1, tk), table_rows,
      lam_params, subln_g.reshape(V_DIM, 1))


def _proj_ln_kernel(a_ref, x_ref, w_ref, lg_ref, lb_ref, o_ref):
    z = DEEPNORM_ALPHA * x_ref[...] + _dot(a_ref[...], w_ref[...])
    o_ref[...] = _layer_norm(z, lg_ref[...], lb_ref[...])


def _proj_ln(a, x, w, ln_g, ln_b):
    S, D = x.shape
    K = a.shape[1]
    tm = TM_PROJ
    return pl.pallas_call(
        _proj_ln_kernel,
        grid=(S // tm,),
        in_specs=[
            pl.BlockSpec((tm, K), lambda i: (i, 0)),
            pl.BlockSpec((tm, D), lambda i: (i, 0)),
            _const_spec((K, D)),
            _const_spec((1, D)),
            _const_spec((1, D)),
        ],
        out_specs=pl.BlockSpec((tm, D), lambda i: (i, 0)),
        out_shape=jax.ShapeDtypeStruct((S, D), F32),
        compiler_params=_params("parallel"),
        name="attn_out_proj",
    )(a, x, w.astype(BF16), ln_g.reshape(1, D), ln_b.reshape(1, D))


def _attn_layer(x, positions, bmin, bmax, order, rel_table, w_qkv, lam_params, subln_g, w_out,
                lambda_init, ln_g, ln_b):
    qt, k, vt = _qkv(x, w_qkv)
    o = _attention(qt, k, vt, positions, bmin, bmax, order, rel_table, lam_params, subln_g,
                   lambda_init)
    return _proj_ln(o, x, w_out, ln_g, ln_b)


def _ffn_kernel(x_ref, xp_ref, xn_ref, wi_ref, cw_ref, cb_ref, wd_ref, lg_ref, lb_ref,
                o_ref, g_scr):
    i = pl.program_id(0)
    x = x_ref[...]
    tm = x.shape[0]
    halo = xp_ref.shape[0]
    F = wd_ref.shape[0]
    xp = jnp.where(i > 0, xp_ref[...], 0.0)
    xn = jnp.where(i < pl.num_programs(0) - 1, xn_ref[...], 0.0)
    xe = jnp.concatenate([xp, x, xn], axis=0).astype(BF16)
    g_scr[...] = _dot(xe, wi_ref[:, 0:F])
    cw = cw_ref[...]
    gate = (g_scr[halo - 1:halo - 1 + tm, :] * cw[0:1]
            + g_scr[halo:halo + tm, :] * cw[1:2]
            + g_scr[halo + 1:halo + 1 + tm, :] * cw[2:3]
            + cb_ref[...])
    up = _dot(x.astype(BF16), wi_ref[:, F:2 * F])
    hidden = (_gelu(gate) * up).astype(BF16)
    z = DEEPNORM_ALPHA * x + _dot(hidden, wd_ref[...])
    o_ref[...] = _layer_norm(z, lg_ref[...], lb_ref[...])


def _ffn_layer(x, w_in, conv_w, conv_b, w_down, ln_g, ln_b):
    S, D = x.shape
    F = w_down.shape[0]
    tm, halo = TM_FFN, SUBLANES
    blocks_per_tile = tm // halo
    last_halo_block = S // halo - 1
    return pl.pallas_call(
        _ffn_kernel,
        grid=(S // tm,),
        in_specs=[
            pl.BlockSpec((tm, D), lambda i: (i, 0)),
            pl.BlockSpec((halo, D), lambda i: (jnp.maximum(i * blocks_per_tile - 1, 0), 0)),
            pl.BlockSpec((halo, D),
                         lambda i: (jnp.minimum((i + 1) * blocks_per_tile, last_halo_block), 0)),
            _const_spec((D, 2 * F)),
            _const_spec((conv_w.shape[0], F)),
            _const_spec((1, F)),
            _const_spec((F, D)),
            _const_spec((1, D)),
            _const_spec((1, D)),
        ],
        out_specs=pl.BlockSpec((tm, D), lambda i: (i, 0)),
        out_shape=jax.ShapeDtypeStruct((S, D), F32),
        scratch_shapes=[pltpu.VMEM((tm + 2 * halo, F), F32)],
        compiler_params=_params("parallel"),
        name="conv_glu_ffn",
    )(x, x, x, w_in.astype(BF16), conv_w, conv_b.reshape(1, F), w_down.astype(BF16),
      ln_g.reshape(1, D), ln_b.reshape(1, D))


def kernel(x, positions, rel_bias_table, a_w_in, a_norm_g, a_norm_b, a_w_s, a_b_s, a_w_out,
           b_w_qkv, b_lambda, b_subln_g, b_w_out, f_w_in, f_conv_w, f_conv_b, f_w_down,
           ln_g, ln_b):
    B, S, D = x.shape
    outs = []
    for b in range(B):
        xs = x[b]
        pos = positions[b]
        bmin, bmax = _pos_stats(pos)
        order = _tile_order(bmin, bmax, S // TQ, S // TK)
        for i in range(DEPTH):
            j = i // 2
            if i % 2 == 0:
                xs = _gmlp_layer(xs, a_w_in[j], a_norm_g[j], a_norm_b[j], a_w_s[j], a_b_s[j],
                                 a_w_out[j], ln_g[i, 0], ln_b[i, 0])
            else:
                xs = _attn_layer(xs, pos, bmin, bmax, order, rel_bias_table, b_w_qkv[j], b_lambda[j],
                                 b_subln_g[j], b_w_out[j], _lambda_init(i), ln_g[i, 0], ln_b[i, 0])
            xs = _ffn_layer(xs, f_w_in[i], f_conv_w[i], f_conv_b[i], f_w_down[i],
                            ln_g[i, 1], ln_b[i, 1])
        outs.append(xs)
    return jnp.stack(outs)
```
